```python
import math
import jax, jax.numpy as jnp
from jax import lax
import numpy as np

D_MODEL = 2048
BATCH = 8
SEQ = 2048
DEPTH = 4

GRID_W = 64
CTX_LEN = 256
N_MIXERS = 3
N_ATTN_LAYERS = (DEPTH + 2) // 3
N_RET_LAYERS = (DEPTH + 1) // 3
N_HYENA_LAYERS = DEPTH // 3
N_MOD = 6
EPS = 1e-6

ATTN_HEADS = 16
ATTN_KV_HEADS = 4
ATTN_HEAD_DIM = D_MODEL // ATTN_HEADS
Q_BLOCK = 128
ROPE_THETA = 10000.0

RET_HEADS = 8
RET_QK_DIM = D_MODEL // RET_HEADS
RET_V_DIM = 2 * RET_QK_DIM
RET_CHUNK = 128

HYENA_EMB = 33
HYENA_FILTER_WIDTH = 64
HYENA_TARGET = 1e-2
HYENA_FAST = 0.3
HYENA_SLOW = 1.5
HYENA_SHIFT = 0.0

N_EXPERTS = 16
CAPACITY_FACTOR = 2
EXPERT_FF = D_MODEL // 2

kernel_name = 'hybrid_interleaved_dit_trunk'


def rms_norm(x, w):
    xf = x.astype(jnp.float32)
    y = xf * lax.rsqrt(jnp.mean(xf * xf, axis=-1, keepdims=True) + EPS)
    return (y * w.astype(jnp.float32)).astype(x.dtype)


def modulate(h, shift, scale):
    return h * (1 + scale) + shift


def axial_rope_angles(seq_len, dim):
    rows = seq_len // GRID_W
    row_id = jnp.repeat(jnp.arange(rows, dtype=jnp.float32), GRID_W)
    col_id = jnp.tile(jnp.arange(GRID_W, dtype=jnp.float32), rows)
    n_freq = dim // 4
    inv = ROPE_THETA ** (-jnp.arange(n_freq, dtype=jnp.float32) / n_freq)
    ang = jnp.concatenate([row_id[:, None] * inv, col_id[:, None] * inv], axis=-1)
    return jnp.cos(ang), jnp.sin(ang)


def apply_axial_rope(x, cos, sin):
    b, l, h, d = x.shape
    nf = d // 4
    xr = x.reshape(b, l, h, 2, 2, nf)
    x1, x2 = xr[..., 0, :], xr[..., 1, :]
    cos = cos.reshape(l, 1, 2, nf).astype(x.dtype)
    sin = sin.reshape(l, 1, 2, nf).astype(x.dtype)
    out = jnp.stack([x1 * cos - x2 * sin, x1 * sin + x2 * cos], axis=-2)
    return out.reshape(b, l, h, d)


def gqa_attend(q, k, v):
    b, lq, h, hd = q.shape
    kvh = k.shape[2]
    g = h // kvh
    nb = lq // Q_BLOCK
    qb = q.reshape(b, nb, Q_BLOCK, kvh, g, hd).transpose(1, 0, 3, 4, 2, 5)
    kt = k.transpose(0, 2, 1, 3)
    vt = v.transpose(0, 2, 1, 3)
    scale = hd ** -0.5

    def one_block(qi):
        s = jnp.einsum('bkgqd,bksd->bkgqs', qi, kt).astype(jnp.float32) * scale
        p = jax.nn.softmax(s, axis=-1).astype(vt.dtype)
        return jnp.einsum('bkgqs,bksd->bkgqd', p, vt)

    o = lax.map(one_block, qb)
    return o.transpose(1, 0, 4, 2, 3, 5).reshape(b, lq, h * hd)


def attention_mixer(h_lat, h_ctx, w_qkv, q_norm, k_norm, w_o, with_ctx_out):
    b, l, _ = h_lat.shape
    H, KV, hd = ATTN_HEADS, ATTN_KV_HEADS, ATTN_HEAD_DIM
    nq, nkv = H * hd, KV * hd
    p = h_lat @ w_qkv
    ql = rms_norm(p[..., :nq].reshape(b, l, H, hd), q_norm)
    kl = rms_norm(p[..., nq:nq + nkv].reshape(b, l, KV, hd), k_norm)
    vl = p[..., nq + nkv:].reshape(b, l, KV, hd)
    cos, sin = axial_rope_angles(l, hd)
    ql = apply_axial_rope(ql, cos, sin)
    kl = apply_axial_rope(kl, cos, sin)
    pc = h_ctx @ w_qkv[:, nq:]
    kc = rms_norm(pc[..., :nkv].reshape(b, -1, KV, hd), k_norm)
    vc = pc[..., nkv:].reshape(b, -1, KV, hd)
    k_all = jnp.concatenate([kc, kl], axis=1)
    v_all = jnp.concatenate([vc, vl], axis=1)
    out_lat = gqa_attend(ql, k_all, v_all) @ w_o
    out_ctx = None
    if with_ctx_out:
        qc = rms_norm((h_ctx @ w_qkv[:, :nq]).reshape(b, -1, H, hd), q_norm)
        out_ctx = gqa_attend(qc, kc, vc) @ w_o
    return out_ctx, out_lat


def retention_scan(q, k, v, log_gamma, state0, inclusive):
    b, l, h, _ = q.shape
    dv = v.shape[-1]
    C = RET_CHUNK
    n = l // C

    def chunks(t):
        return t.reshape(b, n, C, h, t.shape[-1]).transpose(1, 0, 3, 2, 4)

    idx = jnp.arange(C, dtype=jnp.float32)
    diff = idx[:, None] - idx[None, :]
    causal = diff >= 0 if inclusive else diff > 0
    lg = log_gamma[:, None, None]
    dmask = jnp.where(causal, jnp.exp(jnp.where(causal, diff, 0.0) * lg), 0.0).astype(q.dtype)
    q_dec = jnp.exp((idx + 1.0) * log_gamma[:, None])[..., None].astype(q.dtype)
    k_dec = jnp.exp((C - 1.0 - idx) * log_gamma[:, None])[..., None].astype(q.dtype)
    c_dec = jnp.exp(C * log_gamma)[:, None, None].astype(q.dtype)

    def step(S, inp):
        qc, kc, vc = inp
        inner = jnp.einsum('bhqd,bhkd->bhqk', qc, kc) * dmask
        o = jnp.einsum('bhqk,bhkv->bhqv', inner, vc) + jnp.einsum('bhqd,bhdv->bhqv', qc * q_dec, S)
        S = S * c_dec + jnp.einsum('bhkd,bhkv->bhdv', kc * k_dec, vc)
        return S, o

    S, o = lax.scan(step, state0, (chunks(q), chunks(k), chunks(v)))
    return o.transpose(1, 0, 3, 2, 4).reshape(b, l, h, dv), S


def retention_mixer(h_lat, h_ctx, w_in, decay_logit, w_o, with_ctx_out):
    H, dk, dv = RET_HEADS, RET_QK_DIM, RET_V_DIM
    b, l, _ = h_lat.shape
    log_g = jax.nn.log_sigmoid(decay_logit.astype(jnp.float32))

    def project(hx, rope):
        p = hx @ w_in
        q, k, v, g = jnp.split(p, [H * dk, 2 * H * dk, 2 * H * dk + H * dv], axis=-1)
        q = q.reshape(b, -1, H, dk)
        k = k.reshape(b, -1, H, dk) * (dk ** -0.5)
        v = v.reshape(b, -1, H, dv)
        if rope is not None:
            q = apply_axial_rope(q, *rope)
            k = apply_axial_rope(k, *rope)
        return q, k, v, g

    def flip(t):
        return t[:, ::-1]

    def readout(o, g):
        of = o.astype(jnp.float32)
        of = of * lax.rsqrt(jnp.mean(of * of, axis=-1, keepdims=True) + EPS)
        return (jax.nn.silu(g) * of.astype(g.dtype).reshape(b, -1, H * dv)) @ w_o

    qc, kc, vc, gc = project(h_ctx, None)
    zero = jnp.zeros((b, H, dk, dv), qc.dtype)
    oc_f, sc_f = retention_scan(qc, kc, vc, log_g[0], zero, True)
    oc_b, sc_b = retention_scan(flip(qc), flip(kc), flip(vc), log_g[1], zero, False)
    ql, kl, vl, gl = project(h_lat, axial_rope_angles(l, dk))
    ol_f, _ = retention_scan(ql, kl, vl, log_g[0], sc_f, True)
    ol_b, _ = retention_scan(flip(ql), flip(kl), flip(vl), log_g[1], sc_b, False)
    out_lat = readout(ol_f + flip(ol_b), gl)
    out_ctx = readout(oc_f + flip(oc_b), gc) if with_ctx_out else None
    return out_ctx, out_lat


def hyena_filter(seq_len, w1, b1, fr1, w2, b2, fr2, w3):
    f32 = jnp.float32
    t = jnp.linspace(0.0, 1.0, seq_len, dtype=f32)[:, None]
    bands = (HYENA_EMB - 1) // 2
    w = 2.0 * math.pi * jnp.arange(seq_len, dtype=f32)[:, None] / seq_len
    f = jnp.linspace(1e-4, bands - 1, bands, dtype=f32)[None, :]
    z = jnp.concatenate([t, jnp.cos(f * w), -jnp.sin(f * w)], axis=-1)
    h = jnp.sin(fr1.astype(f32) * (z @ w1.astype(f32) + b1.astype(f32)))
    h = jnp.sin(fr2.astype(f32) * (h @ w2.astype(f32) + b2.astype(f32)))
    h = h @ w3.astype(f32)
    deltas = jnp.abs(jnp.linspace(math.log(HYENA_TARGET) / HYENA_SLOW, math.log(HYENA_TARGET) / HYENA_FAST, D_MODEL, dtype=f32))
    decay = jnp.exp(-t * deltas)
    h = h.reshape(seq_len, 2, D_MODEL) * (decay[:, None, :] + HYENA_SHIFT)
    h_f, h_b = h[:, 0], h[:, 1]
    taps = jnp.concatenate([h_f, jnp.zeros((1, D_MODEL), f32), h_b[:0:-1]], axis=0)
    return taps / jnp.sum(jnp.abs(taps), axis=0, keepdims=True)


def hyena_operator(hx, w_in, conv_w, conv_b, taps, skip):
    b, l, _ = hx.shape
    z = hx @ w_in
    zp = jnp.pad(z, ((0, 0), (1, 1), (0, 0)))
    z = zp[:, :-2] * conv_w[0] + zp[:, 1:-1] * conv_w[1] + zp[:, 2:] * conv_w[2] + conv_b
    x0, x1, v = jnp.split(z, 3, axis=-1)
    u = (v * x1).astype(jnp.float32)
    n = 2 * l
    y = jnp.fft.irfft(jnp.fft.rfft(u, n=n, axis=1) * jnp.fft.rfft(taps, n=n, axis=0)[None], n=n, axis=1)[:, :l]
    y = (y + u * skip.astype(jnp.float32)).astype(hx.dtype)
    return y * x0


def hyena_mixer(h_lat, h_ctx, w_in, conv_w, conv_b, f_w1, f_b1, f_fr1, f_w2, f_b2, f_fr2, f_w3, skip, w_out, with_ctx_out):
    fparams = (f_w1, f_b1, f_fr1, f_w2, f_b2, f_fr2, f_w3)
    out_lat = hyena_operator(h_lat, w_in, conv_w, conv_b, hyena_filter(h_lat.shape[1], *fparams), skip) @ w_out
    out_ctx = None
    if with_ctx_out:
        out_ctx = hyena_operator(h_ctx, w_in, conv_w, conv_b, hyena_filter(h_ctx.shape[1], *fparams), skip) @ w_out
    return out_ctx, out_lat


def expert_choice_moe(h, w_router, w_gate, w_up, w_down):
    b, n, d = h.shape
    cap = CAPACITY_FACTOR * n // N_EXPERTS
    probs = jax.nn.softmax((h @ w_router).astype(jnp.float32), axis=-1)
    gate, idx = lax.top_k(jnp.swapaxes(probs, 1, 2), cap)
    xg = jax.vmap(lambda hb, ib: hb[ib])(h, idx)
    a = jnp.einsum('becd,edf->becf', xg, w_gate)
    u = jnp.einsum('becd,edf->becf', xg, w_up)
    y = jnp.einsum('becf,efd->becd', jax.nn.silu(a) * u, w_down) * gate[..., None].astype(h.dtype)
    return jax.vmap(lambda ib, yb: jnp.zeros((n, d), yb.dtype).at[ib.reshape(-1)].add(yb.reshape(-1, d)))(idx, y)


def setup_inputs(seed: int = 0) -> dict:
    key = jax.random.key(seed)
    keys = jax.random.split(key, 32)
    counter = iter(range(32))
    f32 = jnp.float32

    def nrm(shape, scale):
        return jax.random.normal(keys[next(counter)], shape, f32) * scale

    D = D_MODEL
    qkv_w = (ATTN_HEADS + 2 * ATTN_KV_HEADS) * ATTN_HEAD_DIM
    ret_w = 2 * RET_HEADS * RET_QK_DIM + 2 * RET_HEADS * RET_V_DIM
    decay_init = jnp.log(2.0 ** (5.0 + jnp.arange(RET_HEADS, dtype=f32)) - 1.0)
    return {
        'x': nrm((BATCH, SEQ, D), 1.0),
        'c': nrm((BATCH, D), 1.0),
        'ctx': nrm((BATCH, CTX_LEN, D), 1.0),
        'c_ctx': nrm((D,), 1.0),
        'w_mod': nrm((DEPTH, D, N_MOD * D), 0.5 * D ** -0.5),
        'b_mod': nrm((DEPTH, N_MOD * D), 0.02),
        'norm_w': 1.0 + nrm((DEPTH, 2, D), 0.02),
        'attn_w_qkv': nrm((N_ATTN_LAYERS, D, qkv_w), D ** -0.5),
        'attn_q_norm': 1.0 + nrm((N_ATTN_LAYERS, ATTN_HEAD_DIM), 0.02),
        'attn_k_norm': 1.0 + nrm((N_ATTN_LAYERS, ATTN_HEAD_DIM), 0.02),
        'attn_w_o': nrm((N_ATTN_LAYERS, ATTN_HEADS * ATTN_HEAD_DIM, D), (ATTN_HEADS * ATTN_HEAD_DIM) ** -0.5),
        'ret_w_in': nrm((N_RET_LAYERS, D, ret_w), D ** -0.5),
        'ret_decay_logit': decay_init[None, None, :] + nrm((N_RET_LAYERS, 2, RET_HEADS), 0.1),
        'ret_w_o': nrm((N_RET_LAYERS, RET_HEADS * RET_V_DIM, D), (RET_HEADS * RET_V_DIM) ** -0.5),
        'hy_w_in': nrm((N_HYENA_LAYERS, D, 3 * D), D ** -0.5),
        'hy_conv_w': nrm((N_HYENA_LAYERS, 3, 3 * D), 3 ** -0.5),
        'hy_conv_b': nrm((N_HYENA_LAYERS, 3 * D), 0.02),
        'hy_f_w1': nrm((N_HYENA_LAYERS, HYENA_EMB, HYENA_FILTER_WIDTH), HYENA_EMB ** -0.5),
        'hy_f_b1': nrm((N_HYENA_LAYERS, HYENA_FILTER_WIDTH), 0.02),
        'hy_f_freq1': 1.0 + nrm((N_HYENA_LAYERS, HYENA_FILTER_WIDTH), 0.02),
        'hy_f_w2': nrm((N_HYENA_LAYERS, HYENA_FILTER_WIDTH, HYENA_FILTER_WIDTH), HYENA_FILTER_WIDTH ** -0.5),
        'hy_f_b2': nrm((N_HYENA_LAYERS, HYENA_FILTER_WIDTH), 0.02),
        'hy_f_freq2': 1.0 + nrm((N_HYENA_LAYERS, HYENA_FILTER_WIDTH), 0.02),
        'hy_f_w3': nrm((N_HYENA_LAYERS, HYENA_FILTER_WIDTH, 2 * D), HYENA_FILTER_WIDTH ** -0.5),
        'hy_skip': nrm((N_HYENA_LAYERS, D), 0.5),
        'hy_w_out': nrm((N_HYENA_LAYERS, D, D), D ** -0.5),
        'moe_router': nrm((DEPTH, D, N_EXPERTS), D ** -0.5),
        'moe_w_gate': nrm((DEPTH, N_EXPERTS, D, EXPERT_FF), D ** -0.5),
        'moe_w_up': nrm((DEPTH, N_EXPERTS, D, EXPERT_FF), D ** -0.5),
        'moe_w_down': nrm((DEPTH, N_EXPERTS, EXPERT_FF, D), EXPERT_FF ** -0.5),
        'final_norm_w': 1.0 + nrm((D,), 0.02),
    }


def reference(x, c, ctx, c_ctx, w_mod, b_mod, norm_w, attn_w_qkv, attn_q_norm, attn_k_norm, attn_w_o, ret_w_in, ret_decay_logit, ret_w_o, hy_w_in, hy_conv_w, hy_conv_b, hy_f_w1, hy_f_b1, hy_f_freq1, hy_f_w2, hy_f_b2, hy_f_freq2, hy_f_w3, hy_skip, hy_w_out, moe_router, moe_w_gate, moe_w_up, moe_w_down, final_norm_w):
    for i in range(DEPTH):
        kind = i % N_MIXERS
        j = i // N_MIXERS
        with_ctx = i < DEPTH - 1
        mod_lat = (jax.nn.silu(c) @ w_mod[i] + b_mod[i])[:, None, :]
        mod_ctx = jax.nn.silu(c_ctx) @ w_mod[i] + b_mod[i]
        sh1, sc1, g1, sh2, sc2, g2 = jnp.split(mod_lat, N_MOD, axis=-1)
        csh1, csc1, cg1, csh2, csc2, cg2 = jnp.split(mod_ctx, N_MOD, axis=-1)
        h_lat = modulate(rms_norm(x, norm_w[i, 0]), sh1, sc1)
        h_ctx = modulate(rms_norm(ctx, norm_w[i, 0]), csh1, csc1)
        if kind == 0:
            o_ctx, o_lat = attention_mixer(h_lat, h_ctx, attn_w_qkv[j], attn_q_norm[j], attn_k_norm[j], attn_w_o[j], with_ctx)
        elif kind == 1:
            o_ctx, o_lat = retention_mixer(h_lat, h_ctx, ret_w_in[j], ret_decay_logit[j], ret_w_o[j], with_ctx)
        else:
            o_ctx, o_lat = hyena_mixer(h_lat, h_ctx, hy_w_in[j], hy_conv_w[j], hy_conv_b[j], hy_f_w1[j], hy_f_b1[j], hy_f_freq1[j], hy_f_w2[j], hy_f_b2[j], hy_f_freq2[j], hy_f_w3[j], hy_skip[j], hy_w_out[j], with_ctx)
        x = x + g1 * o_lat
        h2 = modulate(rms_norm(x, norm_w[i, 1]), sh2, sc2)
        x = x + g2 * expert_choice_moe(h2, moe_router[i], moe_w_gate[i], moe_w_up[i], moe_w_down[i])
        if with_ctx:
            ctx = ctx + cg1 * o_ctx
            hc2 = modulate(rms_norm(ctx, norm_w[i, 1]), csh2, csc2)
            ctx = ctx + cg2 * expert_choice_moe(hc2, moe_router[i], moe_w_gate[i], moe_w_up[i], moe_w_down[i])
    return rms_norm(x, final_norm_w)
```

```python
import functools
import math

import jax
import jax.numpy as jnp
from jax import lax
from jax.experimental import pallas as pl
from jax.experimental.pallas import tpu as pltpu

F32 = jnp.float32
BF16 = jnp.bfloat16

EPS = 1e-6
GRID_W = 64
ROPE_THETA = 10000.0
ATTN_HEADS = 16
ATTN_KV_HEADS = 4
ATTN_HEAD_DIM = 128
RET_HEADS = 8
RET_QK_DIM = 256
RET_V_DIM = 512
N_EXPERTS = 16
CAPACITY_FACTOR = 2
N_MOD = 6
HYENA_EMB = 33
HYENA_TARGET = 1e-2
HYENA_FAST = 0.3
HYENA_SLOW = 1.5
HYENA_SHIFT = 0.0

LANE = 128
VMEM_LIMIT = 52 * 1024 * 1024


def _cparams(*sem):
    return pltpu.CompilerParams(dimension_semantics=sem, vmem_limit_bytes=VMEM_LIMIT)


def _silu(v):
    return v / (1.0 + jnp.exp(-v))


def _split_bf16(v):
    hi = v.astype(BF16)
    lo = (v - hi.astype(F32)).astype(BF16)
    return hi, lo


def _dot(a, b):
    return jnp.dot(a, b, preferred_element_type=F32)


def _dot3(a, b):
    ah, al = _split_bf16(a)
    bh, bl = _split_bf16(b)
    return _dot(ah, bh) + _dot(al, bh) + _dot(ah, bl)


def _dot_nt(a, b):
    return lax.dot_general(a, b, (((1,), (1,)), ((), ())), preferred_element_type=F32)


def _dot_tn(a, b):
    return lax.dot_general(a, b, (((0,), (0,)), ((), ())), preferred_element_type=F32)


def _mod_kernel(c_ref, w_ref, b_ref, o_ref):
    a = _silu(c_ref[...])
    o_ref[...] = _dot3(a, w_ref[...]) + b_ref[...]


def _modulation(c_all, w_mod, b_mod):
    depth, d, n = w_mod.shape
    rows = c_all.shape[0]
    tn = 512
    return pl.pallas_call(
        _mod_kernel,
        grid=(depth, n // tn),
        in_specs=[
            pl.BlockSpec((rows, d), lambda i, j: (0, 0)),
            pl.BlockSpec((None, d, tn), lambda i, j: (i, 0, j)),
            pl.BlockSpec((None, 1, tn), lambda i, j: (i, 0, j)),
        ],
        out_specs=pl.BlockSpec((None, rows, tn), lambda i, j: (i, 0, j)),
        out_shape=jax.ShapeDtypeStruct((depth, rows, n), F32),
        compiler_params=_cparams("parallel", "parallel"),
        name="modulation",
    )(c_all, w_mod, b_mod.reshape(depth, 1, n))


def _norm_mod(x, nw, sh, sc):
    ms = jnp.mean(x * x, axis=-1, keepdims=True)
    y = x * lax.rsqrt(ms + EPS) * nw
    return y * (1.0 + sc) + sh


def _rope_slices(acc, colw, cos, sin, *, head_norm, half, table_w):
    tn = acc.shape[1]
    lane = lax.broadcasted_iota(jnp.int32, (1, LANE), 1)
    first_half = (lane % (2 * half)) < half
    outs = []
    for s in range(tn // LANE):
        xs = acc[:, s * LANE:(s + 1) * LANE]
        if head_norm:
            xs = xs * lax.rsqrt(jnp.mean(xs * xs, axis=-1, keepdims=True) + EPS)
        xs = xs * colw[:, s * LANE:(s + 1) * LANE]
        t0 = (s * LANE) % table_w
        cs = cos[:, t0:t0 + LANE]
        sn = sin[:, t0:t0 + LANE]
        if 2 * half == LANE:
            partner = pltpu.roll(xs, half, axis=1)
        else:
            partner = jnp.where(first_half, pltpu.roll(xs, LANE - half, axis=1), pltpu.roll(xs, half, axis=1))
        outs.append(xs * cs + partner * sn)
    return jnp.concatenate(outs, axis=1)


def _proj_kernel(x_ref, nw_ref, sh_ref, sc_ref, w_ref, colw_ref, cos_ref, sin_ref, o_ref, h_scr,
                 *, rope_blocks, col_off, head_norm, half, table_w):
    j = pl.program_id(2)

    @pl.when(j == 0)
    def _():
        h_scr[...] = _norm_mod(x_ref[...], nw_ref[...], sh_ref[...], sc_ref[...]).astype(BF16)

    acc = _dot(h_scr[...], w_ref[...].astype(BF16))
    if rope_blocks == 0:
        o_ref[...] = acc.astype(o_ref.dtype)
    else:
        @pl.when(j + col_off < rope_blocks)
        def _():
            o_ref[...] = _rope_slices(acc, colw_ref[...], cos_ref[...], sin_ref[...], head_norm=head_norm,
                                      half=half, table_w=table_w).astype(o_ref.dtype)

        @pl.when(j + col_off >= rope_blocks)
        def _():
            o_ref[...] = acc.astype(o_ref.dtype)


def _project(x, nw, sh, sc, w, *, colw=None, cos=None, sin=None, rope_blocks=0, col_off=0, head_norm=False,
             half=32, tn=512, tm=512):
    b, l, d = x.shape
    n = w.shape[1] - col_off * tn
    tm = min(tm, l)
    per_sample = sh.shape[0] != 1
    mod_map = (lambda bi, i, j: (bi, 0, 0)) if per_sample else (lambda bi, i, j: (0, 0, 0))
    if colw is None:
        colw = jnp.ones((1, w.shape[1]), F32)
        cos = jnp.ones((l, LANE), F32)
        sin = jnp.zeros((l, LANE), F32)
    table_w = cos.shape[1]
    kern = functools.partial(_proj_kernel, rope_blocks=rope_blocks, col_off=col_off, head_norm=head_norm,
                             half=half, table_w=table_w)
    return pl.pallas_call(
        kern,
        grid=(b, l // tm, n // tn),
        in_specs=[
            pl.BlockSpec((None, tm, d), lambda bi, i, j: (bi, i, 0)),
            pl.BlockSpec((1, d), lambda bi, i, j: (0, 0)),
            pl.BlockSpec((None, 1, d), mod_map),
            pl.BlockSpec((None, 1, d), mod_map),
            pl.BlockSpec((d, tn), lambda bi, i, j: (0, j + col_off)),
            pl.BlockSpec((1, tn), lambda bi, i, j: (0, j + col_off)),
            pl.BlockSpec((tm, table_w), lambda bi, i, j: (i, 0)),
            pl.BlockSpec((tm, table_w), lambda bi, i, j: (i, 0)),
        ],
        out_specs=pl.BlockSpec((None, tm, tn), lambda bi, i, j: (bi, i, j)),
        out_shape=jax.ShapeDtypeStruct((b, l, n), BF16),
        scratch_shapes=[pltpu.VMEM((tm, d), BF16)],
        compiler_params=_cparams("parallel", "parallel", "arbitrary"),
        name="norm_mod_project",
    )(x, nw.reshape(1, d), sh, sc, w, colw, cos, sin)


def _out_proj_kernel(a_ref, w_ref, r_ref, g_ref, o_ref):
    acc = _dot(a_ref[...], w_ref[...].astype(BF16))
    o_ref[...] = r_ref[...] + g_ref[...] * acc


def _out_project(a, w, resid, gate, *, tm=512, tn=512):
    b, l, k = a.shape
    n = w.shape[1]
    tm = min(tm, l)
    per_sample = gate.shape[0] != 1
    g_map = (lambda bi, i, j: (bi, 0, j)) if per_sample else (lambda bi, i, j: (0, 0, j))
    return pl.pallas_call(
        _out_proj_kernel,
        grid=(b, l // tm, n // tn),
        in_specs=[
            pl.BlockSpec((None, tm, k), lambda bi, i, j: (bi, i, 0)),
            pl.BlockSpec((k, tn), lambda bi, i, j: (0, j)),
            pl.BlockSpec((None, tm, tn), lambda bi, i, j: (bi, i, j)),
            pl.BlockSpec((None, 1, tn), g_map),
        ],
        out_specs=pl.BlockSpec((None, tm, tn), lambda bi, i, j: (bi, i, j)),
        out_shape=jax.ShapeDtypeStruct((b, l, n), F32),
        compiler_params=_cparams("parallel", "parallel", "parallel"),
        name="out_project_residual",
    )(a, w, resid, gate)


def _attn_kernel(q_ref, k_ref, v_ref, o_ref, *, groups):
    k = k_ref[...]
    v = v_ref[...]
    for g in range(groups):
        q = q_ref[:, g * ATTN_HEAD_DIM:(g + 1) * ATTN_HEAD_DIM]
        s = _dot_nt(q, k)
        m = jnp.max(s, axis=-1, keepdims=True)
        p = jnp.exp(s - m)
        den = jnp.sum(p, axis=-1, keepdims=True)
        o = _dot(p.astype(BF16), v) / den
        o_ref[:, g * ATTN_HEAD_DIM:(g + 1) * ATTN_HEAD_DIM] = o.astype(o_ref.dtype)


def _attention(q, k, v, *, tq=256):
    b, lq, _ = q.shape
    lk = k.shape[1]
    groups = ATTN_HEADS // ATTN_KV_HEADS
    gw = groups * ATTN_HEAD_DIM
    tq = min(tq, lq)
    return pl.pallas_call(
        functools.partial(_attn_kernel, groups=groups),
        grid=(b, ATTN_KV_HEADS, lq // tq),
        in_specs=[
            pl.BlockSpec((None, tq, gw), lambda bi, h, i: (bi, i, h)),
            pl.BlockSpec((None, lk, ATTN_HEAD_DIM), lambda bi, h, i: (bi, 0, h)),
            pl.BlockSpec((None, lk, ATTN_HEAD_DIM), lambda bi, h, i: (bi, 0, h)),
        ],
        out_specs=pl.BlockSpec((None, tq, gw), lambda bi, h, i: (bi, i, h)),
        out_shape=jax.ShapeDtypeStruct(q.shape, BF16),
        compiler_params=_cparams("parallel", "parallel", "parallel"),
        name="gqa_attention",
    )(q, k, v)


RET_CHUNK = 256


def _ret_readout(o, g):
    of = o * lax.rsqrt(jnp.mean(o * o, axis=-1, keepdims=True) + EPS)
    return (_silu(g.astype(F32)) * of).astype(BF16)


def _ret_kernel(lg_ref, qc_ref, kc_ref, vc_ref, gc_ref, ql_ref, kl_ref, vl_ref, gl_ref, oc_ref, ol_ref,
                ob_scr, sf_scr, sb_scr, *, n_chunks, with_ctx_out):
    c = RET_CHUNK
    h = pl.program_id(1)
    lgf = lg_ref[0, h]
    lgb = lg_ref[1, h]
    row = lax.broadcasted_iota(jnp.int32, (c, 1), 0).astype(F32)
    col = lax.broadcasted_iota(jnp.int32, (1, c), 1).astype(F32)
    diff = row - col
    dmask = jnp.exp(jnp.where(diff >= 0, diff * lgf, -diff * lgb))
    qdec_f = jnp.exp((row + 1.0) * lgf)
    kdec_f = jnp.exp((c - 1.0 - row) * lgf)
    qdec_b = jnp.exp((c - row) * lgb)
    kdec_b = jnp.exp(row * lgb)
    one = jnp.ones((1, 1), F32)
    cdec_f = jnp.exp(one * (c * lgf))
    cdec_b = jnp.exp(one * (c * lgb))

    qc = qc_ref[...]
    kc = kc_ref[...].astype(F32)
    vc = vc_ref[...]
    sf_scr[...] = _dot_tn((kc * kdec_f).astype(BF16), vc)
    sb_scr[...] = _dot_tn((kc * kdec_b).astype(BF16), vc)
    if with_ctx_out:
        inner = (_dot_nt(qc, kc_ref[...]) * dmask).astype(BF16)
        oc_ref[...] = _ret_readout(_dot(inner, vc), gc_ref[...])
    else:
        oc_ref[...] = jnp.zeros(oc_ref.shape, oc_ref.dtype)

    def bwd(i, carry):
        ci = n_chunks - 1 - i
        sl = pl.ds(pl.multiple_of(ci * c, c), c)
        q = ql_ref[sl, :].astype(F32)
        k = kl_ref[sl, :].astype(F32)
        ob_scr[sl, :] = _dot((q * qdec_b).astype(BF16), sb_scr[...].astype(BF16))
        sb_scr[...] = sb_scr[...] * cdec_b + _dot_tn((k * kdec_b).astype(BF16), vl_ref[sl, :])
        return carry

    lax.fori_loop(0, n_chunks, bwd, 0)

    def fwd(ci, carry):
        sl = pl.ds(pl.multiple_of(ci * c, c), c)
        qb = ql_ref[sl, :]
        kb = kl_ref[sl, :]
        v = vl_ref[sl, :]
        q = qb.astype(F32)
        k = kb.astype(F32)
        inner = (_dot_nt(qb, kb) * dmask).astype(BF16)
        o = ob_scr[sl, :] + _dot(inner, v) + _dot((q * qdec_f).astype(BF16), sf_scr[...].astype(BF16))
        sf_scr[...] = sf_scr[...] * cdec_f + _dot_tn((k * kdec_f).astype(BF16), v)
        ol_ref[sl, :] = _ret_readout(o, gl_ref[sl, :])
        return carry

    lax.fori_loop(0, n_chunks, fwd, 0)


def _retention(log_g, p_ctx, p_lat, with_ctx_out):
    b, l, _ = p_lat.shape
    lc = p_ctx.shape[1]
    assert lc == RET_CHUNK and l % RET_CHUNK == 0
    dk, dv, hh = RET_QK_DIM, RET_V_DIM, RET_HEADS
    k_off = hh * dk // dk
    v_off = 2 * hh * dk // dv
    g_off = v_off + hh

    def specs(ln):
        return [
            pl.BlockSpec((None, ln, dk), lambda bi, h: (bi, 0, h)),
            pl.BlockSpec((None, ln, dk), lambda bi, h: (bi, 0, k_off + h)),
            pl.BlockSpec((None, ln, dv), lambda bi, h: (bi, 0, v_off + h)),
            pl.BlockSpec((None, ln, dv), lambda bi, h: (bi, 0, g_off + h)),
        ]

    return pl.pallas_call(
        functools.partial(_ret_kernel, n_chunks=l // RET_CHUNK, with_ctx_out=with_ctx_out),
        grid=(b, hh),
        in_specs=[pl.BlockSpec(memory_space=pltpu.SMEM)] + specs(lc) + specs(l),
        out_specs=[
            pl.BlockSpec((None, lc, dv), lambda bi, h: (bi, 0, h)),
            pl.BlockSpec((None, l, dv), lambda bi, h: (bi, 0, h)),
        ],
        out_shape=[
            jax.ShapeDtypeStruct((b, lc, hh * dv), BF16),
            jax.ShapeDtypeStruct((b, l, hh * dv), BF16),
        ],
        scratch_shapes=[pltpu.VMEM((l, dv), F32), pltpu.VMEM((dk, dv), F32), pltpu.VMEM((dk, dv), F32)],
        compiler_params=_cparams("parallel", "parallel"),
        name="retention",
    )(log_g, p_ctx, p_ctx, p_ctx, p_ctx, p_lat, p_lat, p_lat, p_lat)


def _hy_gate_kernel(x0_ref, x1_ref, v_ref, cw0_ref, cw1_ref, cwv_ref, cb0_ref, cb1_ref, cbv_ref, u_ref, g_ref):
    l = x0_ref.shape[0]
    t = lax.broadcasted_iota(jnp.int32, (l, 1), 0)

    def conv3(z_ref, cw_ref, cb_ref):
        z = z_ref[...].astype(F32)
        prev = jnp.where(t == 0, 0.0, pltpu.roll(z, 1, axis=0))
        nxt = jnp.where(t == l - 1, 0.0, pltpu.roll(z, l - 1, axis=0))
        cw = cw_ref[...]
        return prev * cw[0:1, :] + z * cw[1:2, :] + nxt * cw[2:3, :] + cb_ref[...]

    x1 = conv3(x1_ref, cw1_ref, cb1_ref)
    v = conv3(v_ref, cwv_ref, cbv_ref)
    u_ref[...] = (v * x1).astype(u_ref.dtype)
    g_ref[...] = conv3(x0_ref, cw0_ref, cb0_ref).astype(g_ref.dtype)


def _hyena_gate(z, conv_w, conv_b, *, tn=256):
    b, l, d3 = z.shape
    d = d3 // 3
    nb = d // tn
    cb = conv_b.reshape(1, d3)
    zs = [pl.BlockSpec((None, l, tn), (lambda bi, j, o=o: (bi, 0, j + o * nb))) for o in range(3)]
    ws = [pl.BlockSpec((3, tn), (lambda bi, j, o=o: (0, j + o * nb))) for o in range(3)]
    bs = [pl.BlockSpec((1, tn), (lambda bi, j, o=o: (0, j + o * nb))) for o in range(3)]
    return pl.pallas_call(
        _hy_gate_kernel,
        grid=(b, nb),
        in_specs=zs + ws + bs,
        out_specs=[pl.BlockSpec((None, l, tn), lambda bi, j: (bi, 0, j))] * 2,
        out_shape=[jax.ShapeDtypeStruct((b, l, d), BF16)] * 2,
        compiler_params=_cparams("parallel", "parallel"),
        name="hyena_conv3_gate",
    )(z, z, z, conv_w, conv_w, conv_w, cb, cb, cb)


def _hy_filter_kernel(z_ref, w1_ref, b1_ref, f1_ref, w2_ref, b2_ref, f2_ref, w3f_ref, w3b_ref, dec_ref, o_ref):
    l = z_ref.shape[0]
    h = jnp.sin(f1_ref[...] * (_dot3(z_ref[...], w1_ref[...]) + b1_ref[...]))
    h = jnp.sin(f2_ref[...] * (_dot3(h, w2_ref[...]) + b2_ref[...]))
    decay = dec_ref[...] + HYENA_SHIFT
    hf = _dot3(h, w3f_ref[...]) * decay
    hb = _dot3(h, w3b_ref[...]) * decay
    t = lax.broadcasted_iota(jnp.int32, (l, 1), 0)
    hb = jnp.where(t == 0, 0.0, hb)
    norm = jnp.sum(jnp.abs(hf), axis=0, keepdims=True) + jnp.sum(jnp.abs(hb), axis=0, keepdims=True)
    o_ref[0] = (hf / norm).astype(o_ref.dtype)
    o_ref[1] = (hb / norm).astype(o_ref.dtype)


def _hyena_filter(l, d, w1, b1, fr1, w2, b2, fr2, w3, *, tn=256):
    t = jnp.linspace(0.0, 1.0, l, dtype=F32)[:, None]
    bands = (HYENA_EMB - 1) // 2
    w = 2.0 * math.pi * jnp.arange(l, dtype=F32)[:, None] / l
    f = jnp.linspace(1e-4, bands - 1, bands, dtype=F32)[None, :]
    z = jnp.concatenate([t, jnp.cos(f * w), -jnp.sin(f * w)], axis=-1)
    deltas = jnp.abs(jnp.linspace(math.log(HYENA_TARGET) / HYENA_SLOW, math.log(HYENA_TARGET) / HYENA_FAST, d, dtype=F32))
    decay = jnp.exp(-t * deltas)
    fw = w1.shape[1]
    pad = lambda a, r, c: jnp.pad(a.astype(F32), ((0, r - a.shape[0]), (0, c - a.shape[1])))
    z = pad(z, l, LANE)
    w1p = pad(w1, LANE, LANE)
    w2p = pad(w2, LANE, LANE)
    w3p = pad(w3, LANE, 2 * d)
    vec = lambda a: pad(a.reshape(1, fw), 1, LANE)
    nb = d // tn
    full = lambda shape: pl.BlockSpec(shape, lambda j: (0, 0))
    return pl.pallas_call(
        _hy_filter_kernel,
        grid=(nb,),
        in_specs=[full((l, LANE)), full((LANE, LANE)), full((1, LANE)), full((1, LANE)), full((LANE, LANE)),
                  full((1, LANE)), full((1, LANE)),
                  pl.BlockSpec((LANE, tn), lambda j: (0, j)),
                  pl.BlockSpec((LANE, tn), lambda j: (0, j + nb)),
                  pl.BlockSpec((l, tn), lambda j: (0, j))],
        out_specs=pl.BlockSpec((2, l, tn), lambda j: (0, 0, j)),
        out_shape=jax.ShapeDtypeStruct((2, l, d), BF16),
        compiler_params=_cparams("parallel"),
        name="hyena_filter",
    )(z, w1p, vec(b1), vec(fr1), w2p, vec(b2), vec(fr2), w3p, w3p, decay)


def _dft_tables(l):
    n = 2 * l
    k = jnp.arange(l, dtype=jnp.int32)
    ang = ((k[:, None] * k[None, :]) % n).astype(F32) * (2.0 * math.pi / n)
    cs, sn = jnp.cos(ang), jnp.sin(ang)
    alt = jnp.where(k % 2 == 0, 1.0, -1.0).astype(F32)
    f_b = jnp.where(k[:, None] == 0, alt[None, :], -sn)
    fwd = jnp.concatenate([cs, f_b], axis=0)
    wa = jnp.where(k[None, :] == 0, 1.0, 2.0) * cs.T
    wb = jnp.where(k[None, :] == 0, alt[:, None], -2.0 * sn.T)
    inv = jnp.concatenate([wa, wb], axis=1) * (1.0 / n)
    return fwd.astype(BF16), inv.astype(BF16)


def _dft_raw_kernel(fa_ref, fb_ref, u_ref, o_ref):
    u = u_ref[...]
    o_ref[0] = _dot(fa_ref[...], u)
    o_ref[1] = _dot(fb_ref[...], u)


def _dft_raw(fwd, u, *, tm=512, tn=512):
    b, l, d = u.shape
    tm, tn = min(tm, l), min(tn, d)
    nb = l // tm
    return pl.pallas_call(
        _dft_raw_kernel,
        grid=(b, d // tn, nb),
        in_specs=[pl.BlockSpec((tm, l), lambda bi, j, i: (i, 0)),
                  pl.BlockSpec((tm, l), lambda bi, j, i: (i + nb, 0)),
                  pl.BlockSpec((None, l, tn), lambda bi, j, i: (bi, 0, j))],
        out_specs=pl.BlockSpec((None, 2, tm, tn), lambda bi, j, i: (bi, 0, i, j)),
        out_shape=jax.ShapeDtypeStruct((b, 2, l, d), F32),
        compiler_params=_cparams("parallel", "parallel", "parallel"),
        name="hyena_dft_filter",
    )(fwd, fwd, u)


def _dft_mul_kernel(fa_ref, fb_ref, u_ref, hs_ref, o_ref):
    i = pl.program_id(2)
    u = u_ref[...]
    ua = _dot(fa_ref[...], u)
    ub = _dot(fb_ref[...], u)
    first = jnp.logical_and(lax.broadcasted_iota(jnp.int32, (ua.shape[0], 1), 0) == 0, i == 0)
    ha = hs_ref[0, 0] + hs_ref[1, 0]
    hb = jnp.where(first, hs_ref[0, 1] + hs_ref[1, 1], hs_ref[0, 1] - hs_ref[1, 1])
    pa = jnp.where(first, ua * ha, ua * ha - ub * hb)
    pb = jnp.where(first, ub * hb, ua * hb + ub * ha)
    o_ref[0] = pa.astype(o_ref.dtype)
    o_ref[1] = pb.astype(o_ref.dtype)


def _dft_mul(fwd, u, hspec, *, tm=512, tn=512):
    b, l, d = u.shape
    tm, tn = min(tm, l), min(tn, d)
    nb = l // tm
    out = pl.pallas_call(
        _dft_mul_kernel,
        grid=(b, d // tn, nb),
        in_specs=[pl.BlockSpec((tm, l), lambda bi, j, i: (i, 0)),
                  pl.BlockSpec((tm, l), lambda bi, j, i: (i + nb, 0)),
                  pl.BlockSpec((None, l, tn), lambda bi, j, i: (bi, 0, j)),
                  pl.BlockSpec((2, 2, tm, tn), lambda bi, j, i: (0, 0, i, j))],
        out_specs=pl.BlockSpec((None, 2, tm, tn), lambda bi, j, i: (bi, 0, i, j)),
        out_shape=jax.ShapeDtypeStruct((b, 2, l, d), BF16),
        compiler_params=_cparams("parallel", "parallel", "parallel"),
        name="hyena_dft_forward",
    )(fwd, fwd, u, hspec)
    return out.reshape(b, 2 * l, d)


def _idft_kernel(g_ref, p_ref, u_ref, x0_ref, skip_ref, o_ref):
    y = _dot(g_ref[...], p_ref[...]) + u_ref[...].astype(F32) * skip_ref[...]
    o_ref[...] = (y * x0_ref[...].astype(F32)).astype(o_ref.dtype)


def _idft_gate(inv, p, u, x0, skip, *, tm=512, tn=512):
    b, l, d = u.shape
    tm, tn = min(tm, l), min(tn, d)
    return pl.pallas_call(
        _idft_kernel,
        grid=(b, d // tn, l // tm),
        in_specs=[pl.BlockSpec((tm, 2 * l), lambda bi, j, i: (i, 0)),
                  pl.BlockSpec((None, 2 * l, tn), lambda bi, j, i: (bi, 0, j)),
                  pl.BlockSpec((None, tm, tn), lambda bi, j, i: (bi, i, j)),
                  pl.BlockSpec((None, tm, tn), lambda bi, j, i: (bi, i, j)),
                  pl.BlockSpec((1, tn), lambda bi, j, i: (0, j))],
        out_specs=pl.BlockSpec((None, tm, tn), lambda bi, j, i: (bi, i, j)),
        out_shape=jax.ShapeDtypeStruct((b, l, d), BF16),
        compiler_params=_cparams("parallel", "parallel", "parallel"),
        name="hyena_dft_inverse",
    )(inv, p, u, x0, skip.reshape(1, d))


def _hyena_operator(z, conv_w, conv_b, fparams, skip):
    b, l, d3 = z.shape
    d = d3 // 3
    u, x0 = _hyena_gate(z, conv_w, conv_b)
    taps = _hyena_filter(l, d, *fparams)
    fwd, inv = _dft_tables(l)
    hspec = _dft_raw(fwd, taps)
    p = _dft_mul(fwd, u, hspec)
    return _idft_gate(inv, p, u, x0, skip)


def _router_kernel(x_ref, nw_ref, sh_ref, sc_ref, wr_ref, h_ref, lg_ref):
    h = _norm_mod(x_ref[...], nw_ref[...], sh_ref[...], sc_ref[...])
    h_ref[...] = h.astype(BF16)
    lg_ref[...] = _dot3(h, wr_ref[...])


def _norm_mod_router(x, nw, sh, sc, w_router, *, tm=512):
    b, l, d = x.shape
    n_e = w_router.shape[1]
    e = LANE
    w_router = jnp.pad(w_router, ((0, 0), (0, e - n_e)))
    tm = min(tm, l)
    per_sample = sh.shape[0] != 1
    mod_map = (lambda bi, i: (bi, 0, 0)) if per_sample else (lambda bi, i: (0, 0, 0))
    return pl.pallas_call(
        _router_kernel,
        grid=(b, l // tm),
        in_specs=[
            pl.BlockSpec((None, tm, d), lambda bi, i: (bi, i, 0)),
            pl.BlockSpec((1, d), lambda bi, i: (0, 0)),
            pl.BlockSpec((None, 1, d), mod_map),
            pl.BlockSpec((None, 1, d), mod_map),
            pl.BlockSpec((d, e), lambda bi, i: (0, 0)),
        ],
        out_specs=[pl.BlockSpec((None, tm, d), lambda bi, i: (bi, i, 0)),
                   pl.BlockSpec((None, tm, e), lambda bi, i: (bi, i, 0))],
        out_shape=[jax.ShapeDtypeStruct((b, l, d), BF16), jax.ShapeDtypeStruct((b, l, e), F32)],
        compiler_params=_cparams("parallel", "parallel"),
        name="norm_mod_router",
    )(x, nw.reshape(1, d), sh, sc, w_router)


def _route_kernel(lg_ref, tri_ref, pos_ref, prob_ref, *, cap):
    lg = lg_ref[...]
    m = jnp.max(lg, axis=0, keepdims=True)
    ex = jnp.exp(lg - m)
    probs = ex / jnp.sum(ex, axis=0, keepdims=True)
    bits = lax.bitcast_convert_type(probs, jnp.int32)

    def count(mask):
        return jnp.sum(jnp.where(mask, 1.0, 0.0), axis=1, keepdims=True)

    def step(i, thr):
        trial = thr | lax.shift_left(jnp.int32(1), 30 - i)
        return jnp.where(count(bits >= trial) >= cap, trial, thr)

    thr = lax.fori_loop(0, 31, step, jnp.zeros((lg.shape[0], 1), jnp.int32))
    gt = bits > thr
    eq = bits == thr
    need = cap - count(gt).astype(jnp.int32)
    both = jnp.concatenate([jnp.where(gt, 1.0, 0.0), jnp.where(eq, 1.0, 0.0)], axis=0).astype(BF16)
    csum = _dot(both, tri_ref[...])
    e = lg.shape[0]
    rank_gt = csum[:e].astype(jnp.int32)
    rank_eq = csum[e:].astype(jnp.int32)
    sel = jnp.logical_or(gt, jnp.logical_and(eq, rank_eq < need))
    pos = rank_gt + jnp.minimum(rank_eq, need)
    pos_ref[...] = jnp.where(sel, pos, -1)
    prob_ref[...] = probs


def _route(logits_t, cap):
    b, e, n = logits_t.shape
    idx = jnp.arange(n, dtype=jnp.int32)
    tri = (idx[:, None] < idx[None, :]).astype(BF16)
    return pl.pallas_call(
        functools.partial(_route_kernel, cap=cap),
        grid=(b,),
        in_specs=[pl.BlockSpec((None, e, n), lambda bi: (bi, 0, 0)),
                  pl.BlockSpec((n, n), lambda bi: (0, 0))],
        out_specs=[pl.BlockSpec((None, e, n), lambda bi: (bi, 0, 0))] * 2,
        out_shape=[jax.ShapeDtypeStruct((b, e, n), jnp.int32), jax.ShapeDtypeStruct((b, e, n), F32)],
        compiler_params=_cparams("parallel"),
        name="expert_choice_route",
    )(logits_t, tri)


def _gather_kernel(h_ref, pos_ref, prob_ref, xg_ref, gate_ref, *, cap):
    n = h_ref.shape[0]
    slot = lax.broadcasted_iota(jnp.int32, (cap, n), 0)
    match = slot == pos_ref[...]
    onehot = jnp.where(match, 1.0, 0.0).astype(BF16)
    xg_ref[...] = _dot(onehot, h_ref[...]).astype(xg_ref.dtype)
    gate_ref[...] = jnp.sum(jnp.where(match, prob_ref[...], 0.0), axis=1, keepdims=True)


def _moe_gather(h, pos, probs, cap):
    b, n, d = h.shape
    e = pos.shape[1]
    row = lambda a: a.reshape(b, e, 1, n)
    return pl.pallas_call(
        functools.partial(_gather_kernel, cap=cap),
        grid=(b, e),
        in_specs=[pl.BlockSpec((None, n, d), lambda bi, ei: (bi, 0, 0)),
                  pl.BlockSpec((None, None, 1, n), lambda bi, ei: (bi, ei, 0, 0)),
                  pl.BlockSpec((None, None, 1, n), lambda bi, ei: (bi, ei, 0, 0))],
        out_specs=[pl.BlockSpec((None, None, cap, d), lambda bi, ei: (ei, bi, 0, 0)),
                   pl.BlockSpec((None, None, cap, 1), lambda bi, ei: (ei, bi, 0, 0))],
        out_shape=[jax.ShapeDtypeStruct((e, b, cap, d), BF16), jax.ShapeDtypeStruct((e, b, cap, 1), F32)],
        compiler_params=_cparams("parallel", "parallel"),
        name="moe_gather",
    )(h, row(pos), row(probs))


def _ffn_kernel(x_ref, g_ref, wg_ref, wu_ref, wd_ref, o_ref, acc_ref):
    f = pl.program_id(2)
    x = x_ref[...]
    a = _dot(x, wg_ref[...].astype(BF16))
    u = _dot(x, wu_ref[...].astype(BF16))
    hmid = (_silu(a) * u * g_ref[...]).astype(BF16)
    y = _dot(hmid, wd_ref[...].astype(BF16))

    @pl.when(f == 0)
    def _():
        acc_ref[...] = y

    @pl.when(f > 0)
    def _():
        acc_ref[...] += y

    @pl.when(f == pl.num_programs(2) - 1)
    def _():
        o_ref[...] = acc_ref[...].astype(o_ref.dtype)


def _moe_ffn(xg, gate, w_gate, w_up, w_down, *, tm=1024, tf=256):
    e, m, d = xg.shape
    ff = w_gate.shape[2]
    tm = min(tm, m)
    return pl.pallas_call(
        _ffn_kernel,
        grid=(e, m // tm, ff // tf),
        in_specs=[pl.BlockSpec((None, tm, d), lambda ei, i, f: (ei, i, 0)),
                  pl.BlockSpec((None, tm, 1), lambda ei, i, f: (ei, i, 0)),
                  pl.BlockSpec((None, d, tf), lambda ei, i, f: (ei, 0, f)),
                  pl.BlockSpec((None, d, tf), lambda ei, i, f: (ei, 0, f)),
                  pl.BlockSpec((None, tf, d), lambda ei, i, f: (ei, f, 0))],
        out_specs=pl.BlockSpec((None, tm, d), lambda ei, i, f: (ei, i, 0)),
        out_shape=jax.ShapeDtypeStruct((e, m, d), BF16),
        scratch_shapes=[pltpu.VMEM((tm, d), F32)],
        compiler_params=_cparams("parallel", "parallel", "arbitrary"),
        name="moe_expert_ffn",
    )(xg, gate, w_gate, w_up, w_down)


def _combine_kernel(post_ref, y_ref, r_ref, g_ref, o_ref, pt_scr, *, cap):
    j = pl.program_id(2)
    tm, e = post_ref.shape

    @pl.when(j == 0)
    def _():
        post = post_ref[...]
        if cap % LANE == 0:
            slot = lax.broadcasted_iota(jnp.int32, (tm, cap), 1)
            for ei in range(e):
                pt_scr[:, ei * cap:(ei + 1) * cap] = jnp.where(post[:, ei:ei + 1] == slot, 1.0, 0.0).astype(BF16)
        else:
            slot = lax.broadcasted_iota(jnp.int32, (tm, e * cap), 1)
            hit = jnp.zeros((tm, e * cap), jnp.bool_)
            for ei in range(e):
                tgt = jnp.where(post[:, ei:ei + 1] >= 0, post[:, ei:ei + 1] + ei * cap, -1)
                hit = jnp.logical_or(hit, tgt == slot)
            pt_scr[...] = jnp.where(hit, 1.0, 0.0).astype(BF16)

    y = y_ref[...].reshape(e * cap, y_ref.shape[2])
    o_ref[...] = r_ref[...] + g_ref[...] * _dot(pt_scr[...], y)


def _moe_combine(pos_t, y, resid, gate, cap, *, tm=512, tn=512):
    b, n, e = pos_t.shape
    d = y.shape[3]
    tm = min(tm, n)
    per_sample = gate.shape[0] != 1
    g_map = (lambda bi, i, j: (bi, 0, j)) if per_sample else (lambda bi, i, j: (0, 0, j))
    return pl.pallas_call(
        functools.partial(_combine_kernel, cap=cap),
        grid=(b, n // tm, d // tn),
        in_specs=[pl.BlockSpec((None, tm, e), lambda bi, i, j: (bi, i, 0)),
                  pl.BlockSpec((e, None, cap, tn), lambda bi, i, j: (0, bi, 0, j)),
                  pl.BlockSpec((None, tm, tn), lambda bi, i, j: (bi, i, j)),
                  pl.BlockSpec((None, 1, tn), g_map)],
        out_specs=pl.BlockSpec((None, tm, tn), lambda bi, i, j: (bi, i, j)),
        out_shape=jax.ShapeDtypeStruct((b, n, d), F32),
        scratch_shapes=[pltpu.VMEM((tm, e * cap), BF16)],
        compiler_params=_cparams("parallel", "parallel", "arbitrary"),
        name="moe_combine_residual",
    )(pos_t, y, resid, gate)


def _moe_block(x, nw, sh, sc, gate, w_router, w_gate, w_up, w_down):
    b, n, d = x.shape
    e = w_router.shape[1]
    cap = CAPACITY_FACTOR * n // e
    h, logits = _norm_mod_router(x, nw, sh, sc, w_router)
    pos, probs = _route(jnp.swapaxes(logits[..., :e], 1, 2), cap)
    xg, gt = _moe_gather(h, pos, probs, cap)
    y = _moe_ffn(xg.reshape(e, b * cap, d), gt.reshape(e, b * cap, 1), w_gate, w_up, w_down)
    return _moe_combine(jnp.swapaxes(pos, 1, 2), y.reshape(e, b, cap, d), x, gate, cap)


def _final_norm_kernel(x_ref, w_ref, o_ref):
    x = x_ref[...]
    o_ref[...] = x * lax.rsqrt(jnp.mean(x * x, axis=-1, keepdims=True) + EPS) * w_ref[...]


def _final_norm(x, w, *, tm=512):
    b, l, d = x.shape
    return pl.pallas_call(
        _final_norm_kernel,
        grid=(b, l // tm),
        in_specs=[pl.BlockSpec((None, tm, d), lambda bi, i: (bi, i, 0)), pl.BlockSpec((1, d), lambda bi, i: (0, 0))],
        out_specs=pl.BlockSpec((None, tm, d), lambda bi, i: (bi, i, 0)),
        out_shape=jax.ShapeDtypeStruct(x.shape, F32),
        compiler_params=_cparams("parallel", "parallel"),
        name="final_rmsnorm",
    )(x, w.reshape(1, d))


def _rope_tables(seq_len, dim):
    rows = seq_len // GRID_W
    row_id = jnp.repeat(jnp.arange(rows, dtype=F32), GRID_W)
    col_id = jnp.tile(jnp.arange(GRID_W, dtype=F32), rows)
    nf = dim // 4
    inv = ROPE_THETA ** (-jnp.arange(nf, dtype=F32) / nf)
    ang_r = row_id[:, None] * inv
    ang_c = col_id[:, None] * inv
    cos = jnp.concatenate([jnp.cos(ang_r)] * 2 + [jnp.cos(ang_c)] * 2, axis=-1)
    sin = jnp.concatenate([-jnp.sin(ang_r), jnp.sin(ang_r), -jnp.sin(ang_c), jnp.sin(ang_c)], axis=-1)
    return cos, sin


def _identity_tables(seq_len, dim):
    return jnp.ones((seq_len, dim), F32), jnp.zeros((seq_len, dim), F32)


def _attention_mixer(x, ctx, nw, mods_lat, mods_ctx, w_qkv, q_norm, k_norm, w_o, with_ctx_out):
    hd = ATTN_HEAD_DIM
    nq, nkv = ATTN_HEADS * hd, ATTN_KV_HEADS * hd
    l, lc = x.shape[1], ctx.shape[1]
    tn = 512
    colw = jnp.concatenate([jnp.tile(q_norm.astype(F32) * (hd ** -0.5), ATTN_HEADS),
                            jnp.tile(k_norm.astype(F32), ATTN_KV_HEADS), jnp.ones((nkv,), F32)]).reshape(1, -1)
    rope_blocks = (nq + nkv) // tn
    cos, sin = _rope_tables(l, hd)
    icos, isin = _identity_tables(lc, hd)
    kw = dict(colw=colw, rope_blocks=rope_blocks, head_norm=True, half=hd // 4, tn=tn)
    p_lat = _project(x, nw, mods_lat[0], mods_lat[1], w_qkv, cos=cos, sin=sin, **kw)
    off = 0 if with_ctx_out else nq // tn
    p_ctx = _project(ctx, nw, mods_ctx[0], mods_ctx[1], w_qkv, cos=icos, sin=isin, col_off=off, **kw)
    q_lat = p_lat[..., :nq]
    kv_c0 = nq - off * tn
    k_all = jnp.concatenate([p_ctx[..., kv_c0:kv_c0 + nkv], p_lat[..., nq:nq + nkv]], axis=1)
    v_all = jnp.concatenate([p_ctx[..., kv_c0 + nkv:], p_lat[..., nq + nkv:]], axis=1)
    x = _out_project(_attention(q_lat, k_all, v_all), w_o, x, mods_lat[2])
    if with_ctx_out:
        o_ctx = _attention(p_ctx[..., :nq], p_ctx[..., nq:nq + nkv], p_ctx[..., nq + nkv:])
        ctx = _out_project(o_ctx, w_o, ctx, mods_ctx[2])
    return x, ctx


def _retention_mixer(x, ctx, nw, mods_lat, mods_ctx, w_in, decay_logit, w_o, with_ctx_out):
    dk, hh = RET_QK_DIM, RET_HEADS
    l, lc = x.shape[1], ctx.shape[1]
    tn = 512
    log_g = jax.nn.log_sigmoid(decay_logit.astype(F32))
    colw = jnp.concatenate([jnp.ones((hh * dk,), F32), jnp.full((hh * dk,), dk ** -0.5, F32),
                            jnp.ones((w_in.shape[1] - 2 * hh * dk,), F32)]).reshape(1, -1)
    rope_blocks = 2 * hh * dk // tn
    cos, sin = _rope_tables(l, dk)
    icos, isin = _identity_tables(lc, dk)
    kw = dict(colw=colw, rope_blocks=rope_blocks, head_norm=False, half=dk // 4, tn=tn)
    p_lat = _project(x, nw, mods_lat[0], mods_lat[1], w_in, cos=cos, sin=sin, **kw)
    p_ctx = _project(ctx, nw, mods_ctx[0], mods_ctx[1], w_in, cos=icos, sin=isin, **kw)
    r_ctx, r_lat = _retention(log_g, p_ctx, p_lat, with_ctx_out)
    x = _out_project(r_lat, w_o, x, mods_lat[2])
    if with_ctx_out:
        ctx = _out_project(r_ctx, w_o, ctx, mods_ctx[2])
    return x, ctx


def _hyena_mixer(x, ctx, nw, mods_lat, mods_ctx, w_in, conv_w, conv_b, fparams, skip, w_out, with_ctx_out):
    z = _project(x, nw, mods_lat[0], mods_lat[1], w_in)
    x = _out_project(_hyena_operator(z, conv_w, conv_b, fparams, skip), w_out, x, mods_lat[2])
    if with_ctx_out:
        zc = _project(ctx, nw, mods_ctx[0], mods_ctx[1], w_in)
        ctx = _out_project(_hyena_operator(zc, conv_w, conv_b, fparams, skip), w_out, ctx, mods_ctx[2])
    return x, ctx


def kernel(x, c, ctx, c_ctx, w_mod, b_mod, norm_w, attn_w_qkv, attn_q_norm, attn_k_norm, attn_w_o, ret_w_in, ret_decay_logit, ret_w_o, hy_w_in, hy_conv_w, hy_conv_b, hy_f_w1, hy_f_b1, hy_f_freq1, hy_f_w2, hy_f_b2, hy_f_freq2, hy_f_w3, hy_skip, hy_w_out, moe_router, moe_w_gate, moe_w_up, moe_w_down, final_norm_w):
    depth = w_mod.shape[0]
    b, _, d = x.shape
    rows = -(-(b + 1) // 8) * 8
    c_all = jnp.concatenate([c, c_ctx[None, :], jnp.zeros((rows - b - 1, d), F32)], axis=0)
    mod = _modulation(c_all, w_mod, b_mod)

    for i in range(depth):
        kind, j = i % 3, i // 3
        with_ctx = i < depth - 1
        m = mod[i].reshape(rows, N_MOD, d)
        mods_lat = [m[:b, t][:, None, :] for t in range(N_MOD)]
        mods_ctx = [m[b:b + 1, t][:, None, :] for t in range(N_MOD)]
        nw1, nw2 = norm_w[i, 0], norm_w[i, 1]
        if kind == 0:
            x, ctx = _attention_mixer(x, ctx, nw1, mods_lat, mods_ctx, attn_w_qkv[j], attn_q_norm[j], attn_k_norm[j],
                                      attn_w_o[j], with_ctx)
        elif kind == 1:
            x, ctx = _retention_mixer(x, ctx, nw1, mods_lat, mods_ctx, ret_w_in[j], ret_decay_logit[j], ret_w_o[j],
                                      with_ctx)
        else:
            fparams = (hy_f_w1[j], hy_f_b1[j], hy_f_freq1[j], hy_f_w2[j], hy_f_b2[j], hy_f_freq2[j], hy_f_w3[j])
            x, ctx = _hyena_mixer(x, ctx, nw1, mods_lat, mods_ctx, hy_w_in[j], hy_conv_w[j], hy_conv_b[j], fparams,
                                  hy_skip[j], hy_w_out[j], with_ctx)
        moe_w = (moe_router[i], moe_w_gate[i], moe_w_up[i], moe_w_down[i])
        x = _moe_block(x, nw2, mods_lat[3], mods_lat[4], mods_lat[5], *moe_w)
        if with_ctx:
            ctx = _moe_block(ctx, nw2, mods_ctx[3], mods_ctx[4], mods_ctx[5], *moe_w)
    return _final_norm(x, final_norm_w)
```

```python
import functools
import math

import jax
import jax.numpy as jnp
from jax import lax
from jax.experimental import pallas as pl
from jax.experimental.pallas import tpu as pltpu

F32 = jnp.float32
BF16 = jnp.bfloat16

EPS = 1e-6
GRID_W = 64
ROPE_THETA = 10000.0
ATTN_HEADS = 16
ATTN_KV_HEADS = 4
ATTN_HEAD_DIM = 128
RET_HEADS = 8
RET_QK_DIM = 256
RET_V_DIM = 512
N_EXPERTS = 16
CAPACITY_FACTOR = 2
N_MOD = 6
HYENA_EMB = 33
HYENA_TARGET = 1e-2
HYENA_FAST = 0.3
HYENA_SLOW = 1.5
HYENA_SHIFT = 0.0

LANE = 128
VMEM_LIMIT = 52 * 1024 * 1024


def _cparams(*sem):
    return pltpu.CompilerParams(dimension_semantics=sem, vmem_limit_bytes=VMEM_LIMIT)


def _silu(v):
    return v / (1.0 + jnp.exp(-v))


def _split_bf16(v):
    hi = v.astype(BF16)
    lo = (v - hi.astype(F32)).astype(BF16)
    return hi, lo


def _dot(a, b):
    return jnp.dot(a, b, preferred_element_type=F32)


def _dot3(a, b):
    ah, al = _split_bf16(a)
    bh, bl = _split_bf16(b)
    return _dot(ah, bh) + _dot(al, bh) + _dot(ah, bl)


def _dot_nt(a, b):
    return lax.dot_general(a, b, (((1,), (1,)), ((), ())), preferred_element_type=F32)


def _dot_tn(a, b):
    return lax.dot_general(a, b, (((0,), (0,)), ((), ())), preferred_element_type=F32)


def _mod_kernel(c_ref, w_ref, b_ref, o_ref):
    a = _silu(c_ref[...])
    o_ref[...] = _dot3(a, w_ref[...]) + b_ref[...]


def _modulation(c_all, w_mod, b_mod):
    depth, d, n = w_mod.shape
    rows = c_all.shape[0]
    tn = 512
    return pl.pallas_call(
        _mod_kernel,
        grid=(depth, n // tn),
        in_specs=[
            pl.BlockSpec((rows, d), lambda i, j: (0, 0)),
            pl.BlockSpec((None, d, tn), lambda i, j: (i, 0, j)),
            pl.BlockSpec((None, 1, tn), lambda i, j: (i, 0, j)),
        ],
        out_specs=pl.BlockSpec((None, rows, tn), lambda i, j: (i, 0, j)),
        out_shape=jax.ShapeDtypeStruct((depth, rows, n), F32),
        compiler_params=_cparams("parallel", "parallel"),
        name="modulation",
    )(c_all, w_mod, b_mod.reshape(depth, 1, n))


def _norm_mod(x, nw, sh, sc):
    ms = jnp.mean(x * x, axis=-1, keepdims=True)
    y = x * lax.rsqrt(ms + EPS) * nw
    return y * (1.0 + sc) + sh


def _rope_slices(acc, colw, cos, sin, *, head_norm, half, table_w):
    tn = acc.shape[1]
    lane = lax.broadcasted_iota(jnp.int32, (1, LANE), 1)
    first_half = (lane % (2 * half)) < half
    outs = []
    for s in range(tn // LANE):
        xs = acc[:, s * LANE:(s + 1) * LANE]
        if head_norm:
            xs = xs * lax.rsqrt(jnp.mean(xs * xs, axis=-1, keepdims=True) + EPS)
        xs = xs * colw[:, s * LANE:(s + 1) * LANE]
        t0 = (s * LANE) % table_w
        cs = cos[:, t0:t0 + LANE]
        sn = sin[:, t0:t0 + LANE]
        if 2 * half == LANE:
            partner = pltpu.roll(xs, half, axis=1)
        else:
            partner = jnp.where(first_half, pltpu.roll(xs, LANE - half, axis=1), pltpu.roll(xs, half, axis=1))
        outs.append(xs * cs + partner * sn)
    return jnp.concatenate(outs, axis=1)


def _proj_kernel(x_ref, nw_ref, sh_ref, sc_ref, w_ref, colw_ref, cos_ref, sin_ref, o_ref, h_scr,
                 *, rope_blocks, col_off, head_norm, half, table_w):
    j = pl.program_id(2)

    @pl.when(j == 0)
    def _():
        h_scr[...] = _norm_mod(x_ref[...], nw_ref[...], sh_ref[...], sc_ref[...]).astype(BF16)

    acc = _dot(h_scr[...], w_ref[...].astype(BF16))
    o_ref[...] = acc.astype(o_ref.dtype)
    if rope_blocks:
        @pl.when(j + col_off < rope_blocks)
        def _():
            o_ref[...] = _rope_slices(acc, colw_ref[...], cos_ref[...], sin_ref[...], head_norm=head_norm,
                                      half=half, table_w=table_w).astype(o_ref.dtype)


def _project(x, nw, sh, sc, w, layer, *, colw=None, cos=None, sin=None, rope_blocks=0, col_off=0, head_norm=False,
             half=32, tn=512, tm=1024):
    b, l, d = x.shape
    n = w.shape[2] - col_off * tn
    tm = min(tm, l)
    per_sample = sh.shape[0] != 1
    mod_map = (lambda bi, i, j: (bi, 0, 0)) if per_sample else (lambda bi, i, j: (0, 0, 0))
    if colw is None:
        colw = jnp.ones((1, w.shape[2]), F32)
        cos = jnp.ones((l, LANE), F32)
        sin = jnp.zeros((l, LANE), F32)
    table_w = cos.shape[1]
    kern = functools.partial(_proj_kernel, rope_blocks=rope_blocks, col_off=col_off, head_norm=head_norm,
                             half=half, table_w=table_w)
    return pl.pallas_call(
        kern,
        grid=(b, l // tm, n // tn),
        in_specs=[
            pl.BlockSpec((None, tm, d), lambda bi, i, j: (bi, i, 0)),
            pl.BlockSpec((1, d), lambda bi, i, j: (0, 0)),
            pl.BlockSpec((None, 1, d), mod_map),
            pl.BlockSpec((None, 1, d), mod_map),
            pl.BlockSpec((None, d, tn), lambda bi, i, j: (layer, 0, j + col_off)),
            pl.BlockSpec((1, tn), lambda bi, i, j: (0, j + col_off)),
            pl.BlockSpec((tm, table_w), lambda bi, i, j: (i, 0)),
            pl.BlockSpec((tm, table_w), lambda bi, i, j: (i, 0)),
        ],
        out_specs=pl.BlockSpec((None, tm, tn), lambda bi, i, j: (bi, i, j)),
        out_shape=jax.ShapeDtypeStruct((b, l, n), BF16),
        scratch_shapes=[pltpu.VMEM((tm, d), BF16)],
        compiler_params=_cparams("parallel", "parallel", "arbitrary"),
        name="norm_mod_project",
    )(x, nw.reshape(1, d), sh, sc, w, colw, cos, sin)


def _out_proj_kernel(a_ref, w_ref, r_ref, g_ref, o_ref):
    acc = _dot(a_ref[...], w_ref[...].astype(BF16))
    o_ref[...] = r_ref[...] + g_ref[...] * acc


def _out_project(a, w, layer, resid, gate, *, tm=1024, tn=512):
    b, l, k = a.shape
    n = w.shape[2]
    tm = min(tm, l)
    per_sample = gate.shape[0] != 1
    g_map = (lambda bi, i, j: (bi, 0, j)) if per_sample else (lambda bi, i, j: (0, 0, j))
    return pl.pallas_call(
        _out_proj_kernel,
        grid=(b, l // tm, n // tn),
        in_specs=[
            pl.BlockSpec((None, tm, k), lambda bi, i, j: (bi, i, 0)),
            pl.BlockSpec((None, k, tn), lambda bi, i, j: (layer, 0, j)),
            pl.BlockSpec((None, tm, tn), lambda bi, i, j: (bi, i, j)),
            pl.BlockSpec((None, 1, tn), g_map),
        ],
        out_specs=pl.BlockSpec((None, tm, tn), lambda bi, i, j: (bi, i, j)),
        out_shape=jax.ShapeDtypeStruct((b, l, n), F32),
        compiler_params=_cparams("parallel", "parallel", "parallel"),
        name="out_project_residual",
    )(a, w, resid, gate)


def _attn_kernel(q_ref, *refs, groups, n_kv):
    o_ref = refs[2 * n_kv]
    ks = [refs[2 * t][...] for t in range(n_kv)]
    vs = [refs[2 * t + 1][...] for t in range(n_kv)]
    for g in range(groups):
        q = q_ref[:, g * ATTN_HEAD_DIM:(g + 1) * ATTN_HEAD_DIM]
        ss = [_dot_nt(q, k) for k in ks]
        m = functools.reduce(jnp.maximum, [jnp.max(s, axis=-1, keepdims=True) for s in ss])
        ps = [jnp.exp(s - m) for s in ss]
        den = sum(jnp.sum(p, axis=-1, keepdims=True) for p in ps)
        o = sum(_dot(p.astype(BF16), v) for p, v in zip(ps, vs)) / den
        o_ref[:, g * ATTN_HEAD_DIM:(g + 1) * ATTN_HEAD_DIM] = o.astype(o_ref.dtype)


def _attention(pq, kv_sources, *, tq=256):
    b, lq, _ = pq.shape
    hd = ATTN_HEAD_DIM
    groups = ATTN_HEADS // ATTN_KV_HEADS
    gw = groups * hd
    tq = min(tq, lq)
    in_specs = [pl.BlockSpec((None, tq, gw), lambda bi, h, i: (bi, i, h))]
    args = [pq]
    for p, k0, v0 in kv_sources:
        lk = p.shape[1]
        in_specs.append(pl.BlockSpec((None, lk, hd), lambda bi, h, i, o=k0 // hd: (bi, 0, o + h)))
        in_specs.append(pl.BlockSpec((None, lk, hd), lambda bi, h, i, o=v0 // hd: (bi, 0, o + h)))
        args += [p, p]
    return pl.pallas_call(
        functools.partial(_attn_kernel, groups=groups, n_kv=len(kv_sources)),
        grid=(b, ATTN_KV_HEADS, lq // tq),
        in_specs=in_specs,
        out_specs=pl.BlockSpec((None, tq, gw), lambda bi, h, i: (bi, i, h)),
        out_shape=jax.ShapeDtypeStruct((b, lq, ATTN_HEADS * hd), BF16),
        compiler_params=_cparams("parallel", "parallel", "parallel"),
        name="gqa_attention",
    )(*args)


RET_CHUNK = 256


def _ret_readout(o, g):
    of = o * lax.rsqrt(jnp.mean(o * o, axis=-1, keepdims=True) + EPS)
    return (_silu(g.astype(F32)) * of).astype(BF16)


def _ret_kernel(lg_ref, qc_ref, kc_ref, vc_ref, gc_ref, ql_ref, kl_ref, vl_ref, gl_ref, oc_ref, ol_ref,
                ob_scr, sf_scr, sb_scr, *, n_chunks, with_ctx_out):
    c = RET_CHUNK
    h = pl.program_id(1)
    lgf = lg_ref[0, h]
    lgb = lg_ref[1, h]
    row = lax.broadcasted_iota(jnp.int32, (c, 1), 0).astype(F32)
    col = lax.broadcasted_iota(jnp.int32, (1, c), 1).astype(F32)
    diff = row - col
    dmask = jnp.exp(jnp.where(diff >= 0, diff * lgf, -diff * lgb))
    qdec_f = jnp.exp((row + 1.0) * lgf)
    kdec_f = jnp.exp((c - 1.0 - row) * lgf)
    qdec_b = jnp.exp((c - row) * lgb)
    kdec_b = jnp.exp(row * lgb)
    one = jnp.ones((1, 1), F32)
    cdec_f = jnp.exp(one * (c * lgf))
    cdec_b = jnp.exp(one * (c * lgb))

    qc = qc_ref[...]
    kc = kc_ref[...].astype(F32)
    vc = vc_ref[...]
    sf_scr[...] = _dot_tn((kc * kdec_f).astype(BF16), vc)
    sb_scr[...] = _dot_tn((kc * kdec_b).astype(BF16), vc)
    if with_ctx_out:
        inner = (_dot_nt(qc, kc_ref[...]) * dmask).astype(BF16)
        oc_ref[...] = _ret_readout(_dot(inner, vc), gc_ref[...])
    else:
        oc_ref[...] = jnp.zeros(oc_ref.shape, oc_ref.dtype)

    def bwd(i, carry):
        ci = n_chunks - 1 - i
        sl = pl.ds(pl.multiple_of(ci * c, c), c)
        q = ql_ref[sl, :].astype(F32)
        k = kl_ref[sl, :].astype(F32)
        ob_scr[sl, :] = _dot((q * qdec_b).astype(BF16), sb_scr[...].astype(BF16))
        sb_scr[...] = sb_scr[...] * cdec_b + _dot_tn((k * kdec_b).astype(BF16), vl_ref[sl, :])
        return carry

    lax.fori_loop(0, n_chunks, bwd, 0)

    def fwd(ci, carry):
        sl = pl.ds(pl.multiple_of(ci * c, c), c)
        qb = ql_ref[sl, :]
        kb = kl_ref[sl, :]
        v = vl_ref[sl, :]
        q = qb.astype(F32)
        k = kb.astype(F32)
        inner = (_dot_nt(qb, kb) * dmask).astype(BF16)
        o = ob_scr[sl, :] + _dot(inner, v) + _dot((q * qdec_f).astype(BF16), sf_scr[...].astype(BF16))
        sf_scr[...] = sf_scr[...] * cdec_f + _dot_tn((k * kdec_f).astype(BF16), v)
        ol_ref[sl, :] = _ret_readout(o, gl_ref[sl, :])
        return carry

    lax.fori_loop(0, n_chunks, fwd, 0)


def _retention(log_g, p_ctx, p_lat, with_ctx_out):
    b, l, _ = p_lat.shape
    lc = p_ctx.shape[1]
    assert lc == RET_CHUNK and l % RET_CHUNK == 0
    dk, dv, hh = RET_QK_DIM, RET_V_DIM, RET_HEADS
    k_off = hh * dk // dk
    v_off = 2 * hh * dk // dv
    g_off = v_off + hh

    def specs(ln):
        return [
            pl.BlockSpec((None, ln, dk), lambda bi, h: (bi, 0, h)),
            pl.BlockSpec((None, ln, dk), lambda bi, h: (bi, 0, k_off + h)),
            pl.BlockSpec((None, ln, dv), lambda bi, h: (bi, 0, v_off + h)),
            pl.BlockSpec((None, ln, dv), lambda bi, h: (bi, 0, g_off + h)),
        ]

    return pl.pallas_call(
        functools.partial(_ret_kernel, n_chunks=l // RET_CHUNK, with_ctx_out=with_ctx_out),
        grid=(b, hh),
        in_specs=[pl.BlockSpec(memory_space=pltpu.SMEM)] + specs(lc) + specs(l),
        out_specs=[
            pl.BlockSpec((None, lc, dv), lambda bi, h: (bi, 0, h)),
            pl.BlockSpec((None, l, dv), lambda bi, h: (bi, 0, h)),
        ],
        out_shape=[
            jax.ShapeDtypeStruct((b, lc, hh * dv), BF16),
            jax.ShapeDtypeStruct((b, l, hh * dv), BF16),
        ],
        scratch_shapes=[pltpu.VMEM((l, dv), F32), pltpu.VMEM((dk, dv), F32), pltpu.VMEM((dk, dv), F32)],
        compiler_params=_cparams("parallel", "parallel"),
        name="retention",
    )(log_g, p_ctx, p_ctx, p_ctx, p_ctx, p_lat, p_lat, p_lat, p_lat)


def _hy_gate_kernel(x0_ref, x1_ref, v_ref, cw0_ref, cw1_ref, cwv_ref, cb0_ref, cb1_ref, cbv_ref, u_ref, g_ref):
    l = x0_ref.shape[0]
    t = lax.broadcasted_iota(jnp.int32, (l, 1), 0)

    def conv3(z_ref, cw_ref, cb_ref):
        z = z_ref[...].astype(F32)
        prev = jnp.where(t == 0, 0.0, pltpu.roll(z, 1, axis=0))
        nxt = jnp.where(t == l - 1, 0.0, pltpu.roll(z, l - 1, axis=0))
        cw = cw_ref[...]
        return prev * cw[0:1, :] + z * cw[1:2, :] + nxt * cw[2:3, :] + cb_ref[...]

    x1 = conv3(x1_ref, cw1_ref, cb1_ref)
    v = conv3(v_ref, cwv_ref, cbv_ref)
    u_ref[...] = (v * x1).astype(u_ref.dtype)
    g_ref[...] = conv3(x0_ref, cw0_ref, cb0_ref).astype(g_ref.dtype)


def _hyena_gate(z, conv_w, conv_b, *, tn=256):
    b, l, d3 = z.shape
    d = d3 // 3
    nb = d // tn
    cb = conv_b.reshape(1, d3)
    zs = [pl.BlockSpec((None, l, tn), (lambda bi, j, o=o: (bi, 0, j + o * nb))) for o in range(3)]
    ws = [pl.BlockSpec((3, tn), (lambda bi, j, o=o: (0, j + o * nb))) for o in range(3)]
    bs = [pl.BlockSpec((1, tn), (lambda bi, j, o=o: (0, j + o * nb))) for o in range(3)]
    return pl.pallas_call(
        _hy_gate_kernel,
        grid=(b, nb),
        in_specs=zs + ws + bs,
        out_specs=[pl.BlockSpec((None, l, tn), lambda bi, j: (bi, 0, j))] * 2,
        out_shape=[jax.ShapeDtypeStruct((b, l, d), BF16)] * 2,
        compiler_params=_cparams("parallel", "parallel"),
        name="hyena_conv3_gate",
    )(z, z, z, conv_w, conv_w, conv_w, cb, cb, cb)


def _hy_filter_kernel(z_ref, w1_ref, b1_ref, f1_ref, w2_ref, b2_ref, f2_ref, w3f_ref, w3b_ref, dec_ref, o_ref):
    l = z_ref.shape[0]
    h = jnp.sin(f1_ref[...] * (_dot3(z_ref[...], w1_ref[...]) + b1_ref[...]))
    h = jnp.sin(f2_ref[...] * (_dot3(h, w2_ref[...]) + b2_ref[...]))
    decay = dec_ref[...] + HYENA_SHIFT
    hf = _dot3(h, w3f_ref[...]) * decay
    hb = _dot3(h, w3b_ref[...]) * decay
    t = lax.broadcasted_iota(jnp.int32, (l, 1), 0)
    hb = jnp.where(t == 0, 0.0, hb)
    norm = jnp.sum(jnp.abs(hf), axis=0, keepdims=True) + jnp.sum(jnp.abs(hb), axis=0, keepdims=True)
    o_ref[0] = (hf / norm).astype(o_ref.dtype)
    o_ref[1] = (hb / norm).astype(o_ref.dtype)


def _hyena_filter(l, d, w1, b1, fr1, w2, b2, fr2, w3, *, tn=256):
    t = jnp.linspace(0.0, 1.0, l, dtype=F32)[:, None]
    bands = (HYENA_EMB - 1) // 2
    w = 2.0 * math.pi * jnp.arange(l, dtype=F32)[:, None] / l
    f = jnp.linspace(1e-4, bands - 1, bands, dtype=F32)[None, :]
    z = jnp.concatenate([t, jnp.cos(f * w), -jnp.sin(f * w)], axis=-1)
    deltas = jnp.abs(jnp.linspace(math.log(HYENA_TARGET) / HYENA_SLOW, math.log(HYENA_TARGET) / HYENA_FAST, d, dtype=F32))
    decay = jnp.exp(-t * deltas)
    fw = w1.shape[1]
    pad = lambda a, r, c: jnp.pad(a.astype(F32), ((0, r - a.shape[0]), (0, c - a.shape[1])))
    z = pad(z, l, LANE)
    w1p = pad(w1, LANE, LANE)
    w2p = pad(w2, LANE, LANE)
    w3p = pad(w3, LANE, 2 * d)
    vec = lambda a: pad(a.reshape(1, fw), 1, LANE)
    nb = d // tn
    full = lambda shape: pl.BlockSpec(shape, lambda j: (0, 0))
    return pl.pallas_call(
        _hy_filter_kernel,
        grid=(nb,),
        in_specs=[full((l, LANE)), full((LANE, LANE)), full((1, LANE)), full((1, LANE)), full((LANE, LANE)),
                  full((1, LANE)), full((1, LANE)),
                  pl.BlockSpec((LANE, tn), lambda j: (0, j)),
                  pl.BlockSpec((LANE, tn), lambda j: (0, j + nb)),
                  pl.BlockSpec((l, tn), lambda j: (0, j))],
        out_specs=pl.BlockSpec((2, l, tn), lambda j: (0, 0, j)),
        out_shape=jax.ShapeDtypeStruct((2, l, d), BF16),
        compiler_params=_cparams("parallel"),
        name="hyena_filter",
    )(z, w1p, vec(b1), vec(fr1), w2p, vec(b2), vec(fr2), w3p, w3p, decay)


def _dft_tables(l):
    n = 2 * l
    k = jnp.arange(l, dtype=jnp.int32)
    ang = ((k[:, None] * k[None, :]) % n).astype(F32) * (2.0 * math.pi / n)
    cs, sn = jnp.cos(ang), jnp.sin(ang)
    alt = jnp.where(k % 2 == 0, 1.0, -1.0).astype(F32)
    f_b = jnp.where(k[:, None] == 0, alt[None, :], -sn)
    fwd = jnp.concatenate([cs, f_b], axis=0)
    wa = jnp.where(k[None, :] == 0, 1.0, 2.0) * cs.T
    wb = jnp.where(k[None, :] == 0, alt[:, None], -2.0 * sn.T)
    inv = jnp.concatenate([wa, wb], axis=1) * (1.0 / n)
    return fwd.astype(BF16), inv.astype(BF16)


def _dft_raw_kernel(fa_ref, fb_ref, u_ref, o_ref):
    u = u_ref[...]
    o_ref[0] = _dot(fa_ref[...], u)
    o_ref[1] = _dot(fb_ref[...], u)


def _dft_raw(fwd, u, *, tm=512, tn=512):
    b, l, d = u.shape
    tm, tn = min(tm, l), min(tn, d)
    nb = l // tm
    return pl.pallas_call(
        _dft_raw_kernel,
        grid=(b, d // tn, nb),
        in_specs=[pl.BlockSpec((tm, l), lambda bi, j, i: (i, 0)),
                  pl.BlockSpec((tm, l), lambda bi, j, i: (i + nb, 0)),
                  pl.BlockSpec((None, l, tn), lambda bi, j, i: (bi, 0, j))],
        out_specs=pl.BlockSpec((None, 2, tm, tn), lambda bi, j, i: (bi, 0, i, j)),
        out_shape=jax.ShapeDtypeStruct((b, 2, l, d), F32),
        compiler_params=_cparams("parallel", "parallel", "parallel"),
        name="hyena_dft_filter",
    )(fwd, fwd, u)


def _dft_mul_kernel(fa_ref, fb_ref, u_ref, hs_ref, o_ref):
    i = pl.program_id(2)
    u = u_ref[...]
    ua = _dot(fa_ref[...], u)
    ub = _dot(fb_ref[...], u)
    first = jnp.logical_and(lax.broadcasted_iota(jnp.int32, (ua.shape[0], 1), 0) == 0, i == 0)
    ha = hs_ref[0, 0] + hs_ref[1, 0]
    hb = jnp.where(first, hs_ref[0, 1] + hs_ref[1, 1], hs_ref[0, 1] - hs_ref[1, 1])
    pa = jnp.where(first, ua * ha, ua * ha - ub * hb)
    pb = jnp.where(first, ub * hb, ua * hb + ub * ha)
    o_ref[0] = pa.astype(o_ref.dtype)
    o_ref[1] = pb.astype(o_ref.dtype)


def _dft_mul(fwd, u, hspec, *, tm=512, tn=512):
    b, l, d = u.shape
    tm, tn = min(tm, l), min(tn, d)
    nb = l // tm
    out = pl.pallas_call(
        _dft_mul_kernel,
        grid=(b, d // tn, nb),
        in_specs=[pl.BlockSpec((tm, l), lambda bi, j, i: (i, 0)),
                  pl.BlockSpec((tm, l), lambda bi, j, i: (i + nb, 0)),
                  pl.BlockSpec((None, l, tn), lambda bi, j, i: (bi, 0, j)),
                  pl.BlockSpec((2, 2, tm, tn), lambda bi, j, i: (0, 0, i, j))],
        out_specs=pl.BlockSpec((None, 2, tm, tn), lambda bi, j, i: (bi, 0, i, j)),
        out_shape=jax.ShapeDtypeStruct((b, 2, l, d), BF16),
        compiler_params=_cparams("parallel", "parallel", "parallel"),
        name="hyena_dft_forward",
    )(fwd, fwd, u, hspec)
    return out.reshape(b, 2 * l, d)


def _idft_kernel(g_ref, p_ref, u_ref, x0_ref, skip_ref, o_ref):
    y = _dot(g_ref[...], p_ref[...]) + u_ref[...].astype(F32) * skip_ref[...]
    o_ref[...] = (y * x0_ref[...].astype(F32)).astype(o_ref.dtype)


def _idft_gate(inv, p, u, x0, skip, *, tm=512, tn=512):
    b, l, d = u.shape
    tm, tn = min(tm, l), min(tn, d)
    return pl.pallas_call(
        _idft_kernel,
        grid=(b, d // tn, l // tm),
        in_specs=[pl.BlockSpec((tm, 2 * l), lambda bi, j, i: (i, 0)),
                  pl.BlockSpec((None, 2 * l, tn), lambda bi, j, i: (bi, 0, j)),
                  pl.BlockSpec((None, tm, tn), lambda bi, j, i: (bi, i, j)),
                  pl.BlockSpec((None, tm, tn), lambda bi, j, i: (bi, i, j)),
                  pl.BlockSpec((1, tn), lambda bi, j, i: (0, j))],
        out_specs=pl.BlockSpec((None, tm, tn), lambda bi, j, i: (bi, i, j)),
        out_shape=jax.ShapeDtypeStruct((b, l, d), BF16),
        compiler_params=_cparams("parallel", "parallel", "parallel"),
        name="hyena_dft_inverse",
    )(inv, p, u, x0, skip.reshape(1, d))


def _hyena_operator(z, conv_w, conv_b, fparams, skip):
    b, l, d3 = z.shape
    d = d3 // 3
    u, x0 = _hyena_gate(z, conv_w, conv_b)
    taps = _hyena_filter(l, d, *fparams)
    fwd, inv = _dft_tables(l)
    hspec = _dft_raw(fwd, taps)
    p = _dft_mul(fwd, u, hspec)
    return _idft_gate(inv, p, u, x0, skip)


def _router_kernel(x_ref, nw_ref, sh_ref, sc_ref, wr_ref, h_ref, lg_ref):
    h = _norm_mod(x_ref[...], nw_ref[...], sh_ref[...], sc_ref[...])
    h_ref[...] = h.astype(BF16)
    lg_ref[...] = _dot3(h, wr_ref[...])


def _norm_mod_router(x, nw, sh, sc, w_router, *, tm=512):
    b, l, d = x.shape
    n_e = w_router.shape[1]
    e = LANE
    w_router = jnp.pad(w_router, ((0, 0), (0, e - n_e)))
    tm = min(tm, l)
    per_sample = sh.shape[0] != 1
    mod_map = (lambda bi, i: (bi, 0, 0)) if per_sample else (lambda bi, i: (0, 0, 0))
    return pl.pallas_call(
        _router_kernel,
        grid=(b, l // tm),
        in_specs=[
            pl.BlockSpec((None, tm, d), lambda bi, i: (bi, i, 0)),
            pl.BlockSpec((1, d), lambda bi, i: (0, 0)),
            pl.BlockSpec((None, 1, d), mod_map),
            pl.BlockSpec((None, 1, d), mod_map),
            pl.BlockSpec((d, e), lambda bi, i: (0, 0)),
        ],
        out_specs=[pl.BlockSpec((None, tm, d), lambda bi, i: (bi, i, 0)),
                   pl.BlockSpec((None, tm, e), lambda bi, i: (bi, i, 0))],
        out_shape=[jax.ShapeDtypeStruct((b, l, d), BF16), jax.ShapeDtypeStruct((b, l, e), F32)],
        compiler_params=_cparams("parallel", "parallel"),
        name="norm_mod_router",
    )(x, nw.reshape(1, d), sh, sc, w_router)


def _route_kernel(lg_ref, tri_ref, pos_ref, prob_ref, *, cap):
    lg = lg_ref[...]
    m = jnp.max(lg, axis=0, keepdims=True)
    ex = jnp.exp(lg - m)
    probs = ex / jnp.sum(ex, axis=0, keepdims=True)
    bits = lax.bitcast_convert_type(probs, jnp.int32)

    def count(mask):
        return jnp.sum(jnp.where(mask, 1.0, 0.0), axis=1, keepdims=True)

    def step(i, thr):
        trial = thr | lax.shift_left(jnp.int32(1), 30 - i)
        return jnp.where(count(bits >= trial) >= cap, trial, thr)

    thr = lax.fori_loop(0, 31, step, jnp.zeros((lg.shape[0], 1), jnp.int32))
    gt = bits > thr
    eq = bits == thr
    need = cap - count(gt).astype(jnp.int32)
    both = jnp.concatenate([jnp.where(gt, 1.0, 0.0), jnp.where(eq, 1.0, 0.0)], axis=0).astype(BF16)
    csum = _dot(both, tri_ref[...])
    e = lg.shape[0]
    rank_gt = csum[:e].astype(jnp.int32)
    rank_eq = csum[e:].astype(jnp.int32)
    sel = jnp.logical_or(gt, jnp.logical_and(eq, rank_eq < need))
    pos = rank_gt + jnp.minimum(rank_eq, need)
    pos_ref[...] = jnp.where(sel, pos, -1)
    prob_ref[...] = probs


def _route(logits_t, cap):
    b, e, n = logits_t.shape
    idx = jnp.arange(n, dtype=jnp.int32)
    tri = (idx[:, None] < idx[None, :]).astype(BF16)
    return pl.pallas_call(
        functools.partial(_route_kernel, cap=cap),
        grid=(b,),
        in_specs=[pl.BlockSpec((None, e, n), lambda bi: (bi, 0, 0)),
                  pl.BlockSpec((n, n), lambda bi: (0, 0))],
        out_specs=[pl.BlockSpec((None, e, n), lambda bi: (bi, 0, 0))] * 2,
        out_shape=[jax.ShapeDtypeStruct((b, e, n), jnp.int32), jax.ShapeDtypeStruct((b, e, n), F32)],
        compiler_params=_cparams("parallel"),
        name="expert_choice_route",
    )(logits_t, tri)


def _gather_kernel(h_ref, pos_ref, prob_ref, xg_ref, gate_ref, *, cap):
    n = h_ref.shape[0]
    slot = lax.broadcasted_iota(jnp.int32, (cap, n), 0)
    match = slot == pos_ref[...]
    onehot = jnp.where(match, 1.0, 0.0).astype(BF16)
    xg_ref[...] = _dot(onehot, h_ref[...]).astype(xg_ref.dtype)
    gate_ref[...] = jnp.sum(jnp.where(match, prob_ref[...], 0.0), axis=1, keepdims=True)


def _gather_into_kernel(h_ref, pos_ref, prob_ref, xg_in, gate_in, xg_ref, gate_ref, *, cap):
    del xg_in, gate_in
    _gather_kernel(h_ref, pos_ref, prob_ref, xg_ref, gate_ref, cap=cap)


def _moe_gather(h, pos, probs, cap, m_total, row0, buffers=None):
    b, n, d = h.shape
    e = pos.shape[1]
    blk0 = row0 // cap
    row = lambda a: a.reshape(b, e, 1, n)
    in_specs = [pl.BlockSpec((None, n, d), lambda bi, ei: (bi, 0, 0)),
                pl.BlockSpec((None, None, 1, n), lambda bi, ei: (bi, ei, 0, 0)),
                pl.BlockSpec((None, None, 1, n), lambda bi, ei: (bi, ei, 0, 0))]
    args = [h, row(pos), row(probs)]
    kern = functools.partial(_gather_kernel, cap=cap)
    aliases = {}
    if buffers is not None:
        in_specs += [pl.BlockSpec(memory_space=pl.ANY)] * 2
        args += list(buffers)
        kern = functools.partial(_gather_into_kernel, cap=cap)
        aliases = {3: 0, 4: 1}
    return pl.pallas_call(
        kern,
        grid=(b, e),
        in_specs=in_specs,
        out_specs=[pl.BlockSpec((None, cap, d), lambda bi, ei: (ei, blk0 + bi, 0)),
                   pl.BlockSpec((None, cap, 1), lambda bi, ei: (ei, blk0 + bi, 0))],
        out_shape=[jax.ShapeDtypeStruct((e, m_total, d), BF16), jax.ShapeDtypeStruct((e, m_total, 1), F32)],
        input_output_aliases=aliases,
        compiler_params=_cparams("parallel", "parallel"),
        name="moe_gather",
    )(*args)


def _ffn_kernel(x_ref, g_ref, wg_ref, wu_ref, wd_ref, o_ref, acc_ref):
    f = pl.program_id(2)
    x = x_ref[...]
    a = _dot(x, wg_ref[...].astype(BF16))
    u = _dot(x, wu_ref[...].astype(BF16))
    hmid = (_silu(a) * u * g_ref[...]).astype(BF16)
    y = _dot(hmid, wd_ref[...].astype(BF16))

    @pl.when(f == 0)
    def _():
        acc_ref[...] = y

    @pl.when(f > 0)
    def _():
        acc_ref[...] += y

    @pl.when(f == pl.num_programs(2) - 1)
    def _():
        o_ref[...] = acc_ref[...].astype(o_ref.dtype)


def _moe_ffn(xg, gate, w_gate, w_up, w_down, layer, *, tf=256):
    e, m, d = xg.shape
    ff = w_gate.shape[3]
    tm = max(t for t in range(16, 1025, 16) if m % t == 0)
    return pl.pallas_call(
        _ffn_kernel,
        grid=(e, m // tm, ff // tf),
        in_specs=[pl.BlockSpec((None, tm, d), lambda ei, i, f: (ei, i, 0)),
                  pl.BlockSpec((None, tm, 1), lambda ei, i, f: (ei, i, 0)),
                  pl.BlockSpec((None, None, d, tf), lambda ei, i, f: (layer, ei, 0, f)),
                  pl.BlockSpec((None, None, d, tf), lambda ei, i, f: (layer, ei, 0, f)),
                  pl.BlockSpec((None, None, tf, d), lambda ei, i, f: (layer, ei, f, 0))],
        out_specs=pl.BlockSpec((None, tm, d), lambda ei, i, f: (ei, i, 0)),
        out_shape=jax.ShapeDtypeStruct((e, m, d), BF16),
        scratch_shapes=[pltpu.VMEM((tm, d), F32)],
        compiler_params=_cparams("parallel", "parallel", "arbitrary"),
        name="moe_expert_ffn",
    )(xg, gate, w_gate, w_up, w_down)


def _combine_kernel(post_ref, y_ref, r_ref, g_ref, o_ref, pt_scr, *, cap):
    j = pl.program_id(2)
    tm, e = post_ref.shape

    @pl.when(j == 0)
    def _():
        post = post_ref[...]
        if cap % LANE == 0:
            slot = lax.broadcasted_iota(jnp.int32, (tm, cap), 1)
            for ei in range(e):
                pt_scr[:, ei * cap:(ei + 1) * cap] = jnp.where(post[:, ei:ei + 1] == slot, 1.0, 0.0).astype(BF16)
        else:
            slot = lax.broadcasted_iota(jnp.int32, (tm, e * cap), 1)
            hit = jnp.zeros((tm, e * cap), jnp.bool_)
            for ei in range(e):
                tgt = jnp.where(post[:, ei:ei + 1] >= 0, post[:, ei:ei + 1] + ei * cap, -1)
                hit = jnp.logical_or(hit, tgt == slot)
            pt_scr[...] = jnp.where(hit, 1.0, 0.0).astype(BF16)

    y = y_ref[...].reshape(e * cap, y_ref.shape[2])
    o_ref[...] = r_ref[...] + g_ref[...] * _dot(pt_scr[...], y)


def _moe_combine(pos_t, y, row0, resid, gate, cap, *, tm=512, tn=512):
    b, n, e = pos_t.shape
    d = y.shape[2]
    tm = min(tm, n)
    blk0 = row0 // cap
    per_sample = gate.shape[0] != 1
    g_map = (lambda bi, i, j: (bi, 0, j)) if per_sample else (lambda bi, i, j: (0, 0, j))
    return pl.pallas_call(
        functools.partial(_combine_kernel, cap=cap),
        grid=(b, n // tm, d // tn),
        in_specs=[pl.BlockSpec((None, tm, e), lambda bi, i, j: (bi, i, 0)),
                  pl.BlockSpec((e, cap, tn), lambda bi, i, j: (0, blk0 + bi, j)),
                  pl.BlockSpec((None, tm, tn), lambda bi, i, j: (bi, i, j)),
                  pl.BlockSpec((None, 1, tn), g_map)],
        out_specs=pl.BlockSpec((None, tm, tn), lambda bi, i, j: (bi, i, j)),
        out_shape=jax.ShapeDtypeStruct((b, n, d), F32),
        scratch_shapes=[pltpu.VMEM((tm, e * cap), BF16)],
        compiler_params=_cparams("parallel", "parallel", "arbitrary"),
        name="moe_combine_residual",
    )(pos_t, y, resid, gate)


def _moe_block(streams, nw, w_router, w_gate, w_up, w_down, layer):
    e = w_router.shape[1]
    caps = [CAPACITY_FACTOR * s[0].shape[1] // e for s in streams]
    rows = [s[0].shape[0] * cap for s, cap in zip(streams, caps)]
    row0 = [sum(rows[:t]) for t in range(len(streams))]
    m_total = sum(rows)
    routed, buffers = [], None
    for (x, sh, sc, _), cap, r0 in zip(streams, caps, row0):
        h, logits = _norm_mod_router(x, nw, sh, sc, w_router)
        pos, probs = _route(jnp.swapaxes(logits[..., :e], 1, 2), cap)
        buffers = _moe_gather(h, pos, probs, cap, m_total, r0, buffers)
        routed.append(jnp.swapaxes(pos, 1, 2))
    y = _moe_ffn(buffers[0], buffers[1], w_gate, w_up, w_down, layer)
    return [_moe_combine(pos_t, y, r0, x, gate, cap)
            for (x, _, _, gate), pos_t, cap, r0 in zip(streams, routed, caps, row0)]


def _final_norm_kernel(x_ref, w_ref, o_ref):
    x = x_ref[...]
    o_ref[...] = x * lax.rsqrt(jnp.mean(x * x, axis=-1, keepdims=True) + EPS) * w_ref[...]


def _final_norm(x, w, *, tm=512):
    b, l, d = x.shape
    return pl.pallas_call(
        _final_norm_kernel,
        grid=(b, l // tm),
        in_specs=[pl.BlockSpec((None, tm, d), lambda bi, i: (bi, i, 0)), pl.BlockSpec((1, d), lambda bi, i: (0, 0))],
        out_specs=pl.BlockSpec((None, tm, d), lambda bi, i: (bi, i, 0)),
        out_shape=jax.ShapeDtypeStruct(x.shape, F32),
        compiler_params=_cparams("parallel", "parallel"),
        name="final_rmsnorm",
    )(x, w.reshape(1, d))


def _rope_tables(seq_len, dim):
    rows = seq_len // GRID_W
    row_id = jnp.repeat(jnp.arange(rows, dtype=F32), GRID_W)
    col_id = jnp.tile(jnp.arange(GRID_W, dtype=F32), rows)
    nf = dim // 4
    inv = ROPE_THETA ** (-jnp.arange(nf, dtype=F32) / nf)
    ang_r = row_id[:, None] * inv
    ang_c = col_id[:, None] * inv
    cos = jnp.concatenate([jnp.cos(ang_r)] * 2 + [jnp.cos(ang_c)] * 2, axis=-1)
    sin = jnp.concatenate([-jnp.sin(ang_r), jnp.sin(ang_r), -jnp.sin(ang_c), jnp.sin(ang_c)], axis=-1)
    return cos, sin


def _identity_tables(seq_len, dim):
    return jnp.ones((seq_len, dim), F32), jnp.zeros((seq_len, dim), F32)


def _attention_mixer(x, ctx, nw, mods_lat, mods_ctx, w_qkv, q_norm, k_norm, w_o, layer, with_ctx_out):
    hd = ATTN_HEAD_DIM
    nq, nkv = ATTN_HEADS * hd, ATTN_KV_HEADS * hd
    l, lc = x.shape[1], ctx.shape[1]
    tn = 512
    colw = jnp.concatenate([jnp.tile(q_norm.astype(F32) * (hd ** -0.5), ATTN_HEADS),
                            jnp.tile(k_norm.astype(F32), ATTN_KV_HEADS), jnp.ones((nkv,), F32)]).reshape(1, -1)
    rope_blocks = (nq + nkv) // tn
    cos, sin = _rope_tables(l, hd)
    icos, isin = _identity_tables(lc, hd)
    kw = dict(colw=colw, rope_blocks=rope_blocks, head_norm=True, half=hd // 4, tn=tn)
    p_lat = _project(x, nw, mods_lat[0], mods_lat[1], w_qkv, layer, cos=cos, sin=sin, **kw)
    off = 0 if with_ctx_out else nq // tn
    p_ctx = _project(ctx, nw, mods_ctx[0], mods_ctx[1], w_qkv, layer, cos=icos, sin=isin, col_off=off, **kw)
    kc0 = nq - off * tn
    ctx_kv = (p_ctx, kc0, kc0 + nkv)
    o_lat = _attention(p_lat, [ctx_kv, (p_lat, nq, nq + nkv)])
    x = _out_project(o_lat, w_o, layer, x, mods_lat[2])
    if with_ctx_out:
        ctx = _out_project(_attention(p_ctx, [ctx_kv]), w_o, layer, ctx, mods_ctx[2])
    return x, ctx


def _retention_mixer(x, ctx, nw, mods_lat, mods_ctx, w_in, decay_logit, w_o, layer, with_ctx_out):
    dk, hh = RET_QK_DIM, RET_HEADS
    l, lc = x.shape[1], ctx.shape[1]
    tn = 512
    log_g = jax.nn.log_sigmoid(decay_logit.astype(F32))
    colw = jnp.concatenate([jnp.ones((hh * dk,), F32), jnp.full((hh * dk,), dk ** -0.5, F32),
                            jnp.ones((w_in.shape[2] - 2 * hh * dk,), F32)]).reshape(1, -1)
    rope_blocks = 2 * hh * dk // tn
    cos, sin = _rope_tables(l, dk)
    icos, isin = _identity_tables(lc, dk)
    kw = dict(colw=colw, rope_blocks=rope_blocks, head_norm=False, half=dk // 4, tn=tn)
    p_lat = _project(x, nw, mods_lat[0], mods_lat[1], w_in, layer, cos=cos, sin=sin, **kw)
    p_ctx = _project(ctx, nw, mods_ctx[0], mods_ctx[1], w_in, layer, cos=icos, sin=isin, **kw)
    r_ctx, r_lat = _retention(log_g, p_ctx, p_lat, with_ctx_out)
    x = _out_project(r_lat, w_o, layer, x, mods_lat[2])
    if with_ctx_out:
        ctx = _out_project(r_ctx, w_o, layer, ctx, mods_ctx[2])
    return x, ctx


def _hyena_mixer(x, ctx, nw, mods_lat, mods_ctx, w_in, conv_w, conv_b, fparams, skip, w_out, layer, with_ctx_out):
    z = _project(x, nw, mods_lat[0], mods_lat[1], w_in, layer)
    x = _out_project(_hyena_operator(z, conv_w, conv_b, fparams, skip), w_out, layer, x, mods_lat[2])
    if with_ctx_out:
        zc = _project(ctx, nw, mods_ctx[0], mods_ctx[1], w_in, layer)
        ctx = _out_project(_hyena_operator(zc, conv_w, conv_b, fparams, skip), w_out, layer, ctx, mods_ctx[2])
    return x, ctx


def kernel(x, c, ctx, c_ctx, w_mod, b_mod, norm_w, attn_w_qkv, attn_q_norm, attn_k_norm, attn_w_o, ret_w_in, ret_decay_logit, ret_w_o, hy_w_in, hy_conv_w, hy_conv_b, hy_f_w1, hy_f_b1, hy_f_freq1, hy_f_w2, hy_f_b2, hy_f_freq2, hy_f_w3, hy_skip, hy_w_out, moe_router, moe_w_gate, moe_w_up, moe_w_down, final_norm_w):
    depth = w_mod.shape[0]
    b, _, d = x.shape
    rows = -(-(b + 1) // 8) * 8
    c_all = jnp.concatenate([c, c_ctx[None, :], jnp.zeros((rows - b - 1, d), F32)], axis=0)
    mod = _modulation(c_all, w_mod, b_mod)

    for i in range(depth):
        kind, j = i % 3, i // 3
        with_ctx = i < depth - 1
        m = mod[i].reshape(rows, N_MOD, d)
        mods_lat = [m[:b, t][:, None, :] for t in range(N_MOD)]
        mods_ctx = [m[b:b + 1, t][:, None, :] for t in range(N_MOD)]
        nw1, nw2 = norm_w[i, 0], norm_w[i, 1]
        if kind == 0:
            x, ctx = _attention_mixer(x, ctx, nw1, mods_lat, mods_ctx, attn_w_qkv, attn_q_norm[j], attn_k_norm[j],
                                      attn_w_o, j, with_ctx)
        elif kind == 1:
            x, ctx = _retention_mixer(x, ctx, nw1, mods_lat, mods_ctx, ret_w_in, ret_decay_logit[j], ret_w_o,
                                      j, with_ctx)
        else:
            fparams = (hy_f_w1[j], hy_f_b1[j], hy_f_freq1[j], hy_f_w2[j], hy_f_b2[j], hy_f_freq2[j], hy_f_w3[j])
            x, ctx = _hyena_mixer(x, ctx, nw1, mods_lat, mods_ctx, hy_w_in, hy_conv_w[j], hy_conv_b[j], fparams,
                                  hy_skip[j], hy_w_out, j, with_ctx)
        streams = [(x, mods_lat[3], mods_lat[4], mods_lat[5])]
        if with_ctx:
            streams.append((ctx, mods_ctx[3], mods_ctx[4], mods_ctx[5]))
        outs = _moe_block(streams, nw2, moe_router[i], moe_w_gate, moe_w_up, moe_w_down, i)
        x = outs[0]
        if with_ctx:
            ctx = outs[1]
    return _final_norm(x, final_norm_w)
```

```python
import functools
import math

import jax
import jax.numpy as jnp
from jax import lax
from jax.experimental import pallas as pl
from jax.experimental.pallas import tpu as pltpu

F32 = jnp.float32
BF16 = jnp.bfloat16

EPS = 1e-6
GRID_W = 64
ROPE_THETA = 10000.0
ATTN_HEADS = 16
ATTN_KV_HEADS = 4
ATTN_HEAD_DIM = 128
RET_HEADS = 8
RET_QK_DIM = 256
RET_V_DIM = 512
N_EXPERTS = 16
CAPACITY_FACTOR = 2
N_MOD = 6
HYENA_EMB = 33
HYENA_TARGET = 1e-2
HYENA_FAST = 0.3
HYENA_SLOW = 1.5
HYENA_SHIFT = 0.0

LANE = 128
VMEM_LIMIT = 52 * 1024 * 1024


def _cparams(*sem):
    return pltpu.CompilerParams(dimension_semantics=sem, vmem_limit_bytes=VMEM_LIMIT)


def _silu(v):
    return v / (1.0 + jnp.exp(-v))


def _split_bf16(v):
    hi = v.astype(BF16)
    lo = (v - hi.astype(F32)).astype(BF16)
    return hi, lo


def _dot(a, b):
    return jnp.dot(a, b, preferred_element_type=F32)


def _dot3(a, b):
    ah, al = _split_bf16(a)
    bh, bl = _split_bf16(b)
    return _dot(ah, bh) + _dot(al, bh) + _dot(ah, bl)


def _dot_nt(a, b):
    return lax.dot_general(a, b, (((1,), (1,)), ((), ())), preferred_element_type=F32)


def _dot_tn(a, b):
    return lax.dot_general(a, b, (((0,), (0,)), ((), ())), preferred_element_type=F32)


def _mod_kernel(c_ref, w_ref, b_ref, o_ref):
    a = _silu(c_ref[...])
    o_ref[...] = _dot3(a, w_ref[...]) + b_ref[...]


def _modulation(c_all, w_mod, b_mod):
    depth, d, n = w_mod.shape
    rows = c_all.shape[0]
    tn = 512
    return pl.pallas_call(
        _mod_kernel,
        grid=(depth, n // tn),
        in_specs=[
            pl.BlockSpec((rows, d), lambda i, j: (0, 0)),
            pl.BlockSpec((None, d, tn), lambda i, j: (i, 0, j)),
            pl.BlockSpec((None, 1, tn), lambda i, j: (i, 0, j)),
        ],
        out_specs=pl.BlockSpec((None, rows, tn), lambda i, j: (i, 0, j)),
        out_shape=jax.ShapeDtypeStruct((depth, rows, n), F32),
        compiler_params=_cparams("parallel", "parallel"),
        name="modulation",
    )(c_all, w_mod, b_mod.reshape(depth, 1, n))


def _norm_mod(x, nw, sh, sc):
    ms = jnp.mean(x * x, axis=-1, keepdims=True)
    y = x * lax.rsqrt(ms + EPS) * nw
    return y * (1.0 + sc) + sh


def _rope_slices(acc, colw, cos, sin, *, head_norm, half, table_w):
    tn = acc.shape[1]
    lane = lax.broadcasted_iota(jnp.int32, (1, LANE), 1)
    first_half = (lane % (2 * half)) < half
    outs = []
    for s in range(tn // LANE):
        xs = acc[:, s * LANE:(s + 1) * LANE]
        if head_norm:
            xs = xs * lax.rsqrt(jnp.mean(xs * xs, axis=-1, keepdims=True) + EPS)
        xs = xs * colw[:, s * LANE:(s + 1) * LANE]
        t0 = (s * LANE) % table_w
        cs = cos[:, t0:t0 + LANE]
        sn = sin[:, t0:t0 + LANE]
        if 2 * half == LANE:
            partner = pltpu.roll(xs, half, axis=1)
        else:
            partner = jnp.where(first_half, pltpu.roll(xs, LANE - half, axis=1), pltpu.roll(xs, half, axis=1))
        outs.append(xs * cs + partner * sn)
    return jnp.concatenate(outs, axis=1)


def _proj_kernel(x_ref, nw_ref, sh_ref, sc_ref, w_ref, colw_ref, cos_ref, sin_ref, o_ref, h_scr,
                 *, rope_blocks, rope_major, col_off, head_norm, half, table_w):
    j = pl.program_id(2)

    @pl.when(j == 0)
    def _():
        h_scr[...] = _norm_mod(x_ref[...], nw_ref[...], sh_ref[...], sc_ref[...]).astype(BF16)

    acc = _dot(h_scr[...], w_ref[...].astype(BF16))
    rope = functools.partial(_rope_slices, head_norm=head_norm, half=half, table_w=table_w)
    if not rope_blocks:
        o_ref[...] = acc.astype(o_ref.dtype)
    elif rope_major:
        o_ref[...] = rope(acc, colw_ref[...], cos_ref[...], sin_ref[...]).astype(o_ref.dtype)

        @pl.when(j + col_off >= rope_blocks)
        def _():
            o_ref[...] = acc.astype(o_ref.dtype)
    else:
        o_ref[...] = acc.astype(o_ref.dtype)

        @pl.when(j + col_off < rope_blocks)
        def _():
            o_ref[...] = rope(acc, colw_ref[...], cos_ref[...], sin_ref[...]).astype(o_ref.dtype)


def _project(x, nw, sh, sc, w, layer, *, colw=None, cos=None, sin=None, rope_blocks=0, col_off=0, head_norm=False,
             half=32, tn=512, tm=1024):
    b, l, d = x.shape
    n = w.shape[2] - col_off * tn
    per_sample = sh.shape[0] != 1
    if colw is None:
        colw = jnp.ones((1, w.shape[2]), F32)
        cos = jnp.ones((l, LANE), F32)
        sin = jnp.zeros((l, LANE), F32)
    if not per_sample and b > 1:
        out = _project(x.reshape(1, b * l, d), nw, sh, sc, w, layer, colw=colw, cos=jnp.tile(cos, (b, 1)),
                       sin=jnp.tile(sin, (b, 1)), rope_blocks=rope_blocks, col_off=col_off, head_norm=head_norm,
                       half=half, tn=tn, tm=tm)
        return out.reshape(b, l, n)
    tm = min(tm, l)
    mod_map = (lambda bi, i, j: (bi, 0, 0)) if per_sample else (lambda bi, i, j: (0, 0, 0))
    table_w = cos.shape[1]
    rope_major = 2 * (rope_blocks - col_off) > n // tn
    kern = functools.partial(_proj_kernel, rope_blocks=rope_blocks, rope_major=rope_major, col_off=col_off,
                             head_norm=head_norm, half=half, table_w=table_w)
    return pl.pallas_call(
        kern,
        grid=(b, l // tm, n // tn),
        in_specs=[
            pl.BlockSpec((None, tm, d), lambda bi, i, j: (bi, i, 0)),
            pl.BlockSpec((1, d), lambda bi, i, j: (0, 0)),
            pl.BlockSpec((None, 1, d), mod_map),
            pl.BlockSpec((None, 1, d), mod_map),
            pl.BlockSpec((None, d, tn), lambda bi, i, j: (layer, 0, j + col_off)),
            pl.BlockSpec((1, tn), lambda bi, i, j: (0, j + col_off)),
            pl.BlockSpec((tm, table_w), lambda bi, i, j: (i, 0)),
            pl.BlockSpec((tm, table_w), lambda bi, i, j: (i, 0)),
        ],
        out_specs=pl.BlockSpec((None, tm, tn), lambda bi, i, j: (bi, i, j)),
        out_shape=jax.ShapeDtypeStruct((b, l, n), BF16),
        scratch_shapes=[pltpu.VMEM((tm, d), BF16)],
        compiler_params=_cparams("parallel", "parallel", "arbitrary"),
        name="norm_mod_project",
    )(x, nw.reshape(1, d), sh, sc, w, colw, cos, sin)


def _out_proj_kernel(a_ref, w_ref, r_ref, g_ref, o_ref):
    acc = _dot(a_ref[...], w_ref[...].astype(BF16))
    o_ref[...] = r_ref[...] + g_ref[...] * acc


def _out_project(a, w, layer, resid, gate, *, tm=1024, tn=512):
    b, l, k = a.shape
    n = w.shape[2]
    per_sample = gate.shape[0] != 1
    if not per_sample and b > 1:
        out = _out_project(a.reshape(1, b * l, k), w, layer, resid.reshape(1, b * l, n), gate, tm=tm, tn=tn)
        return out.reshape(b, l, n)
    tm = min(tm, l)
    g_map = (lambda bi, i, j: (bi, 0, j)) if per_sample else (lambda bi, i, j: (0, 0, j))
    return pl.pallas_call(
        _out_proj_kernel,
        grid=(b, l // tm, n // tn),
        in_specs=[
            pl.BlockSpec((None, tm, k), lambda bi, i, j: (bi, i, 0)),
            pl.BlockSpec((None, k, tn), lambda bi, i, j: (layer, 0, j)),
            pl.BlockSpec((None, tm, tn), lambda bi, i, j: (bi, i, j)),
            pl.BlockSpec((None, 1, tn), g_map),
        ],
        out_specs=pl.BlockSpec((None, tm, tn), lambda bi, i, j: (bi, i, j)),
        out_shape=jax.ShapeDtypeStruct((b, l, n), F32),
        compiler_params=_cparams("parallel", "parallel", "parallel"),
        name="out_project_residual",
    )(a, w, resid, gate)


def _attn_kernel(q_ref, *refs, groups, n_kv):
    o_ref = refs[2 * n_kv]
    ks = [refs[2 * t][...] for t in range(n_kv)]
    vs = [refs[2 * t + 1][...] for t in range(n_kv)]
    for g in range(groups):
        q = q_ref[:, g * ATTN_HEAD_DIM:(g + 1) * ATTN_HEAD_DIM]
        ss = [_dot_nt(q, k) for k in ks]
        m = functools.reduce(jnp.maximum, [jnp.max(s, axis=-1, keepdims=True) for s in ss])
        ps = [jnp.exp(s - m) for s in ss]
        den = sum(jnp.sum(p, axis=-1, keepdims=True) for p in ps)
        o = sum(_dot(p.astype(BF16), v) for p, v in zip(ps, vs)) / den
        o_ref[:, g * ATTN_HEAD_DIM:(g + 1) * ATTN_HEAD_DIM] = o.astype(o_ref.dtype)


def _attention(pq, kv_sources, *, tq=512):
    b, lq, _ = pq.shape
    hd = ATTN_HEAD_DIM
    groups = ATTN_HEADS // ATTN_KV_HEADS
    gw = groups * hd
    tq = min(tq, lq)
    in_specs = [pl.BlockSpec((None, tq, gw), lambda bi, h, i: (bi, i, h))]
    args = [pq]
    for p, k0, v0 in kv_sources:
        lk = p.shape[1]
        in_specs.append(pl.BlockSpec((None, lk, hd), lambda bi, h, i, o=k0 // hd: (bi, 0, o + h)))
        in_specs.append(pl.BlockSpec((None, lk, hd), lambda bi, h, i, o=v0 // hd: (bi, 0, o + h)))
        args += [p, p]
    return pl.pallas_call(
        functools.partial(_attn_kernel, groups=groups, n_kv=len(kv_sources)),
        grid=(b, ATTN_KV_HEADS, lq // tq),
        in_specs=in_specs,
        out_specs=pl.BlockSpec((None, tq, gw), lambda bi, h, i: (bi, i, h)),
        out_shape=jax.ShapeDtypeStruct((b, lq, ATTN_HEADS * hd), BF16),
        compiler_params=_cparams("parallel", "parallel", "parallel"),
        name="gqa_attention",
    )(*args)


RET_CHUNK = 256


def _ret_readout(o, g):
    of = o * lax.rsqrt(jnp.mean(o * o, axis=-1, keepdims=True) + EPS)
    return (_silu(g.astype(F32)) * of).astype(BF16)


def _ret_kernel(lg_ref, qc_ref, kc_ref, vc_ref, gc_ref, ql_ref, kl_ref, vl_ref, gl_ref, oc_ref, ol_ref,
                ob_scr, sf_scr, sb_scr, *, n_chunks, with_ctx_out):
    c = RET_CHUNK
    h = pl.program_id(1)
    lgf = lg_ref[0, h]
    lgb = lg_ref[1, h]
    row = lax.broadcasted_iota(jnp.int32, (c, 1), 0).astype(F32)
    col = lax.broadcasted_iota(jnp.int32, (1, c), 1).astype(F32)
    diff = row - col
    dmask = jnp.exp(jnp.where(diff >= 0, diff * lgf, -diff * lgb))
    qdec_f = jnp.exp((row + 1.0) * lgf)
    kdec_f = jnp.exp((c - 1.0 - row) * lgf)
    qdec_b = jnp.exp((c - row) * lgb)
    kdec_b = jnp.exp(row * lgb)
    one = jnp.ones((1, 1), F32)
    cdec_f = jnp.exp(one * (c * lgf))
    cdec_b = jnp.exp(one * (c * lgb))

    qc = qc_ref[...]
    kc = kc_ref[...].astype(F32)
    vc = vc_ref[...]
    sf_scr[...] = _dot_tn((kc * kdec_f).astype(BF16), vc)
    sb_scr[...] = _dot_tn((kc * kdec_b).astype(BF16), vc)
    if with_ctx_out:
        inner = (_dot_nt(qc, kc_ref[...]) * dmask).astype(BF16)
        oc_ref[...] = _ret_readout(_dot(inner, vc), gc_ref[...])
    else:
        oc_ref[...] = jnp.zeros(oc_ref.shape, oc_ref.dtype)

    def bwd(i, carry):
        ci = n_chunks - 1 - i
        sl = pl.ds(pl.multiple_of(ci * c, c), c)
        q = ql_ref[sl, :].astype(F32)
        k = kl_ref[sl, :].astype(F32)
        ob_scr[sl, :] = _dot((q * qdec_b).astype(BF16), sb_scr[...].astype(BF16))
        sb_scr[...] = sb_scr[...] * cdec_b + _dot_tn((k * kdec_b).astype(BF16), vl_ref[sl, :])
        return carry

    lax.fori_loop(0, n_chunks, bwd, 0)

    def fwd(ci, carry):
        sl = pl.ds(pl.multiple_of(ci * c, c), c)
        qb = ql_ref[sl, :]
        kb = kl_ref[sl, :]
        v = vl_ref[sl, :]
        q = qb.astype(F32)
        k = kb.astype(F32)
        inner = (_dot_nt(qb, kb) * dmask).astype(BF16)
        o = ob_scr[sl, :] + _dot(inner, v) + _dot((q * qdec_f).astype(BF16), sf_scr[...].astype(BF16))
        sf_scr[...] = sf_scr[...] * cdec_f + _dot_tn((k * kdec_f).astype(BF16), v)
        ol_ref[sl, :] = _ret_readout(o, gl_ref[sl, :])
        return carry

    lax.fori_loop(0, n_chunks, fwd, 0)


def _retention(log_g, p_ctx, p_lat, with_ctx_out):
    b, l, _ = p_lat.shape
    lc = p_ctx.shape[1]
    assert lc == RET_CHUNK and l % RET_CHUNK == 0
    dk, dv, hh = RET_QK_DIM, RET_V_DIM, RET_HEADS
    k_off = hh * dk // dk
    v_off = 2 * hh * dk // dv
    g_off = v_off + hh

    def specs(ln):
        return [
            pl.BlockSpec((None, ln, dk), lambda bi, h: (bi, 0, h)),
            pl.BlockSpec((None, ln, dk), lambda bi, h: (bi, 0, k_off + h)),
            pl.BlockSpec((None, ln, dv), lambda bi, h: (bi, 0, v_off + h)),
            pl.BlockSpec((None, ln, dv), lambda bi, h: (bi, 0, g_off + h)),
        ]

    return pl.pallas_call(
        functools.partial(_ret_kernel, n_chunks=l // RET_CHUNK, with_ctx_out=with_ctx_out),
        grid=(b, hh),
        in_specs=[pl.BlockSpec(memory_space=pltpu.SMEM)] + specs(lc) + specs(l),
        out_specs=[
            pl.BlockSpec((None, lc, dv), lambda bi, h: (bi, 0, h)),
            pl.BlockSpec((None, l, dv), lambda bi, h: (bi, 0, h)),
        ],
        out_shape=[
            jax.ShapeDtypeStruct((b, lc, hh * dv), BF16),
            jax.ShapeDtypeStruct((b, l, hh * dv), BF16),
        ],
        scratch_shapes=[pltpu.VMEM((l, dv), F32), pltpu.VMEM((dk, dv), F32), pltpu.VMEM((dk, dv), F32)],
        compiler_params=_cparams("parallel", "parallel"),
        name="retention",
    )(log_g, p_ctx, p_ctx, p_ctx, p_ctx, p_lat, p_lat, p_lat, p_lat)


def _hy_gate_kernel(x0_ref, x1_ref, v_ref, cw0_ref, cw1_ref, cwv_ref, cb0_ref, cb1_ref, cbv_ref, u_ref, g_ref):
    l = x0_ref.shape[0]
    t = lax.broadcasted_iota(jnp.int32, (l, 1), 0)

    def conv3(z_ref, cw_ref, cb_ref):
        z = z_ref[...].astype(F32)
        prev = jnp.where(t == 0, 0.0, pltpu.roll(z, 1, axis=0))
        nxt = jnp.where(t == l - 1, 0.0, pltpu.roll(z, l - 1, axis=0))
        cw = cw_ref[...]
        return prev * cw[0:1, :] + z * cw[1:2, :] + nxt * cw[2:3, :] + cb_ref[...]

    x1 = conv3(x1_ref, cw1_ref, cb1_ref)
    v = conv3(v_ref, cwv_ref, cbv_ref)
    u_ref[...] = (v * x1).astype(u_ref.dtype)
    g_ref[...] = conv3(x0_ref, cw0_ref, cb0_ref).astype(g_ref.dtype)


def _hyena_gate(z, conv_w, conv_b, *, tn=256):
    b, l, d3 = z.shape
    d = d3 // 3
    nb = d // tn
    cb = conv_b.reshape(1, d3)
    zs = [pl.BlockSpec((None, l, tn), (lambda bi, j, o=o: (bi, 0, j + o * nb))) for o in range(3)]
    ws = [pl.BlockSpec((3, tn), (lambda bi, j, o=o: (0, j + o * nb))) for o in range(3)]
    bs = [pl.BlockSpec((1, tn), (lambda bi, j, o=o: (0, j + o * nb))) for o in range(3)]
    return pl.pallas_call(
        _hy_gate_kernel,
        grid=(b, nb),
        in_specs=zs + ws + bs,
        out_specs=[pl.BlockSpec((None, l, tn), lambda bi, j: (bi, 0, j))] * 2,
        out_shape=[jax.ShapeDtypeStruct((b, l, d), BF16)] * 2,
        compiler_params=_cparams("parallel", "parallel"),
        name="hyena_conv3_gate",
    )(z, z, z, conv_w, conv_w, conv_w, cb, cb, cb)


def _hy_filter_kernel(z_ref, w1_ref, b1_ref, f1_ref, w2_ref, b2_ref, f2_ref, w3f_ref, w3b_ref, dec_ref, o_ref):
    l = z_ref.shape[0]
    h = jnp.sin(f1_ref[...] * (_dot3(z_ref[...], w1_ref[...]) + b1_ref[...]))
    h = jnp.sin(f2_ref[...] * (_dot3(h, w2_ref[...]) + b2_ref[...]))
    decay = dec_ref[...] + HYENA_SHIFT
    hf = _dot3(h, w3f_ref[...]) * decay
    hb = _dot3(h, w3b_ref[...]) * decay
    t = lax.broadcasted_iota(jnp.int32, (l, 1), 0)
    hb = jnp.where(t == 0, 0.0, hb)
    norm = jnp.sum(jnp.abs(hf), axis=0, keepdims=True) + jnp.sum(jnp.abs(hb), axis=0, keepdims=True)
    o_ref[0] = (hf / norm).astype(o_ref.dtype)
    o_ref[1] = (hb / norm).astype(o_ref.dtype)


def _hyena_filter(l, d, w1, b1, fr1, w2, b2, fr2, w3, *, tn=256):
    t = jnp.linspace(0.0, 1.0, l, dtype=F32)[:, None]
    bands = (HYENA_EMB - 1) // 2
    w = 2.0 * math.pi * jnp.arange(l, dtype=F32)[:, None] / l
    f = jnp.linspace(1e-4, bands - 1, bands, dtype=F32)[None, :]
    z = jnp.concatenate([t, jnp.cos(f * w), -jnp.sin(f * w)], axis=-1)
    deltas = jnp.abs(jnp.linspace(math.log(HYENA_TARGET) / HYENA_SLOW, math.log(HYENA_TARGET) / HYENA_FAST, d, dtype=F32))
    decay = jnp.exp(-t * deltas)
    fw = w1.shape[1]
    pad = lambda a, r, c: jnp.pad(a.astype(F32), ((0, r - a.shape[0]), (0, c - a.shape[1])))
    z = pad(z, l, LANE)
    w1p = pad(w1, LANE, LANE)
    w2p = pad(w2, LANE, LANE)
    w3p = pad(w3, LANE, 2 * d)
    vec = lambda a: pad(a.reshape(1, fw), 1, LANE)
    nb = d // tn
    full = lambda shape: pl.BlockSpec(shape, lambda j: (0, 0))
    return pl.pallas_call(
        _hy_filter_kernel,
        grid=(nb,),
        in_specs=[full((l, LANE)), full((LANE, LANE)), full((1, LANE)), full((1, LANE)), full((LANE, LANE)),
                  full((1, LANE)), full((1, LANE)),
                  pl.BlockSpec((LANE, tn), lambda j: (0, j)),
                  pl.BlockSpec((LANE, tn), lambda j: (0, j + nb)),
                  pl.BlockSpec((l, tn), lambda j: (0, j))],
        out_specs=pl.BlockSpec((2, l, tn), lambda j: (0, 0, j)),
        out_shape=jax.ShapeDtypeStruct((2, l, d), BF16),
        compiler_params=_cparams("parallel"),
        name="hyena_filter",
    )(z, w1p, vec(b1), vec(fr1), w2p, vec(b2), vec(fr2), w3p, w3p, decay)


def _dft_tables(l):
    n = 2 * l
    k = jnp.arange(l, dtype=jnp.int32)
    nb = 1 << ((l.bit_length() - 1 + 1) // 2)
    na = l // nb
    theta = lambda m: (m % n).astype(F32) * (2.0 * math.pi / n)
    ang_a = theta(k[:, None] * (jnp.arange(na, dtype=jnp.int32) * nb)[None, :])[:, :, None]
    ang_b = theta(k[:, None] * jnp.arange(nb, dtype=jnp.int32)[None, :])[:, None, :]
    cs = (jnp.cos(ang_a) * jnp.cos(ang_b) - jnp.sin(ang_a) * jnp.sin(ang_b)).reshape(l, l)
    sn = (jnp.sin(ang_a) * jnp.cos(ang_b) + jnp.cos(ang_a) * jnp.sin(ang_b)).reshape(l, l)
    alt = jnp.where(k % 2 == 0, 1.0, -1.0).astype(F32)
    f_b = jnp.where(k[:, None] == 0, alt[None, :], -sn)
    fwd = jnp.concatenate([cs, f_b], axis=0)
    wa = jnp.where(k[None, :] == 0, 1.0, 2.0) * cs.T
    wb = jnp.where(k[None, :] == 0, alt[:, None], -2.0 * sn.T)
    inv = jnp.concatenate([wa, wb], axis=1) * (1.0 / n)
    return fwd.astype(BF16), inv.astype(BF16)


def _dft_raw_kernel(fa_ref, fb_ref, u_ref, o_ref):
    u = u_ref[...]
    o_ref[0] = _dot(fa_ref[...], u)
    o_ref[1] = _dot(fb_ref[...], u)


def _dft_raw(fwd, u, *, tm=512, tn=512):
    b, l, d = u.shape
    tm, tn = min(tm, l), min(tn, d)
    nb = l // tm
    return pl.pallas_call(
        _dft_raw_kernel,
        grid=(b, d // tn, nb),
        in_specs=[pl.BlockSpec((tm, l), lambda bi, j, i: (i, 0)),
                  pl.BlockSpec((tm, l), lambda bi, j, i: (i + nb, 0)),
                  pl.BlockSpec((None, l, tn), lambda bi, j, i: (bi, 0, j))],
        out_specs=pl.BlockSpec((None, 2, tm, tn), lambda bi, j, i: (bi, 0, i, j)),
        out_shape=jax.ShapeDtypeStruct((b, 2, l, d), F32),
        compiler_params=_cparams("parallel", "parallel", "parallel"),
        name="hyena_dft_filter",
    )(fwd, fwd, u)


def _dft_mul_kernel(fa_ref, fb_ref, u_ref, hs_ref, o_ref):
    i = pl.program_id(2)
    u = u_ref[...]
    ua = _dot(fa_ref[...], u)
    ub = _dot(fb_ref[...], u)
    first = jnp.logical_and(lax.broadcasted_iota(jnp.int32, (ua.shape[0], 1), 0) == 0, i == 0)
    ha = hs_ref[0, 0] + hs_ref[1, 0]
    hb = jnp.where(first, hs_ref[0, 1] + hs_ref[1, 1], hs_ref[0, 1] - hs_ref[1, 1])
    pa = jnp.where(first, ua * ha, ua * ha - ub * hb)
    pb = jnp.where(first, ub * hb, ua * hb + ub * ha)
    o_ref[0] = pa.astype(o_ref.dtype)
    o_ref[1] = pb.astype(o_ref.dtype)


def _dft_mul(fwd, u, hspec, *, tm=512, tn=512):
    b, l, d = u.shape
    tm, tn = min(tm, l), min(tn, d)
    nb = l // tm
    out = pl.pallas_call(
        _dft_mul_kernel,
        grid=(b, d // tn, nb),
        in_specs=[pl.BlockSpec((tm, l), lambda bi, j, i: (i, 0)),
                  pl.BlockSpec((tm, l), lambda bi, j, i: (i + nb, 0)),
                  pl.BlockSpec((None, l, tn), lambda bi, j, i: (bi, 0, j)),
                  pl.BlockSpec((2, 2, tm, tn), lambda bi, j, i: (0, 0, i, j))],
        out_specs=pl.BlockSpec((None, 2, tm, tn), lambda bi, j, i: (bi, 0, i, j)),
        out_shape=jax.ShapeDtypeStruct((b, 2, l, d), BF16),
        compiler_params=_cparams("parallel", "parallel", "parallel"),
        name="hyena_dft_forward",
    )(fwd, fwd, u, hspec)
    return out.reshape(b, 2 * l, d)


def _idft_kernel(g_ref, p_ref, u_ref, x0_ref, skip_ref, o_ref):
    y = _dot(g_ref[...], p_ref[...]) + u_ref[...].astype(F32) * skip_ref[...]
    o_ref[...] = (y * x0_ref[...].astype(F32)).astype(o_ref.dtype)


def _idft_gate(inv, p, u, x0, skip, *, tm=512, tn=512):
    b, l, d = u.shape
    tm, tn = min(tm, l), min(tn, d)
    return pl.pallas_call(
        _idft_kernel,
        grid=(b, d // tn, l // tm),
        in_specs=[pl.BlockSpec((tm, 2 * l), lambda bi, j, i: (i, 0)),
                  pl.BlockSpec((None, 2 * l, tn), lambda bi, j, i: (bi, 0, j)),
                  pl.BlockSpec((None, tm, tn), lambda bi, j, i: (bi, i, j)),
                  pl.BlockSpec((None, tm, tn), lambda bi, j, i: (bi, i, j)),
                  pl.BlockSpec((1, tn), lambda bi, j, i: (0, j))],
        out_specs=pl.BlockSpec((None, tm, tn), lambda bi, j, i: (bi, i, j)),
        out_shape=jax.ShapeDtypeStruct((b, l, d), BF16),
        compiler_params=_cparams("parallel", "parallel", "parallel"),
        name="hyena_dft_inverse",
    )(inv, p, u, x0, skip.reshape(1, d))


def _hyena_operator(z, conv_w, conv_b, fparams, skip):
    b, l, d3 = z.shape
    d = d3 // 3
    u, x0 = _hyena_gate(z, conv_w, conv_b)
    taps = _hyena_filter(l, d, *fparams)
    fwd, inv = _dft_tables(l)
    hspec = _dft_raw(fwd, taps)
    p = _dft_mul(fwd, u, hspec)
    return _idft_gate(inv, p, u, x0, skip)


def _router_kernel(x_ref, nw_ref, sh_ref, sc_ref, wr_ref, h_ref, lg_ref):
    h = _norm_mod(x_ref[...], nw_ref[...], sh_ref[...], sc_ref[...])
    h_ref[...] = h.astype(BF16)
    lg_ref[...] = _dot3(h, wr_ref[...])


def _norm_mod_router(x, nw, sh, sc, w_router, *, tm=512):
    b, l, d = x.shape
    n_e = w_router.shape[1]
    e = LANE
    w_router = jnp.pad(w_router, ((0, 0), (0, e - n_e)))
    tm = min(tm, l)
    per_sample = sh.shape[0] != 1
    mod_map = (lambda bi, i: (bi, 0, 0)) if per_sample else (lambda bi, i: (0, 0, 0))
    return pl.pallas_call(
        _router_kernel,
        grid=(b, l // tm),
        in_specs=[
            pl.BlockSpec((None, tm, d), lambda bi, i: (bi, i, 0)),
            pl.BlockSpec((1, d), lambda bi, i: (0, 0)),
            pl.BlockSpec((None, 1, d), mod_map),
            pl.BlockSpec((None, 1, d), mod_map),
            pl.BlockSpec((d, e), lambda bi, i: (0, 0)),
        ],
        out_specs=[pl.BlockSpec((None, tm, d), lambda bi, i: (bi, i, 0)),
                   pl.BlockSpec((None, tm, e), lambda bi, i: (bi, i, 0))],
        out_shape=[jax.ShapeDtypeStruct((b, l, d), BF16), jax.ShapeDtypeStruct((b, l, e), F32)],
        compiler_params=_cparams("parallel", "parallel"),
        name="norm_mod_router",
    )(x, nw.reshape(1, d), sh, sc, w_router)


def _route_kernel(lg_ref, tri_ref, pos_ref, prob_ref, *, cap):
    lg = lg_ref[...]
    m = jnp.max(lg, axis=0, keepdims=True)
    ex = jnp.exp(lg - m)
    probs = ex / jnp.sum(ex, axis=0, keepdims=True)
    bits = lax.bitcast_convert_type(probs, jnp.int32)

    def count(mask):
        return jnp.sum(jnp.where(mask, 1.0, 0.0), axis=1, keepdims=True)

    def step(i, thr):
        trial = thr | lax.shift_left(jnp.int32(1), 30 - i)
        return jnp.where(count(bits >= trial) >= cap, trial, thr)

    thr = lax.fori_loop(0, 31, step, jnp.zeros((lg.shape[0], 1), jnp.int32))
    gt = bits > thr
    eq = bits == thr
    need = cap - count(gt).astype(jnp.int32)
    both = jnp.concatenate([jnp.where(gt, 1.0, 0.0), jnp.where(eq, 1.0, 0.0)], axis=0).astype(BF16)
    csum = _dot(both, tri_ref[...])
    e = lg.shape[0]
    rank_gt = csum[:e].astype(jnp.int32)
    rank_eq = csum[e:].astype(jnp.int32)
    sel = jnp.logical_or(gt, jnp.logical_and(eq, rank_eq < need))
    pos = rank_gt + jnp.minimum(rank_eq, need)
    pos_ref[...] = jnp.where(sel, pos, -1)
    prob_ref[...] = probs


def _route(logits_t, cap):
    b, e, n = logits_t.shape
    idx = jnp.arange(n, dtype=jnp.int32)
    tri = (idx[:, None] < idx[None, :]).astype(BF16)
    return pl.pallas_call(
        functools.partial(_route_kernel, cap=cap),
        grid=(b,),
        in_specs=[pl.BlockSpec((None, e, n), lambda bi: (bi, 0, 0)),
                  pl.BlockSpec((n, n), lambda bi: (0, 0))],
        out_specs=[pl.BlockSpec((None, e, n), lambda bi: (bi, 0, 0))] * 2,
        out_shape=[jax.ShapeDtypeStruct((b, e, n), jnp.int32), jax.ShapeDtypeStruct((b, e, n), F32)],
        compiler_params=_cparams("parallel"),
        name="expert_choice_route",
    )(logits_t, tri)


def _gather_kernel(h_ref, pos_ref, prob_ref, xg_ref, gate_ref, *, cap):
    n = h_ref.shape[0]
    slot = lax.broadcasted_iota(jnp.int32, (cap, n), 0)
    match = slot == pos_ref[...]
    onehot = jnp.where(match, 1.0, 0.0).astype(BF16)
    xg_ref[...] = _dot(onehot, h_ref[...]).astype(xg_ref.dtype)
    gate_ref[...] = jnp.sum(jnp.where(match, prob_ref[...], 0.0), axis=1, keepdims=True)


def _gather_into_kernel(h_ref, pos_ref, prob_ref, xg_in, gate_in, xg_ref, gate_ref, *, cap):
    del xg_in, gate_in
    _gather_kernel(h_ref, pos_ref, prob_ref, xg_ref, gate_ref, cap=cap)


def _moe_gather(h, pos, probs, cap, m_total, row0, buffers=None):
    b, n, d = h.shape
    e = pos.shape[1]
    blk0 = row0 // cap
    row = lambda a: a.reshape(b, e, 1, n)
    in_specs = [pl.BlockSpec((None, n, d), lambda bi, ei: (bi, 0, 0)),
                pl.BlockSpec((None, None, 1, n), lambda bi, ei: (bi, ei, 0, 0)),
                pl.BlockSpec((None, None, 1, n), lambda bi, ei: (bi, ei, 0, 0))]
    args = [h, row(pos), row(probs)]
    kern = functools.partial(_gather_kernel, cap=cap)
    aliases = {}
    if buffers is not None:
        in_specs += [pl.BlockSpec(memory_space=pl.ANY)] * 2
        args += list(buffers)
        kern = functools.partial(_gather_into_kernel, cap=cap)
        aliases = {3: 0, 4: 1}
    return pl.pallas_call(
        kern,
        grid=(b, e),
        in_specs=in_specs,
        out_specs=[pl.BlockSpec((None, cap, d), lambda bi, ei: (ei, blk0 + bi, 0)),
                   pl.BlockSpec((None, cap, 1), lambda bi, ei: (ei, blk0 + bi, 0))],
        out_shape=[jax.ShapeDtypeStruct((e, m_total, d), BF16), jax.ShapeDtypeStruct((e, m_total, 1), F32)],
        input_output_aliases=aliases,
        compiler_params=_cparams("parallel", "parallel"),
        name="moe_gather",
    )(*args)


def _ffn_kernel(x_ref, g_ref, wg_ref, wu_ref, wd_ref, o_ref, h_scr, *, n_up, tf):
    s = pl.program_id(2)

    @pl.when(s < n_up)
    def _():
        x = x_ref[...]
        a = _dot(x, wg_ref[...].astype(BF16))
        u = _dot(x, wu_ref[...].astype(BF16))
        h_scr[s] = (_silu(a) * u * g_ref[...]).astype(BF16)

    @pl.when(s >= n_up)
    def _():
        y = _dot(h_scr[0], wd_ref[0:tf, :].astype(BF16))
        for f in range(1, n_up):
            y += _dot(h_scr[f], wd_ref[f * tf:(f + 1) * tf, :].astype(BF16))
        o_ref[...] = y.astype(o_ref.dtype)


def _moe_ffn(xg, gate, w_gate, w_up, w_down, layer, *, tf=512, tn=512):
    e, m, d = xg.shape
    ff = w_gate.shape[3]
    tm = max(t for t in range(16, 1153, 16) if m % t == 0)
    n_up, n_down = ff // tf, d // tn
    up_map = lambda ei, i, s: (layer, ei, 0, jnp.minimum(s, n_up - 1))
    down_blk = lambda s: jnp.maximum(s - n_up, 0)
    return pl.pallas_call(
        functools.partial(_ffn_kernel, n_up=n_up, tf=tf),
        grid=(e, m // tm, n_up + n_down),
        in_specs=[pl.BlockSpec((None, tm, d), lambda ei, i, s: (ei, i, 0)),
                  pl.BlockSpec((None, tm, 1), lambda ei, i, s: (ei, i, 0)),
                  pl.BlockSpec((None, None, d, tf), up_map),
                  pl.BlockSpec((None, None, d, tf), up_map),
                  pl.BlockSpec((None, None, ff, tn), lambda ei, i, s: (layer, ei, 0, down_blk(s)))],
        out_specs=pl.BlockSpec((None, tm, tn), lambda ei, i, s: (ei, i, down_blk(s))),
        out_shape=jax.ShapeDtypeStruct((e, m, d), BF16),
        scratch_shapes=[pltpu.VMEM((n_up, tm, tf), BF16)],
        compiler_params=_cparams("parallel", "parallel", "arbitrary"),
        name="moe_expert_ffn",
    )(xg, gate, w_gate, w_up, w_down)


def _combine_kernel(post_ref, y_ref, r_ref, g_ref, o_ref, pt_scr, *, cap):
    j = pl.program_id(2)
    tm, e = post_ref.shape

    @pl.when(j == 0)
    def _():
        post = post_ref[...]
        if cap % LANE == 0:
            slot = lax.broadcasted_iota(jnp.int32, (tm, cap), 1)
            for ei in range(e):
                pt_scr[:, ei * cap:(ei + 1) * cap] = jnp.where(post[:, ei:ei + 1] == slot, 1.0, 0.0).astype(BF16)
        else:
            slot = lax.broadcasted_iota(jnp.int32, (tm, e * cap), 1)
            hit = jnp.zeros((tm, e * cap), jnp.bool_)
            for ei in range(e):
                tgt = jnp.where(post[:, ei:ei + 1] >= 0, post[:, ei:ei + 1] + ei * cap, -1)
                hit = jnp.logical_or(hit, tgt == slot)
            pt_scr[...] = jnp.where(hit, 1.0, 0.0).astype(BF16)

    y = y_ref[...].reshape(e * cap, y_ref.shape[2])
    o_ref[...] = r_ref[...] + g_ref[...] * _dot(pt_scr[...], y)


def _moe_combine(pos_t, y, row0, resid, gate, cap, *, tm=1024, tn=512):
    b, n, e = pos_t.shape
    d = y.shape[2]
    tm = min(tm, n)
    blk0 = row0 // cap
    per_sample = gate.shape[0] != 1
    g_map = (lambda bi, i, j: (bi, 0, j)) if per_sample else (lambda bi, i, j: (0, 0, j))
    return pl.pallas_call(
        functools.partial(_combine_kernel, cap=cap),
        grid=(b, n // tm, d // tn),
        in_specs=[pl.BlockSpec((None, tm, e), lambda bi, i, j: (bi, i, 0)),
                  pl.BlockSpec((e, cap, tn), lambda bi, i, j: (0, blk0 + bi, j)),
                  pl.BlockSpec((None, tm, tn), lambda bi, i, j: (bi, i, j)),
                  pl.BlockSpec((None, 1, tn), g_map)],
        out_specs=pl.BlockSpec((None, tm, tn), lambda bi, i, j: (bi, i, j)),
        out_shape=jax.ShapeDtypeStruct((b, n, d), F32),
        scratch_shapes=[pltpu.VMEM((tm, e * cap), BF16)],
        compiler_params=_cparams("parallel", "parallel", "arbitrary"),
        name="moe_combine_residual",
    )(pos_t, y, resid, gate)


def _moe_block(streams, nw, w_router, w_gate, w_up, w_down, layer):
    e = w_router.shape[1]
    caps = [CAPACITY_FACTOR * s[0].shape[1] // e for s in streams]
    rows = [s[0].shape[0] * cap for s, cap in zip(streams, caps)]
    row0 = [sum(rows[:t]) for t in range(len(streams))]
    m_total = sum(rows)
    routed, buffers = [], None
    for (x, sh, sc, _), cap, r0 in zip(streams, caps, row0):
        h, logits = _norm_mod_router(x, nw, sh, sc, w_router)
        pos, probs = _route(jnp.swapaxes(logits[..., :e], 1, 2), cap)
        buffers = _moe_gather(h, pos, probs, cap, m_total, r0, buffers)
        routed.append(jnp.swapaxes(pos, 1, 2))
    y = _moe_ffn(buffers[0], buffers[1], w_gate, w_up, w_down, layer)
    return [_moe_combine(pos_t, y, r0, x, gate, cap)
            for (x, _, _, gate), pos_t, cap, r0 in zip(streams, routed, caps, row0)]


def _final_norm_kernel(x_ref, w_ref, o_ref):
    x = x_ref[...]
    o_ref[...] = x * lax.rsqrt(jnp.mean(x * x, axis=-1, keepdims=True) + EPS) * w_ref[...]


def _final_norm(x, w, *, tm=512):
    b, l, d = x.shape
    return pl.pallas_call(
        _final_norm_kernel,
        grid=(b, l // tm),
        in_specs=[pl.BlockSpec((None, tm, d), lambda bi, i: (bi, i, 0)), pl.BlockSpec((1, d), lambda bi, i: (0, 0))],
        out_specs=pl.BlockSpec((None, tm, d), lambda bi, i: (bi, i, 0)),
        out_shape=jax.ShapeDtypeStruct(x.shape, F32),
        compiler_params=_cparams("parallel", "parallel"),
        name="final_rmsnorm",
    )(x, w.reshape(1, d))


def _rope_tables(seq_len, dim):
    rows = seq_len // GRID_W
    row_id = jnp.repeat(jnp.arange(rows, dtype=F32), GRID_W)
    col_id = jnp.tile(jnp.arange(GRID_W, dtype=F32), rows)
    nf = dim // 4
    inv = ROPE_THETA ** (-jnp.arange(nf, dtype=F32) / nf)
    ang_r = row_id[:, None] * inv
    ang_c = col_id[:, None] * inv
    cos = jnp.concatenate([jnp.cos(ang_r)] * 2 + [jnp.cos(ang_c)] * 2, axis=-1)
    sin = jnp.concatenate([-jnp.sin(ang_r), jnp.sin(ang_r), -jnp.sin(ang_c), jnp.sin(ang_c)], axis=-1)
    return cos, sin


def _identity_tables(seq_len, dim):
    return jnp.ones((seq_len, dim), F32), jnp.zeros((seq_len, dim), F32)


def _attention_mixer(x, ctx, nw, mods_lat, mods_ctx, w_qkv, q_norm, k_norm, w_o, layer, with_ctx_out):
    hd = ATTN_HEAD_DIM
    nq, nkv = ATTN_HEADS * hd, ATTN_KV_HEADS * hd
    l, lc = x.shape[1], ctx.shape[1]
    tn = 512
    colw = jnp.concatenate([jnp.tile(q_norm.astype(F32) * (hd ** -0.5), ATTN_HEADS),
                            jnp.tile(k_norm.astype(F32), ATTN_KV_HEADS), jnp.ones((nkv,), F32)]).reshape(1, -1)
    rope_blocks = (nq + nkv) // tn
    cos, sin = _rope_tables(l, hd)
    icos, isin = _identity_tables(lc, hd)
    kw = dict(colw=colw, rope_blocks=rope_blocks, head_norm=True, half=hd // 4, tn=tn)
    p_lat = _project(x, nw, mods_lat[0], mods_lat[1], w_qkv, layer, cos=cos, sin=sin, **kw)
    off = 0 if with_ctx_out else nq // tn
    p_ctx = _project(ctx, nw, mods_ctx[0], mods_ctx[1], w_qkv, layer, cos=icos, sin=isin, col_off=off, **kw)
    kc0 = nq - off * tn
    ctx_kv = (p_ctx, kc0, kc0 + nkv)
    o_lat = _attention(p_lat, [ctx_kv, (p_lat, nq, nq + nkv)])
    x = _out_project(o_lat, w_o, layer, x, mods_lat[2])
    if with_ctx_out:
        ctx = _out_project(_attention(p_ctx, [ctx_kv]), w_o, layer, ctx, mods_ctx[2])
    return x, ctx


def _retention_mixer(x, ctx, nw, mods_lat, mods_ctx, w_in, decay_logit, w_o, layer, with_ctx_out):
    dk, hh = RET_QK_DIM, RET_HEADS
    l, lc = x.shape[1], ctx.shape[1]
    tn = 512
    log_g = jax.nn.log_sigmoid(decay_logit.astype(F32))
    colw = jnp.concatenate([jnp.ones((hh * dk,), F32), jnp.full((hh * dk,), dk ** -0.5, F32),
                            jnp.ones((w_in.shape[2] - 2 * hh * dk,), F32)]).reshape(1, -1)
    rope_blocks = 2 * hh * dk // tn
    cos, sin = _rope_tables(l, dk)
    icos, isin = _identity_tables(lc, dk)
    kw = dict(colw=colw, rope_blocks=rope_blocks, head_norm=False, half=dk // 4, tn=tn)
    p_lat = _project(x, nw, mods_lat[0], mods_lat[1], w_in, layer, cos=cos, sin=sin, **kw)
    p_ctx = _project(ctx, nw, mods_ctx[0], mods_ctx[1], w_in, layer, cos=icos, sin=isin, **kw)
    r_ctx, r_lat = _retention(log_g, p_ctx, p_lat, with_ctx_out)
    x = _out_project(r_lat, w_o, layer, x, mods_lat[2])
    if with_ctx_out:
        ctx = _out_project(r_ctx, w_o, layer, ctx, mods_ctx[2])
    return x, ctx


def _hyena_mixer(x, ctx, nw, mods_lat, mods_ctx, w_in, conv_w, conv_b, fparams, skip, w_out, layer, with_ctx_out):
    z = _project(x, nw, mods_lat[0], mods_lat[1], w_in, layer)
    x = _out_project(_hyena_operator(z, conv_w, conv_b, fparams, skip), w_out, layer, x, mods_lat[2])
    if with_ctx_out:
        zc = _project(ctx, nw, mods_ctx[0], mods_ctx[1], w_in, layer)
        ctx = _out_project(_hyena_operator(zc, conv_w, conv_b, fparams, skip), w_out, layer, ctx, mods_ctx[2])
    return x, ctx


def kernel(x, c, ctx, c_ctx, w_mod, b_mod, norm_w, attn_w_qkv, attn_q_norm, attn_k_norm, attn_w_o, ret_w_in, ret_decay_logit, ret_w_o, hy_w_in, hy_conv_w, hy_conv_b, hy_f_w1, hy_f_b1, hy_f_freq1, hy_f_w2, hy_f_b2, hy_f_freq2, hy_f_w3, hy_skip, hy_w_out, moe_router, moe_w_gate, moe_w_up, moe_w_down, final_norm_w):
    depth = w_mod.shape[0]
    b, _, d = x.shape
    rows = -(-(b + 1) // 8) * 8
    c_all = jnp.concatenate([c, c_ctx[None, :], jnp.zeros((rows - b - 1, d), F32)], axis=0)
    mod = _modulation(c_all, w_mod, b_mod)

    for i in range(depth):
        kind, j = i % 3, i // 3
        with_ctx = i < depth - 1
        m = mod[i].reshape(rows, N_MOD, d)
        mods_lat = [m[:b, t][:, None, :] for t in range(N_MOD)]
        mods_ctx = [m[b:b + 1, t][:, None, :] for t in range(N_MOD)]
        nw1, nw2 = norm_w[i, 0], norm_w[i, 1]
        if kind == 0:
            x, ctx = _attention_mixer(x, ctx, nw1, mods_lat, mods_ctx, attn_w_qkv, attn_q_norm[j], attn_k_norm[j],
                                      attn_w_o, j, with_ctx)
        elif kind == 1:
            x, ctx = _retention_mixer(x, ctx, nw1, mods_lat, mods_ctx, ret_w_in, ret_decay_logit[j], ret_w_o,
                                      j, with_ctx)
        else:
            fparams = (hy_f_w1[j], hy_f_b1[j], hy_f_freq1[j], hy_f_w2[j], hy_f_b2[j], hy_f_freq2[j], hy_f_w3[j])
            x, ctx = _hyena_mixer(x, ctx, nw1, mods_lat, mods_ctx, hy_w_in, hy_conv_w[j], hy_conv_b[j], fparams,
                                  hy_skip[j], hy_w_out, j, with_ctx)
        streams = [(x, mods_lat[3], mods_lat[4], mods_lat[5])]
        if with_ctx:
            streams.append((ctx, mods_ctx[3], mods_ctx[4], mods_ctx[5]))
        outs = _moe_block(streams, nw2, moe_router[i], moe_w_gate, moe_w_up, moe_w_down, i)
        x = outs[0]
        if with_ctx:
            ctx = outs[1]
    return _final_norm(x, final_norm_w)
```

```python
import functools
import math

import jax
import jax.numpy as jnp
from jax import lax
from jax.experimental import pallas as pl
from jax.experimental.pallas import tpu as pltpu

F32 = jnp.float32
BF16 = jnp.bfloat16

EPS = 1e-6
GRID_W = 64
ROPE_THETA = 10000.0
ATTN_HEADS = 16
ATTN_KV_HEADS = 4
ATTN_HEAD_DIM = 128
RET_HEADS = 8
RET_QK_DIM = 256
RET_V_DIM = 512
N_EXPERTS = 16
CAPACITY_FACTOR = 2
N_MOD = 6
HYENA_EMB = 33
HYENA_TARGET = 1e-2
HYENA_FAST = 0.3
HYENA_SLOW = 1.5
HYENA_SHIFT = 0.0

LANE = 128
VMEM_LIMIT = 52 * 1024 * 1024


def _cparams(*sem):
    return pltpu.CompilerParams(dimension_semantics=sem, vmem_limit_bytes=VMEM_LIMIT)


def _silu(v):
    return v / (1.0 + jnp.exp(-v))


def _split_bf16(v):
    hi = v.astype(BF16)
    lo = (v - hi.astype(F32)).astype(BF16)
    return hi, lo


def _dot(a, b):
    return jnp.dot(a, b, preferred_element_type=F32)


def _dot3(a, b):
    ah, al = _split_bf16(a)
    bh, bl = _split_bf16(b)
    return _dot(ah, bh) + _dot(al, bh) + _dot(ah, bl)


def _dot_nt(a, b):
    return lax.dot_general(a, b, (((1,), (1,)), ((), ())), preferred_element_type=F32)


def _dot_tn(a, b):
    return lax.dot_general(a, b, (((0,), (0,)), ((), ())), preferred_element_type=F32)


def _mod_kernel(c_ref, w_ref, b_ref, o_ref):
    a = _silu(c_ref[...])
    o_ref[...] = _dot3(a, w_ref[...]) + b_ref[...]


def _modulation(c_all, w_mod, b_mod):
    depth, d, n = w_mod.shape
    rows = c_all.shape[0]
    tn = 512
    return pl.pallas_call(
        _mod_kernel,
        grid=(depth, n // tn),
        in_specs=[
            pl.BlockSpec((rows, d), lambda i, j: (0, 0)),
            pl.BlockSpec((None, d, tn), lambda i, j: (i, 0, j)),
            pl.BlockSpec((None, 1, tn), lambda i, j: (i, 0, j)),
        ],
        out_specs=pl.BlockSpec((None, rows, tn), lambda i, j: (i, 0, j)),
        out_shape=jax.ShapeDtypeStruct((depth, rows, n), F32),
        compiler_params=_cparams("parallel", "parallel"),
        name="modulation",
    )(c_all, w_mod, b_mod.reshape(depth, 1, n))


def _norm_mod(x, nw, sh, sc):
    ms = jnp.mean(x * x, axis=-1, keepdims=True)
    y = x * lax.rsqrt(ms + EPS) * nw
    return y * (1.0 + sc) + sh


def _rope_slices(acc, colw, cos, sin, *, head_norm, half, table_w):
    tn = acc.shape[1]
    lane = lax.broadcasted_iota(jnp.int32, (1, LANE), 1)
    first_half = (lane % (2 * half)) < half
    outs = []
    for s in range(tn // LANE):
        xs = acc[:, s * LANE:(s + 1) * LANE]
        if head_norm:
            xs = xs * lax.rsqrt(jnp.mean(xs * xs, axis=-1, keepdims=True) + EPS)
        xs = xs * colw[:, s * LANE:(s + 1) * LANE]
        t0 = (s * LANE) % table_w
        cs = cos[:, t0:t0 + LANE]
        sn = sin[:, t0:t0 + LANE]
        if 2 * half == LANE:
            partner = pltpu.roll(xs, half, axis=1)
        else:
            partner = jnp.where(first_half, pltpu.roll(xs, LANE - half, axis=1), pltpu.roll(xs, half, axis=1))
        outs.append(xs * cs + partner * sn)
    return jnp.concatenate(outs, axis=1)


def _proj_kernel(x_ref, nw_ref, sh_ref, sc_ref, w_ref, colw_ref, cos_ref, sin_ref, o_ref, h_scr,
                 *, rope_blocks, col_off, head_norm, half, table_w):
    j = pl.program_id(2)

    @pl.when(j == 0)
    def _():
        h_scr[...] = _norm_mod(x_ref[...], nw_ref[...], sh_ref[...], sc_ref[...]).astype(BF16)

    acc = _dot(h_scr[...], w_ref[...].astype(BF16))
    o_ref[...] = acc.astype(o_ref.dtype)
    if rope_blocks:
        @pl.when(j + col_off < rope_blocks)
        def _():
            o_ref[...] = _rope_slices(acc, colw_ref[...], cos_ref[...], sin_ref[...], head_norm=head_norm,
                                      half=half, table_w=table_w).astype(o_ref.dtype)


def _project(x, nw, sh, sc, w, layer, *, colw=None, cos=None, sin=None, rope_blocks=0, col_off=0, head_norm=False,
             half=32, tn=512, tm=1024):
    b, l, d = x.shape
    n = w.shape[2] - col_off * tn
    per_sample = sh.shape[0] != 1
    if colw is None:
        colw = jnp.ones((1, w.shape[2]), F32)
        cos = jnp.ones((l, LANE), F32)
        sin = jnp.zeros((l, LANE), F32)
    if not per_sample and b > 1:
        out = _project(x.reshape(1, b * l, d), nw, sh, sc, w, layer, colw=colw, cos=jnp.tile(cos, (b, 1)),
                       sin=jnp.tile(sin, (b, 1)), rope_blocks=rope_blocks, col_off=col_off, head_norm=head_norm,
                       half=half, tn=tn, tm=tm)
        return out.reshape(b, l, n)
    tm = min(tm, l)
    mod_map = (lambda bi, i, j: (bi, 0, 0)) if per_sample else (lambda bi, i, j: (0, 0, 0))
    table_w = cos.shape[1]
    kern = functools.partial(_proj_kernel, rope_blocks=rope_blocks, col_off=col_off, head_norm=head_norm,
                             half=half, table_w=table_w)
    return pl.pallas_call(
        kern,
        grid=(b, l // tm, n // tn),
        in_specs=[
            pl.BlockSpec((None, tm, d), lambda bi, i, j: (bi, i, 0)),
            pl.BlockSpec((1, d), lambda bi, i, j: (0, 0)),
            pl.BlockSpec((None, 1, d), mod_map),
            pl.BlockSpec((None, 1, d), mod_map),
            pl.BlockSpec((None, d, tn), lambda bi, i, j: (layer, 0, j + col_off)),
            pl.BlockSpec((1, tn), lambda bi, i, j: (0, j + col_off)),
            pl.BlockSpec((tm, table_w), lambda bi, i, j: (i, 0)),
            pl.BlockSpec((tm, table_w), lambda bi, i, j: (i, 0)),
        ],
        out_specs=pl.BlockSpec((None, tm, tn), lambda bi, i, j: (bi, i, j)),
        out_shape=jax.ShapeDtypeStruct((b, l, n), BF16),
        scratch_shapes=[pltpu.VMEM((tm, d), BF16)],
        compiler_params=_cparams("parallel", "parallel", "arbitrary"),
        name="norm_mod_project",
    )(x, nw.reshape(1, d), sh, sc, w, colw, cos, sin)


def _out_proj_kernel(a_ref, w_ref, r_ref, g_ref, o_ref):
    acc = _dot(a_ref[...], w_ref[...].astype(BF16))
    o_ref[...] = r_ref[...] + g_ref[...] * acc


def _out_project(a, w, layer, resid, gate, *, tm=None, tn=512):
    b, l, k = a.shape
    n = w.shape[2]
    if tm is None:
        tm = (2048 * 2048) // k
    per_sample = gate.shape[0] != 1
    if not per_sample and b > 1:
        out = _out_project(a.reshape(1, b * l, k), w, layer, resid.reshape(1, b * l, n), gate, tm=tm, tn=tn)
        return out.reshape(b, l, n)
    tm = min(tm, l)
    g_map = (lambda bi, i, j: (bi, 0, j)) if per_sample else (lambda bi, i, j: (0, 0, j))
    return pl.pallas_call(
        _out_proj_kernel,
        grid=(b, l // tm, n // tn),
        in_specs=[
            pl.BlockSpec((None, tm, k), lambda bi, i, j: (bi, i, 0)),
            pl.BlockSpec((None, k, tn), lambda bi, i, j: (layer, 0, j)),
            pl.BlockSpec((None, tm, tn), lambda bi, i, j: (bi, i, j)),
            pl.BlockSpec((None, 1, tn), g_map),
        ],
        out_specs=pl.BlockSpec((None, tm, tn), lambda bi, i, j: (bi, i, j)),
        out_shape=jax.ShapeDtypeStruct((b, l, n), F32),
        compiler_params=_cparams("parallel", "parallel", "parallel"),
        name="out_project_residual",
    )(a, w, resid, gate)


def _attn_kernel(q_ref, *refs, groups, n_kv, exp_dtype):
    hd = ATTN_HEAD_DIM
    o_ref = refs[2 * n_kv]
    ks = [refs[2 * t][...] for t in range(n_kv)]
    vs = [jnp.concatenate([refs[2 * t + 1][...], jnp.ones((k.shape[0], hd), BF16)], axis=1) for t, k in enumerate(ks)]
    for g in range(groups):
        q = q_ref[:, g * hd:(g + 1) * hd]
        ss = [_dot_nt(q, k) for k in ks]
        m = functools.reduce(jnp.maximum, [jnp.max(s, axis=-1, keepdims=True) for s in ss])
        ps = [jnp.exp((s - m).astype(exp_dtype)).astype(BF16) for s in ss]
        ov = sum(_dot(p, v) for p, v in zip(ps, vs))
        o_ref[:, g * hd:(g + 1) * hd] = (ov[:, :hd] / ov[:, hd:hd + 1]).astype(o_ref.dtype)


def _attention(pq, kv_sources, *, tq=512, exp_dtype=F32):
    b, lq, _ = pq.shape
    hd = ATTN_HEAD_DIM
    groups = ATTN_HEADS // ATTN_KV_HEADS
    gw = groups * hd
    tq = min(tq, lq)
    in_specs = [pl.BlockSpec((None, tq, gw), lambda bi, h, i: (bi, i, h))]
    args = [pq]
    for p, k0, v0 in kv_sources:
        lk = p.shape[1]
        in_specs.append(pl.BlockSpec((None, lk, hd), lambda bi, h, i, o=k0 // hd: (bi, 0, o + h)))
        in_specs.append(pl.BlockSpec((None, lk, hd), lambda bi, h, i, o=v0 // hd: (bi, 0, o + h)))
        args += [p, p]
    return pl.pallas_call(
        functools.partial(_attn_kernel, groups=groups, n_kv=len(kv_sources), exp_dtype=exp_dtype),
        grid=(b, ATTN_KV_HEADS, lq // tq),
        in_specs=in_specs,
        out_specs=pl.BlockSpec((None, tq, gw), lambda bi, h, i: (bi, i, h)),
        out_shape=jax.ShapeDtypeStruct((b, lq, ATTN_HEADS * hd), BF16),
        compiler_params=_cparams("parallel", "parallel", "parallel"),
        name="gqa_attention",
    )(*args)


RET_CHUNK = 256


def _ret_readout(o, g):
    of = o * lax.rsqrt(jnp.mean(o * o, axis=-1, keepdims=True) + EPS)
    return (_silu(g.astype(F32)) * of).astype(BF16)


def _ret_kernel(lg_ref, qc_ref, kc_ref, vc_ref, gc_ref, ql_ref, kl_ref, vl_ref, gl_ref, oc_ref, ol_ref,
                ob_scr, sf_scr, sb_scr, *, n_chunks, with_ctx_out):
    c = RET_CHUNK
    h = pl.program_id(1)
    lgf = lg_ref[0, h]
    lgb = lg_ref[1, h]
    row = lax.broadcasted_iota(jnp.int32, (c, 1), 0).astype(F32)
    col = lax.broadcasted_iota(jnp.int32, (1, c), 1).astype(F32)
    diff = row - col
    dmask = jnp.exp(jnp.where(diff >= 0, diff * lgf, -diff * lgb))
    qdec_f = jnp.exp((row + 1.0) * lgf)
    kdec_f = jnp.exp((c - 1.0 - row) * lgf)
    qdec_b = jnp.exp((c - row) * lgb)
    kdec_b = jnp.exp(row * lgb)
    one = jnp.ones((1, 1), F32)
    cdec_f = jnp.exp(one * (c * lgf))
    cdec_b = jnp.exp(one * (c * lgb))

    qc = qc_ref[...]
    kc = kc_ref[...].astype(F32)
    vc = vc_ref[...]
    sf_scr[...] = _dot_tn((kc * kdec_f).astype(BF16), vc)
    sb_scr[...] = _dot_tn((kc * kdec_b).astype(BF16), vc)
    if with_ctx_out:
        inner = (_dot_nt(qc, kc_ref[...]) * dmask).astype(BF16)
        oc_ref[...] = _ret_readout(_dot(inner, vc), gc_ref[...])
    else:
        oc_ref[...] = jnp.zeros(oc_ref.shape, oc_ref.dtype)

    def bwd(i, carry):
        ci = n_chunks - 1 - i
        sl = pl.ds(pl.multiple_of(ci * c, c), c)
        q = ql_ref[sl, :].astype(F32)
        k = kl_ref[sl, :].astype(F32)
        ob_scr[sl, :] = _dot((q * qdec_b).astype(BF16), sb_scr[...].astype(BF16))
        sb_scr[...] = sb_scr[...] * cdec_b + _dot_tn((k * kdec_b).astype(BF16), vl_ref[sl, :])
        return carry

    lax.fori_loop(0, n_chunks, bwd, 0)

    def fwd(ci, carry):
        sl = pl.ds(pl.multiple_of(ci * c, c), c)
        qb = ql_ref[sl, :]
        kb = kl_ref[sl, :]
        v = vl_ref[sl, :]
        q = qb.astype(F32)
        k = kb.astype(F32)
        inner = (_dot_nt(qb, kb) * dmask).astype(BF16)
        o = ob_scr[sl, :] + _dot(inner, v) + _dot((q * qdec_f).astype(BF16), sf_scr[...].astype(BF16))
        sf_scr[...] = sf_scr[...] * cdec_f + _dot_tn((k * kdec_f).astype(BF16), v)
        ol_ref[sl, :] = _ret_readout(o, gl_ref[sl, :])
        return carry

    lax.fori_loop(0, n_chunks, fwd, 0)


def _retention(log_g, p_ctx, p_lat, with_ctx_out):
    b, l, _ = p_lat.shape
    lc = p_ctx.shape[1]
    assert lc == RET_CHUNK and l % RET_CHUNK == 0
    dk, dv, hh = RET_QK_DIM, RET_V_DIM, RET_HEADS
    k_off = hh * dk // dk
    v_off = 2 * hh * dk // dv
    g_off = v_off + hh

    def specs(ln):
        return [
            pl.BlockSpec((None, ln, dk), lambda bi, h: (bi, 0, h)),
            pl.BlockSpec((None, ln, dk), lambda bi, h: (bi, 0, k_off + h)),
            pl.BlockSpec((None, ln, dv), lambda bi, h: (bi, 0, v_off + h)),
            pl.BlockSpec((None, ln, dv), lambda bi, h: (bi, 0, g_off + h)),
        ]

    return pl.pallas_call(
        functools.partial(_ret_kernel, n_chunks=l // RET_CHUNK, with_ctx_out=with_ctx_out),
        grid=(b, hh),
        in_specs=[pl.BlockSpec(memory_space=pltpu.SMEM)] + specs(lc) + specs(l),
        out_specs=[
            pl.BlockSpec((None, lc, dv), lambda bi, h: (bi, 0, h)),
            pl.BlockSpec((None, l, dv), lambda bi, h: (bi, 0, h)),
        ],
        out_shape=[
            jax.ShapeDtypeStruct((b, lc, hh * dv), BF16),
            jax.ShapeDtypeStruct((b, l, hh * dv), BF16),
        ],
        scratch_shapes=[pltpu.VMEM((l, dv), F32), pltpu.VMEM((dk, dv), F32), pltpu.VMEM((dk, dv), F32)],
        compiler_params=_cparams("parallel", "parallel"),
        name="retention",
    )(log_g, p_ctx, p_ctx, p_ctx, p_ctx, p_lat, p_lat, p_lat, p_lat)


def _hy_gate_kernel(x0_ref, x1_ref, v_ref, cw0_ref, cw1_ref, cwv_ref, cb0_ref, cb1_ref, cbv_ref, u_ref, g_ref):
    l = x0_ref.shape[0]
    t = lax.broadcasted_iota(jnp.int32, (l, 1), 0)

    def conv3(z_ref, cw_ref, cb_ref):
        z = z_ref[...].astype(F32)
        prev = jnp.where(t == 0, 0.0, pltpu.roll(z, 1, axis=0))
        nxt = jnp.where(t == l - 1, 0.0, pltpu.roll(z, l - 1, axis=0))
        cw = cw_ref[...]
        return prev * cw[0:1, :] + z * cw[1:2, :] + nxt * cw[2:3, :] + cb_ref[...]

    x1 = conv3(x1_ref, cw1_ref, cb1_ref)
    v = conv3(v_ref, cwv_ref, cbv_ref)
    u_ref[...] = (v * x1).astype(u_ref.dtype)
    g_ref[...] = conv3(x0_ref, cw0_ref, cb0_ref).astype(g_ref.dtype)


def _hyena_gate(z, conv_w, conv_b, *, tn=256):
    b, l, d3 = z.shape
    d = d3 // 3
    nb = d // tn
    cb = conv_b.reshape(1, d3)
    zs = [pl.BlockSpec((None, l, tn), (lambda bi, j, o=o: (bi, 0, j + o * nb))) for o in range(3)]
    ws = [pl.BlockSpec((3, tn), (lambda bi, j, o=o: (0, j + o * nb))) for o in range(3)]
    bs = [pl.BlockSpec((1, tn), (lambda bi, j, o=o: (0, j + o * nb))) for o in range(3)]
    return pl.pallas_call(
        _hy_gate_kernel,
        grid=(b, nb),
        in_specs=zs + ws + bs,
        out_specs=[pl.BlockSpec((None, l, tn), lambda bi, j: (bi, 0, j))] * 2,
        out_shape=[jax.ShapeDtypeStruct((b, l, d), BF16)] * 2,
        compiler_params=_cparams("parallel", "parallel"),
        name="hyena_conv3_gate",
    )(z, z, z, conv_w, conv_w, conv_w, cb, cb, cb)


def _hy_filter_kernel(z_ref, w1_ref, b1_ref, f1_ref, w2_ref, b2_ref, f2_ref, w3f_ref, w3b_ref, dec_ref, o_ref):
    l = z_ref.shape[0]
    h = jnp.sin(f1_ref[...] * (_dot3(z_ref[...], w1_ref[...]) + b1_ref[...]))
    h = jnp.sin(f2_ref[...] * (_dot3(h, w2_ref[...]) + b2_ref[...]))
    decay = dec_ref[...] + HYENA_SHIFT
    hf = _dot3(h, w3f_ref[...]) * decay
    hb = _dot3(h, w3b_ref[...]) * decay
    t = lax.broadcasted_iota(jnp.int32, (l, 1), 0)
    hb = jnp.where(t == 0, 0.0, hb)
    norm = jnp.sum(jnp.abs(hf), axis=0, keepdims=True) + jnp.sum(jnp.abs(hb), axis=0, keepdims=True)
    o_ref[0] = (hf / norm).astype(o_ref.dtype)
    o_ref[1] = (hb / norm).astype(o_ref.dtype)


def _hyena_filter(l, d, w1, b1, fr1, w2, b2, fr2, w3, *, tn=256):
    t = jnp.linspace(0.0, 1.0, l, dtype=F32)[:, None]
    bands = (HYENA_EMB - 1) // 2
    w = 2.0 * math.pi * jnp.arange(l, dtype=F32)[:, None] / l
    f = jnp.linspace(1e-4, bands - 1, bands, dtype=F32)[None, :]
    z = jnp.concatenate([t, jnp.cos(f * w), -jnp.sin(f * w)], axis=-1)
    deltas = jnp.abs(jnp.linspace(math.log(HYENA_TARGET) / HYENA_SLOW, math.log(HYENA_TARGET) / HYENA_FAST, d, dtype=F32))
    decay = jnp.exp(-t * deltas)
    fw = w1.shape[1]
    pad = lambda a, r, c: jnp.pad(a.astype(F32), ((0, r - a.shape[0]), (0, c - a.shape[1])))
    z = pad(z, l, LANE)
    w1p = pad(w1, LANE, LANE)
    w2p = pad(w2, LANE, LANE)
    w3p = pad(w3, LANE, 2 * d)
    vec = lambda a: pad(a.reshape(1, fw), 1, LANE)
    nb = d // tn
    full = lambda shape: pl.BlockSpec(shape, lambda j: (0, 0))
    return pl.pallas_call(
        _hy_filter_kernel,
        grid=(nb,),
        in_specs=[full((l, LANE)), full((LANE, LANE)), full((1, LANE)), full((1, LANE)), full((LANE, LANE)),
                  full((1, LANE)), full((1, LANE)),
                  pl.BlockSpec((LANE, tn), lambda j: (0, j)),
                  pl.BlockSpec((LANE, tn), lambda j: (0, j + nb)),
                  pl.BlockSpec((l, tn), lambda j: (0, j))],
        out_specs=pl.BlockSpec((2, l, tn), lambda j: (0, 0, j)),
        out_shape=jax.ShapeDtypeStruct((2, l, d), BF16),
        compiler_params=_cparams("parallel"),
        name="hyena_filter",
    )(z, w1p, vec(b1), vec(fr1), w2p, vec(b2), vec(fr2), w3p, w3p, decay)


def _dft_tables(l):
    n = 2 * l
    k = jnp.arange(l, dtype=jnp.int32)
    nb = 1 << ((l.bit_length() - 1 + 1) // 2)
    na = l // nb
    theta = lambda m: (m % n).astype(F32) * (2.0 * math.pi / n)
    ang_a = theta(k[:, None] * (jnp.arange(na, dtype=jnp.int32) * nb)[None, :])[:, :, None]
    ang_b = theta(k[:, None] * jnp.arange(nb, dtype=jnp.int32)[None, :])[:, None, :]
    cs = (jnp.cos(ang_a) * jnp.cos(ang_b) - jnp.sin(ang_a) * jnp.sin(ang_b)).reshape(l, l)
    sn = (jnp.sin(ang_a) * jnp.cos(ang_b) + jnp.cos(ang_a) * jnp.sin(ang_b)).reshape(l, l)
    alt = jnp.where(k % 2 == 0, 1.0, -1.0).astype(F32)
    f_b = jnp.where(k[:, None] == 0, alt[None, :], -sn)
    fwd = jnp.concatenate([cs, f_b], axis=0)
    wa = jnp.where(k[None, :] == 0, 1.0, 2.0) * cs.T
    wb = jnp.where(k[None, :] == 0, alt[:, None], -2.0 * sn.T)
    inv = jnp.concatenate([wa, wb], axis=1) * (1.0 / n)
    return fwd.astype(BF16), inv.astype(BF16)


def _dft_raw_kernel(fa_ref, fb_ref, u_ref, o_ref):
    u = u_ref[...]
    o_ref[0] = _dot(fa_ref[...], u)
    o_ref[1] = _dot(fb_ref[...], u)


def _dft_raw(fwd, u, *, tm=512, tn=512):
    b, l, d = u.shape
    tm, tn = min(tm, l), min(tn, d)
    nb = l // tm
    return pl.pallas_call(
        _dft_raw_kernel,
        grid=(b, d // tn, nb),
        in_specs=[pl.BlockSpec((tm, l), lambda bi, j, i: (i, 0)),
                  pl.BlockSpec((tm, l), lambda bi, j, i: (i + nb, 0)),
                  pl.BlockSpec((None, l, tn), lambda bi, j, i: (bi, 0, j))],
        out_specs=pl.BlockSpec((None, 2, tm, tn), lambda bi, j, i: (bi, 0, i, j)),
        out_shape=jax.ShapeDtypeStruct((b, 2, l, d), F32),
        compiler_params=_cparams("parallel", "parallel", "parallel"),
        name="hyena_dft_filter",
    )(fwd, fwd, u)


def _dft_mul_kernel(fa_ref, fb_ref, u_ref, hs_ref, o_ref):
    i = pl.program_id(2)
    u = u_ref[...]
    ua = _dot(fa_ref[...], u)
    ub = _dot(fb_ref[...], u)
    first = jnp.logical_and(lax.broadcasted_iota(jnp.int32, (ua.shape[0], 1), 0) == 0, i == 0)
    ha = hs_ref[0, 0] + hs_ref[1, 0]
    hb = jnp.where(first, hs_ref[0, 1] + hs_ref[1, 1], hs_ref[0, 1] - hs_ref[1, 1])
    pa = jnp.where(first, ua * ha, ua * ha - ub * hb)
    pb = jnp.where(first, ub * hb, ua * hb + ub * ha)
    o_ref[0] = pa.astype(o_ref.dtype)
    o_ref[1] = pb.astype(o_ref.dtype)


def _dft_mul(fwd, u, hspec, *, tm=512, tn=512):
    b, l, d = u.shape
    tm, tn = min(tm, l), min(tn, d)
    nb = l // tm
    out = pl.pallas_call(
        _dft_mul_kernel,
        grid=(b, d // tn, nb),
        in_specs=[pl.BlockSpec((tm, l), lambda bi, j, i: (i, 0)),
                  pl.BlockSpec((tm, l), lambda bi, j, i: (i + nb, 0)),
                  pl.BlockSpec((None, l, tn), lambda bi, j, i: (bi, 0, j)),
                  pl.BlockSpec((2, 2, tm, tn), lambda bi, j, i: (0, 0, i, j))],
        out_specs=pl.BlockSpec((None, 2, tm, tn), lambda bi, j, i: (bi, 0, i, j)),
        out_shape=jax.ShapeDtypeStruct((b, 2, l, d), BF16),
        compiler_params=_cparams("parallel", "parallel", "parallel"),
        name="hyena_dft_forward",
    )(fwd, fwd, u, hspec)
    return out.reshape(b, 2 * l, d)


def _idft_kernel(g_ref, p_ref, u_ref, x0_ref, skip_ref, o_ref):
    y = _dot(g_ref[...], p_ref[...]) + u_ref[...].astype(F32) * skip_ref[...]
    o_ref[...] = (y * x0_ref[...].astype(F32)).astype(o_ref.dtype)


def _idft_gate(inv, p, u, x0, skip, *, tm=512, tn=512):
    b, l, d = u.shape
    tm, tn = min(tm, l), min(tn, d)
    return pl.pallas_call(
        _idft_kernel,
        grid=(b, d // tn, l // tm),
        in_specs=[pl.BlockSpec((tm, 2 * l), lambda bi, j, i: (i, 0)),
                  pl.BlockSpec((None, 2 * l, tn), lambda bi, j, i: (bi, 0, j)),
                  pl.BlockSpec((None, tm, tn), lambda bi, j, i: (bi, i, j)),
                  pl.BlockSpec((None, tm, tn), lambda bi, j, i: (bi, i, j)),
                  pl.BlockSpec((1, tn), lambda bi, j, i: (0, j))],
        out_specs=pl.BlockSpec((None, tm, tn), lambda bi, j, i: (bi, i, j)),
        out_shape=jax.ShapeDtypeStruct((b, l, d), BF16),
        compiler_params=_cparams("parallel", "parallel", "parallel"),
        name="hyena_dft_inverse",
    )(inv, p, u, x0, skip.reshape(1, d))


def _hyena_operator(z, conv_w, conv_b, fparams, skip):
    b, l, d3 = z.shape
    d = d3 // 3
    u, x0 = _hyena_gate(z, conv_w, conv_b)
    taps = _hyena_filter(l, d, *fparams)
    fwd, inv = _dft_tables(l)
    hspec = _dft_raw(fwd, taps)
    p = _dft_mul(fwd, u, hspec)
    return _idft_gate(inv, p, u, x0, skip)


def _router_kernel(x_ref, nw_ref, sh_ref, sc_ref, wr_ref, h_ref, lg_ref):
    h = _norm_mod(x_ref[...], nw_ref[...], sh_ref[...], sc_ref[...])
    h_ref[...] = h.astype(BF16)
    lg_ref[...] = _dot3(h, wr_ref[...])


def _norm_mod_router(x, nw, sh, sc, w_router, *, tm=512):
    b, l, d = x.shape
    n_e = w_router.shape[1]
    e = LANE
    w_router = jnp.pad(w_router, ((0, 0), (0, e - n_e)))
    tm = min(tm, l)
    per_sample = sh.shape[0] != 1
    mod_map = (lambda bi, i: (bi, 0, 0)) if per_sample else (lambda bi, i: (0, 0, 0))
    return pl.pallas_call(
        _router_kernel,
        grid=(b, l // tm),
        in_specs=[
            pl.BlockSpec((None, tm, d), lambda bi, i: (bi, i, 0)),
            pl.BlockSpec((1, d), lambda bi, i: (0, 0)),
            pl.BlockSpec((None, 1, d), mod_map),
            pl.BlockSpec((None, 1, d), mod_map),
            pl.BlockSpec((d, e), lambda bi, i: (0, 0)),
        ],
        out_specs=[pl.BlockSpec((None, tm, d), lambda bi, i: (bi, i, 0)),
                   pl.BlockSpec((None, tm, e), lambda bi, i: (bi, i, 0))],
        out_shape=[jax.ShapeDtypeStruct((b, l, d), BF16), jax.ShapeDtypeStruct((b, l, e), F32)],
        compiler_params=_cparams("parallel", "parallel"),
        name="norm_mod_router",
    )(x, nw.reshape(1, d), sh, sc, w_router)


def _route_kernel(lg_ref, tri_ref, pos_ref, prob_ref, *, cap):
    lg = lg_ref[...]
    m = jnp.max(lg, axis=0, keepdims=True)
    ex = jnp.exp(lg - m)
    probs = ex / jnp.sum(ex, axis=0, keepdims=True)
    bits = lax.bitcast_convert_type(probs, jnp.int32)

    def count(mask):
        return jnp.sum(jnp.where(mask, 1.0, 0.0), axis=1, keepdims=True)

    def step(i, thr):
        trial = thr | lax.shift_left(jnp.int32(1), 30 - i)
        return jnp.where(count(bits >= trial) >= cap, trial, thr)

    thr = lax.fori_loop(0, 31, step, jnp.zeros((lg.shape[0], 1), jnp.int32))
    gt = bits > thr
    eq = bits == thr
    need = cap - count(gt).astype(jnp.int32)
    both = jnp.concatenate([jnp.where(gt, 1.0, 0.0), jnp.where(eq, 1.0, 0.0)], axis=0).astype(BF16)
    csum = _dot(both, tri_ref[...])
    e = lg.shape[0]
    rank_gt = csum[:e].astype(jnp.int32)
    rank_eq = csum[e:].astype(jnp.int32)
    sel = jnp.logical_or(gt, jnp.logical_and(eq, rank_eq < need))
    pos = rank_gt + jnp.minimum(rank_eq, need)
    pos_ref[...] = jnp.where(sel, pos, -1)
    prob_ref[...] = probs


def _route(logits_t, cap):
    b, e, n = logits_t.shape
    idx = jnp.arange(n, dtype=jnp.int32)
    tri = (idx[:, None] < idx[None, :]).astype(BF16)
    return pl.pallas_call(
        functools.partial(_route_kernel, cap=cap),
        grid=(b,),
        in_specs=[pl.BlockSpec((None, e, n), lambda bi: (bi, 0, 0)),
                  pl.BlockSpec((n, n), lambda bi: (0, 0))],
        out_specs=[pl.BlockSpec((None, e, n), lambda bi: (bi, 0, 0))] * 2,
        out_shape=[jax.ShapeDtypeStruct((b, e, n), jnp.int32), jax.ShapeDtypeStruct((b, e, n), F32)],
        compiler_params=_cparams("parallel"),
        name="expert_choice_route",
    )(logits_t, tri)


def _gather_kernel(h_ref, pos_ref, prob_ref, xg_ref, gate_ref, *, cap):
    n = h_ref.shape[0]
    slot = lax.broadcasted_iota(jnp.int32, (cap, n), 0)
    match = slot == pos_ref[...]
    onehot = jnp.where(match, 1.0, 0.0).astype(BF16)
    xg_ref[...] = _dot(onehot, h_ref[...]).astype(xg_ref.dtype)
    gate_ref[...] = jnp.sum(jnp.where(match, prob_ref[...], 0.0), axis=1, keepdims=True)


def _gather_append_kernel(h_ref, pos_ref, prob_ref, xin_ref, gin_ref, xg_ref, gate_ref, *, cap, nb):
    bi = pl.program_id(0)

    @pl.when(bi < nb)
    def _():
        _gather_kernel(h_ref, pos_ref, prob_ref, xg_ref, gate_ref, cap=cap)

    @pl.when(bi >= nb)
    def _():
        xg_ref[...] = xin_ref[...]
        gate_ref[...] = gin_ref[...]


def _moe_gather(h, pos, probs, cap, tail=None):
    b, n, d = h.shape
    e = pos.shape[1]
    row = lambda a: a.reshape(b, e, 1, n)
    if tail is None:
        nx, kern, clamp = 0, functools.partial(_gather_kernel, cap=cap), lambda bi: bi
        tail_specs, tail_args = [], []
    else:
        r = tail[0].shape[1]
        assert r % cap == 0
        nx, kern = r // cap, functools.partial(_gather_append_kernel, cap=cap, nb=b)
        clamp = lambda bi: jnp.minimum(bi, b - 1)
        tail_map = lambda bi, ei: (ei, jnp.maximum(bi - b, 0), 0)
        tail_specs = [pl.BlockSpec((None, cap, d), tail_map), pl.BlockSpec((None, cap, 1), tail_map)]
        tail_args = list(tail)
    m_total = (b + nx) * cap
    return pl.pallas_call(
        kern,
        grid=(b + nx, e),
        in_specs=[pl.BlockSpec((None, n, d), lambda bi, ei: (clamp(bi), 0, 0)),
                  pl.BlockSpec((None, None, 1, n), lambda bi, ei: (clamp(bi), ei, 0, 0)),
                  pl.BlockSpec((None, None, 1, n), lambda bi, ei: (clamp(bi), ei, 0, 0))] + tail_specs,
        out_specs=[pl.BlockSpec((None, cap, d), lambda bi, ei: (ei, bi, 0)),
                   pl.BlockSpec((None, cap, 1), lambda bi, ei: (ei, bi, 0))],
        out_shape=[jax.ShapeDtypeStruct((e, m_total, d), BF16), jax.ShapeDtypeStruct((e, m_total, 1), F32)],
        compiler_params=_cparams("parallel", "parallel"),
        name="moe_gather",
    )(h, row(pos), row(probs), *tail_args)


def _ffn_kernel(x_ref, g_ref, wg_ref, wu_ref, wd_ref, o_ref, h_scr, *, n_up, tf):
    s = pl.program_id(2)

    @pl.when(s < n_up)
    def _():
        x = x_ref[...]
        a = _dot(x, wg_ref[...].astype(BF16))
        u = _dot(x, wu_ref[...].astype(BF16))
        h_scr[s] = (_silu(a) * u * g_ref[...]).astype(BF16)

    @pl.when(s >= n_up)
    def _():
        y = _dot(h_scr[0], wd_ref[0:tf, :].astype(BF16))
        for f in range(1, n_up):
            y += _dot(h_scr[f], wd_ref[f * tf:(f + 1) * tf, :].astype(BF16))
        o_ref[...] = y.astype(o_ref.dtype)


def _ffn_acc_kernel(x_ref, g_ref, wg_ref, wu_ref, wd_ref, o_ref, acc_ref):
    f = pl.program_id(2)
    x = x_ref[...]
    a = _dot(x, wg_ref[...].astype(BF16))
    u = _dot(x, wu_ref[...].astype(BF16))
    hmid = (_silu(a) * u * g_ref[...]).astype(BF16)
    y = _dot(hmid, wd_ref[...].astype(BF16))
    acc = jnp.where(f == 0, y, acc_ref[...] + y)
    acc_ref[...] = acc
    o_ref[...] = acc.astype(o_ref.dtype)


def _moe_ffn_acc(xg, gate, w_gate, w_up, w_down, layer, *, tf=256):
    e, m, d = xg.shape
    ff = w_gate.shape[3]
    tm = max(t for t in range(16, 1025, 16) if m % t == 0)
    return pl.pallas_call(
        _ffn_acc_kernel,
        grid=(e, m // tm, ff // tf),
        in_specs=[pl.BlockSpec((None, tm, d), lambda ei, i, f: (ei, i, 0)),
                  pl.BlockSpec((None, tm, 1), lambda ei, i, f: (ei, i, 0)),
                  pl.BlockSpec((None, None, d, tf), lambda ei, i, f: (layer, ei, 0, f)),
                  pl.BlockSpec((None, None, d, tf), lambda ei, i, f: (layer, ei, 0, f)),
                  pl.BlockSpec((None, None, tf, d), lambda ei, i, f: (layer, ei, f, 0))],
        out_specs=pl.BlockSpec((None, tm, d), lambda ei, i, f: (ei, i, 0)),
        out_shape=jax.ShapeDtypeStruct((e, m, d), BF16),
        scratch_shapes=[pltpu.VMEM((tm, d), F32)],
        compiler_params=_cparams("parallel", "parallel", "arbitrary"),
        name="moe_expert_ffn_acc",
    )(xg, gate, w_gate, w_up, w_down)


def _moe_ffn(xg, gate, w_gate, w_up, w_down, layer, *, tf=256, tn=1024):
    e, m, d = xg.shape
    ff = w_gate.shape[3]
    tm = max(t for t in range(16, 1153, 16) if m % t == 0)
    n_up, n_down = ff // tf, d // tn
    up_map = lambda ei, i, s: (layer, ei, 0, jnp.minimum(s, n_up - 1))
    down_blk = lambda s: jnp.maximum(s - n_up, 0)
    return pl.pallas_call(
        functools.partial(_ffn_kernel, n_up=n_up, tf=tf),
        grid=(e, m // tm, n_up + n_down),
        in_specs=[pl.BlockSpec((None, tm, d), lambda ei, i, s: (ei, i, 0)),
                  pl.BlockSpec((None, tm, 1), lambda ei, i, s: (ei, i, 0)),
                  pl.BlockSpec((None, None, d, tf), up_map),
                  pl.BlockSpec((None, None, d, tf), up_map),
                  pl.BlockSpec((None, None, ff, tn), lambda ei, i, s: (layer, ei, 0, down_blk(s)))],
        out_specs=pl.BlockSpec((None, tm, tn), lambda ei, i, s: (ei, i, down_blk(s))),
        out_shape=jax.ShapeDtypeStruct((e, m, d), BF16),
        scratch_shapes=[pltpu.VMEM((n_up, tm, tf), BF16)],
        compiler_params=_cparams("parallel", "parallel", "arbitrary"),
        name="moe_expert_ffn",
    )(xg, gate, w_gate, w_up, w_down)


def _combine_kernel(post_ref, y_ref, r_ref, g_ref, o_ref, pt_scr, *, cap):
    j = pl.program_id(2)
    tm, e = post_ref.shape

    @pl.when(j == 0)
    def _():
        post = post_ref[...]
        if cap % LANE == 0:
            slot = lax.broadcasted_iota(jnp.int32, (tm, cap), 1)
            for ei in range(e):
                pt_scr[:, ei * cap:(ei + 1) * cap] = jnp.where(post[:, ei:ei + 1] == slot, 1.0, 0.0).astype(BF16)
        else:
            slot = lax.broadcasted_iota(jnp.int32, (tm, e * cap), 1)
            hit = jnp.zeros((tm, e * cap), jnp.bool_)
            for ei in range(e):
                tgt = jnp.where(post[:, ei:ei + 1] >= 0, post[:, ei:ei + 1] + ei * cap, -1)
                hit = jnp.logical_or(hit, tgt == slot)
            pt_scr[...] = jnp.where(hit, 1.0, 0.0).astype(BF16)

    y = y_ref[...].reshape(e * cap, y_ref.shape[2])
    o_ref[...] = r_ref[...] + g_ref[...] * _dot(pt_scr[...], y)


def _moe_combine(pos_t, y, row0, resid, gate, cap, *, tm=1024, tn=512):
    b, n, e = pos_t.shape
    d = y.shape[2]
    tm = min(tm, n)
    blk0 = row0 // cap
    per_sample = gate.shape[0] != 1
    g_map = (lambda bi, i, j: (bi, 0, j)) if per_sample else (lambda bi, i, j: (0, 0, j))
    return pl.pallas_call(
        functools.partial(_combine_kernel, cap=cap),
        grid=(b, n // tm, d // tn),
        in_specs=[pl.BlockSpec((None, tm, e), lambda bi, i, j: (bi, i, 0)),
                  pl.BlockSpec((e, cap, tn), lambda bi, i, j: (0, blk0 + bi, j)),
                  pl.BlockSpec((None, tm, tn), lambda bi, i, j: (bi, i, j)),
                  pl.BlockSpec((None, 1, tn), g_map)],
        out_specs=pl.BlockSpec((None, tm, tn), lambda bi, i, j: (bi, i, j)),
        out_shape=jax.ShapeDtypeStruct((b, n, d), F32),
        scratch_shapes=[pltpu.VMEM((tm, e * cap), BF16)],
        compiler_params=_cparams("parallel", "parallel", "arbitrary"),
        name="moe_combine_residual",
    )(pos_t, y, resid, gate)


def _moe_block(streams, nw, w_router, w_gate, w_up, w_down, layer):
    e = w_router.shape[1]
    caps = [CAPACITY_FACTOR * s[0].shape[1] // e for s in streams]
    rows = [s[0].shape[0] * cap for s, cap in zip(streams, caps)]
    row0 = [sum(rows[:t]) for t in range(len(streams))]
    routed, buffers = [None] * len(streams), None
    for t in reversed(range(len(streams))):
        x, sh, sc, _ = streams[t]
        h, logits = _norm_mod_router(x, nw, sh, sc, w_router)
        pos, probs = _route(jnp.swapaxes(logits[..., :e], 1, 2), caps[t])
        buffers = _moe_gather(h, pos, probs, caps[t], buffers)
        routed[t] = jnp.swapaxes(pos, 1, 2)
    ffn = _moe_ffn_acc if layer % 2 == 0 else _moe_ffn
    y = ffn(buffers[0], buffers[1], w_gate, w_up, w_down, layer)
    return [_moe_combine(pos_t, y, r0, x, gate, cap)
            for (x, _, _, gate), pos_t, cap, r0 in zip(streams, routed, caps, row0)]


def _final_norm_kernel(x_ref, w_ref, o_ref):
    x = x_ref[...]
    o_ref[...] = x * lax.rsqrt(jnp.mean(x * x, axis=-1, keepdims=True) + EPS) * w_ref[...]


def _final_norm(x, w, *, tm=512):
    b, l, d = x.shape
    return pl.pallas_call(
        _final_norm_kernel,
        grid=(b, l // tm),
        in_specs=[pl.BlockSpec((None, tm, d), lambda bi, i: (bi, i, 0)), pl.BlockSpec((1, d), lambda bi, i: (0, 0))],
        out_specs=pl.BlockSpec((None, tm, d), lambda bi, i: (bi, i, 0)),
        out_shape=jax.ShapeDtypeStruct(x.shape, F32),
        compiler_params=_cparams("parallel", "parallel"),
        name="final_rmsnorm",
    )(x, w.reshape(1, d))


def _rope_tables(seq_len, dim):
    rows = seq_len // GRID_W
    row_id = jnp.repeat(jnp.arange(rows, dtype=F32), GRID_W)
    col_id = jnp.tile(jnp.arange(GRID_W, dtype=F32), rows)
    nf = dim // 4
    inv = ROPE_THETA ** (-jnp.arange(nf, dtype=F32) / nf)
    ang_r = row_id[:, None] * inv
    ang_c = col_id[:, None] * inv
    cos = jnp.concatenate([jnp.cos(ang_r)] * 2 + [jnp.cos(ang_c)] * 2, axis=-1)
    sin = jnp.concatenate([-jnp.sin(ang_r), jnp.sin(ang_r), -jnp.sin(ang_c), jnp.sin(ang_c)], axis=-1)
    return cos, sin


def _identity_tables(seq_len, dim):
    return jnp.ones((seq_len, dim), F32), jnp.zeros((seq_len, dim), F32)


def _attention_mixer(x, ctx, nw, mods_lat, mods_ctx, w_qkv, q_norm, k_norm, w_o, layer, with_ctx_out):
    hd = ATTN_HEAD_DIM
    nq, nkv = ATTN_HEADS * hd, ATTN_KV_HEADS * hd
    l, lc = x.shape[1], ctx.shape[1]
    tn = 512
    colw = jnp.concatenate([jnp.tile(q_norm.astype(F32) * (hd ** -0.5), ATTN_HEADS),
                            jnp.tile(k_norm.astype(F32), ATTN_KV_HEADS), jnp.ones((nkv,), F32)]).reshape(1, -1)
    rope_blocks = (nq + nkv) // tn
    cos, sin = _rope_tables(l, hd)
    icos, isin = _identity_tables(lc, hd)
    kw = dict(colw=colw, rope_blocks=rope_blocks, head_norm=True, half=hd // 4, tn=tn)
    p_lat = _project(x, nw, mods_lat[0], mods_lat[1], w_qkv, layer, cos=cos, sin=sin, **kw)
    off = 0 if with_ctx_out else nq // tn
    p_ctx = _project(ctx, nw, mods_ctx[0], mods_ctx[1], w_qkv, layer, cos=icos, sin=isin, col_off=off, **kw)
    kc0 = nq - off * tn
    ctx_kv = (p_ctx, kc0, kc0 + nkv)
    exp_dtype = BF16 if layer % 2 else F32
    o_lat = _attention(p_lat, [ctx_kv, (p_lat, nq, nq + nkv)], exp_dtype=exp_dtype)
    x = _out_project(o_lat, w_o, layer, x, mods_lat[2])
    if with_ctx_out:
        ctx = _out_project(_attention(p_ctx, [ctx_kv]), w_o, layer, ctx, mods_ctx[2])
    return x, ctx


def _retention_mixer(x, ctx, nw, mods_lat, mods_ctx, w_in, decay_logit, w_o, layer, with_ctx_out):
    dk, hh = RET_QK_DIM, RET_HEADS
    l, lc = x.shape[1], ctx.shape[1]
    tn = 512
    log_g = jax.nn.log_sigmoid(decay_logit.astype(F32))
    colw = jnp.concatenate([jnp.ones((hh * dk,), F32), jnp.full((hh * dk,), dk ** -0.5, F32),
                            jnp.ones((w_in.shape[2] - 2 * hh * dk,), F32)]).reshape(1, -1)
    rope_blocks = 2 * hh * dk // tn
    cos, sin = _rope_tables(l, dk)
    icos, isin = _identity_tables(lc, dk)
    kw = dict(colw=colw, rope_blocks=rope_blocks, head_norm=False, half=dk // 4, tn=tn)
    p_lat = _project(x, nw, mods_lat[0], mods_lat[1], w_in, layer, cos=cos, sin=sin, **kw)
    p_ctx = _project(ctx, nw, mods_ctx[0], mods_ctx[1], w_in, layer, cos=icos, sin=isin, **kw)
    r_ctx, r_lat = _retention(log_g, p_ctx, p_lat, with_ctx_out)
    x = _out_project(r_lat, w_o, layer, x, mods_lat[2])
    if with_ctx_out:
        ctx = _out_project(r_ctx, w_o, layer, ctx, mods_ctx[2])
    return x, ctx


def _hyena_mixer(x, ctx, nw, mods_lat, mods_ctx, w_in, conv_w, conv_b, fparams, skip, w_out, layer, with_ctx_out):
    z = _project(x, nw, mods_lat[0], mods_lat[1], w_in, layer)
    x = _out_project(_hyena_operator(z, conv_w, conv_b, fparams, skip), w_out, layer, x, mods_lat[2])
    if with_ctx_out:
        zc = _project(ctx, nw, mods_ctx[0], mods_ctx[1], w_in, layer)
        ctx = _out_project(_hyena_operator(zc, conv_w, conv_b, fparams, skip), w_out, layer, ctx, mods_ctx[2])
    return x, ctx


def kernel(x, c, ctx, c_ctx, w_mod, b_mod, norm_w, attn_w_qkv, attn_q_norm, attn_k_norm, attn_w_o, ret_w_in, ret_decay_logit, ret_w_o, hy_w_in, hy_conv_w, hy_conv_b, hy_f_w1, hy_f_b1, hy_f_freq1, hy_f_w2, hy_f_b2, hy_f_freq2, hy_f_w3, hy_skip, hy_w_out, moe_router, moe_w_gate, moe_w_up, moe_w_down, final_norm_w):
    depth = w_mod.shape[0]
    b, _, d = x.shape
    rows = -(-(b + 1) // 8) * 8
    c_all = jnp.concatenate([c, c_ctx[None, :], jnp.zeros((rows - b - 1, d), F32)], axis=0)
    mod = _modulation(c_all, w_mod, b_mod)

    for i in range(depth):
        kind, j = i % 3, i // 3
        with_ctx = i < depth - 1
        m = mod[i].reshape(rows, N_MOD, d)
        mods_lat = [m[:b, t][:, None, :] for t in range(N_MOD)]
        mods_ctx = [m[b:b + 1, t][:, None, :] for t in range(N_MOD)]
        nw1, nw2 = norm_w[i, 0], norm_w[i, 1]
        if kind == 0:
            x, ctx = _attention_mixer(x, ctx, nw1, mods_lat, mods_ctx, attn_w_qkv, attn_q_norm[j], attn_k_norm[j],
                                      attn_w_o, j, with_ctx)
        elif kind == 1:
            x, ctx = _retention_mixer(x, ctx, nw1, mods_lat, mods_ctx, ret_w_in, ret_decay_logit[j], ret_w_o,
                                      j, with_ctx)
        else:
            fparams = (hy_f_w1[j], hy_f_b1[j], hy_f_freq1[j], hy_f_w2[j], hy_f_b2[j], hy_f_freq2[j], hy_f_w3[j])
            x, ctx = _hyena_mixer(x, ctx, nw1, mods_lat, mods_ctx, hy_w_in, hy_conv_w[j], hy_conv_b[j], fparams,
                                  hy_skip[j], hy_w_out, j, with_ctx)
        streams = [(x, mods_lat[3], mods_lat[4], mods_lat[5])]
        if with_ctx:
            streams.append((ctx, mods_ctx[3], mods_ctx[4], mods_ctx[5]))
        outs = _moe_block(streams, nw2, moe_router[i], moe_w_gate, moe_w_up, moe_w_down, i)
        x = outs[0]
        if with_ctx:
            ctx = outs[1]
    return _final_norm(x, final_norm_w)
```

```python
import functools
import math

import jax
import jax.numpy as jnp
from jax import lax
from jax.experimental import pallas as pl
from jax.experimental.pallas import tpu as pltpu

F32 = jnp.float32
BF16 = jnp.bfloat16

EPS = 1e-6
GRID_W = 64
ROPE_THETA = 10000.0
ATTN_HEADS = 16
ATTN_KV_HEADS = 4
ATTN_HEAD_DIM = 128
RET_HEADS = 8
RET_QK_DIM = 256
RET_V_DIM = 512
N_EXPERTS = 16
CAPACITY_FACTOR = 2
N_MOD = 6
HYENA_EMB = 33
HYENA_TARGET = 1e-2
HYENA_FAST = 0.3
HYENA_SLOW = 1.5
HYENA_SHIFT = 0.0

LANE = 128
VMEM_LIMIT = 52 * 1024 * 1024
FFN_VMEM_LIMIT = 58 * 1024 * 1024
FFN_MAX_ROWS = 1152


def _cparams(*sem, vmem=VMEM_LIMIT):
    return pltpu.CompilerParams(dimension_semantics=sem, vmem_limit_bytes=vmem)


def _silu(v):
    return v / (1.0 + jnp.exp(-v))


def _split_bf16(v):
    hi = v.astype(BF16)
    lo = (v - hi.astype(F32)).astype(BF16)
    return hi, lo


def _dot(a, b):
    return jnp.dot(a, b, preferred_element_type=F32)


def _dot3(a, b):
    ah, al = _split_bf16(a)
    bh, bl = _split_bf16(b)
    return _dot(ah, bh) + _dot(al, bh) + _dot(ah, bl)


def _dot_nt(a, b):
    return lax.dot_general(a, b, (((1,), (1,)), ((), ())), preferred_element_type=F32)


def _dot_tn(a, b):
    return lax.dot_general(a, b, (((0,), (0,)), ((), ())), preferred_element_type=F32)


def _mod_kernel(c_ref, w_ref, b_ref, o_ref):
    a = _silu(c_ref[...])
    o_ref[...] = _dot3(a, w_ref[...]) + b_ref[...]


def _modulation(c_all, w_mod, b_mod):
    depth, d, n = w_mod.shape
    rows = c_all.shape[0]
    tn = 512
    return pl.pallas_call(
        _mod_kernel,
        grid=(depth, n // tn),
        in_specs=[
            pl.BlockSpec((rows, d), lambda i, j: (0, 0)),
            pl.BlockSpec((None, d, tn), lambda i, j: (i, 0, j)),
            pl.BlockSpec((None, 1, tn), lambda i, j: (i, 0, j)),
        ],
        out_specs=pl.BlockSpec((None, rows, tn), lambda i, j: (i, 0, j)),
        out_shape=jax.ShapeDtypeStruct((depth, rows, n), F32),
        compiler_params=_cparams("parallel", "parallel"),
        name="modulation",
    )(c_all, w_mod, b_mod.reshape(depth, 1, n))


def _norm_mod(x, nw, sh, sc):
    ms = jnp.mean(x * x, axis=-1, keepdims=True)
    y = x * lax.rsqrt(ms + EPS) * nw
    return y * (1.0 + sc) + sh


def _rope_slices(acc, colw, cos, sin, *, head_norm, half, table_w):
    tn = acc.shape[1]
    lane = lax.broadcasted_iota(jnp.int32, (1, LANE), 1)
    first_half = (lane % (2 * half)) < half
    outs = []
    for s in range(tn // LANE):
        xs = acc[:, s * LANE:(s + 1) * LANE]
        if head_norm:
            xs = xs * lax.rsqrt(jnp.mean(xs * xs, axis=-1, keepdims=True) + EPS)
        xs = xs * colw[:, s * LANE:(s + 1) * LANE]
        t0 = (s * LANE) % table_w
        cs = cos[:, t0:t0 + LANE]
        sn = sin[:, t0:t0 + LANE]
        if 2 * half == LANE:
            partner = pltpu.roll(xs, half, axis=1)
        else:
            partner = jnp.where(first_half, pltpu.roll(xs, LANE - half, axis=1), pltpu.roll(xs, half, axis=1))
        outs.append(xs * cs + partner * sn)
    return jnp.concatenate(outs, axis=1)


def _proj_kernel(x_ref, nw_ref, sh_ref, sc_ref, w_ref, colw_ref, cos_ref, sin_ref, o_ref, h_scr,
                 *, rope_blocks, col_off, head_norm, half, table_w):
    j = pl.program_id(2)

    @pl.when(j == 0)
    def _():
        h_scr[...] = _norm_mod(x_ref[...], nw_ref[...], sh_ref[...], sc_ref[...]).astype(BF16)

    acc = _dot(h_scr[...], w_ref[...].astype(BF16))
    o_ref[...] = acc.astype(o_ref.dtype)
    if rope_blocks:
        @pl.when(j + col_off < rope_blocks)
        def _():
            o_ref[...] = _rope_slices(acc, colw_ref[...], cos_ref[...], sin_ref[...], head_norm=head_norm,
                                      half=half, table_w=table_w).astype(o_ref.dtype)


def _project(x, nw, sh, sc, w, layer, *, colw=None, cos=None, sin=None, rope_blocks=0, col_off=0, head_norm=False,
             half=32, tn=512, tm=1024):
    b, l, d = x.shape
    n = w.shape[2] - col_off * tn
    per_sample = sh.shape[0] != 1
    if colw is None:
        colw = jnp.ones((1, w.shape[2]), F32)
        cos = jnp.ones((l, LANE), F32)
        sin = jnp.zeros((l, LANE), F32)
    if not per_sample and b > 1:
        out = _project(x.reshape(1, b * l, d), nw, sh, sc, w, layer, colw=colw, cos=jnp.tile(cos, (b, 1)),
                       sin=jnp.tile(sin, (b, 1)), rope_blocks=rope_blocks, col_off=col_off, head_norm=head_norm,
                       half=half, tn=tn, tm=tm)
        return out.reshape(b, l, n)
    tm = min(tm, l)
    mod_map = (lambda bi, i, j: (bi, 0, 0)) if per_sample else (lambda bi, i, j: (0, 0, 0))
    table_w = cos.shape[1]
    kern = functools.partial(_proj_kernel, rope_blocks=rope_blocks, col_off=col_off, head_norm=head_norm,
                             half=half, table_w=table_w)
    return pl.pallas_call(
        kern,
        grid=(b, l // tm, n // tn),
        in_specs=[
            pl.BlockSpec((None, tm, d), lambda bi, i, j: (bi, i, 0)),
            pl.BlockSpec((1, d), lambda bi, i, j: (0, 0)),
            pl.BlockSpec((None, 1, d), mod_map),
            pl.BlockSpec((None, 1, d), mod_map),
            pl.BlockSpec((None, d, tn), lambda bi, i, j: (layer, 0, j + col_off)),
            pl.BlockSpec((1, tn), lambda bi, i, j: (0, j + col_off)),
            pl.BlockSpec((tm, table_w), lambda bi, i, j: (i, 0)),
            pl.BlockSpec((tm, table_w), lambda bi, i, j: (i, 0)),
        ],
        out_specs=pl.BlockSpec((None, tm, tn), lambda bi, i, j: (bi, i, j)),
        out_shape=jax.ShapeDtypeStruct((b, l, n), BF16),
        scratch_shapes=[pltpu.VMEM((tm, d), BF16)],
        compiler_params=_cparams("parallel", "parallel", "arbitrary"),
        name="norm_mod_project",
    )(x, nw.reshape(1, d), sh, sc, w, colw, cos, sin)


def _out_proj_kernel(a_ref, w_ref, r_ref, g_ref, o_ref):
    acc = _dot(a_ref[...], w_ref[...].astype(BF16))
    o_ref[...] = r_ref[...] + g_ref[...] * acc


def _out_project(a, w, layer, resid, gate, *, tm=None, tn=512):
    b, l, k = a.shape
    n = w.shape[2]
    if tm is None:
        tm = (2048 * 2048) // k
    per_sample = gate.shape[0] != 1
    if not per_sample and b > 1:
        out = _out_project(a.reshape(1, b * l, k), w, layer, resid.reshape(1, b * l, n), gate, tm=tm, tn=tn)
        return out.reshape(b, l, n)
    tm = min(tm, l)
    g_map = (lambda bi, i, j: (bi, 0, j)) if per_sample else (lambda bi, i, j: (0, 0, j))
    return pl.pallas_call(
        _out_proj_kernel,
        grid=(b, l // tm, n // tn),
        in_specs=[
            pl.BlockSpec((None, tm, k), lambda bi, i, j: (bi, i, 0)),
            pl.BlockSpec((None, k, tn), lambda bi, i, j: (layer, 0, j)),
            pl.BlockSpec((None, tm, tn), lambda bi, i, j: (bi, i, j)),
            pl.BlockSpec((None, 1, tn), g_map),
        ],
        out_specs=pl.BlockSpec((None, tm, tn), lambda bi, i, j: (bi, i, j)),
        out_shape=jax.ShapeDtypeStruct((b, l, n), F32),
        compiler_params=_cparams("parallel", "parallel", "parallel"),
        name="out_project_residual",
    )(a, w, resid, gate)


def _attn_kernel(q_ref, *refs, groups, n_kv):
    hd = ATTN_HEAD_DIM
    o_ref = refs[2 * n_kv]
    ks = [refs[2 * t][...] for t in range(n_kv)]
    vs = [jnp.concatenate([refs[2 * t + 1][...], jnp.ones((k.shape[0], hd), BF16)], axis=1) for t, k in enumerate(ks)]
    for g in range(groups):
        q = q_ref[:, g * hd:(g + 1) * hd]
        ss = [_dot_nt(q, k) for k in ks]
        m = functools.reduce(jnp.maximum, [jnp.max(s, axis=-1, keepdims=True) for s in ss])
        ps = [jnp.exp(s - m).astype(BF16) for s in ss]
        ov = sum(_dot(p, v) for p, v in zip(ps, vs))
        o_ref[:, g * hd:(g + 1) * hd] = (ov[:, :hd] / ov[:, hd:hd + 1]).astype(o_ref.dtype)


def _attention(pq, kv_sources, *, tq=512):
    b, lq, _ = pq.shape
    hd = ATTN_HEAD_DIM
    groups = ATTN_HEADS // ATTN_KV_HEADS
    gw = groups * hd
    tq = min(tq, lq)
    in_specs = [pl.BlockSpec((None, tq, gw), lambda bi, h, i: (bi, i, h))]
    args = [pq]
    for p, k0, v0 in kv_sources:
        lk = p.shape[1]
        in_specs.append(pl.BlockSpec((None, lk, hd), lambda bi, h, i, o=k0 // hd: (bi, 0, o + h)))
        in_specs.append(pl.BlockSpec((None, lk, hd), lambda bi, h, i, o=v0 // hd: (bi, 0, o + h)))
        args += [p, p]
    return pl.pallas_call(
        functools.partial(_attn_kernel, groups=groups, n_kv=len(kv_sources)),
        grid=(b, ATTN_KV_HEADS, lq // tq),
        in_specs=in_specs,
        out_specs=pl.BlockSpec((None, tq, gw), lambda bi, h, i: (bi, i, h)),
        out_shape=jax.ShapeDtypeStruct((b, lq, ATTN_HEADS * hd), BF16),
        compiler_params=_cparams("parallel", "parallel", "parallel"),
        name="gqa_attention",
    )(*args)


RET_CHUNK = 256


def _ret_readout(o, g):
    of = o * lax.rsqrt(jnp.mean(o * o, axis=-1, keepdims=True) + EPS)
    return (_silu(g.astype(F32)) * of).astype(BF16)


def _ret_kernel(lg_ref, qc_ref, kc_ref, vc_ref, gc_ref, ql_ref, kl_ref, vl_ref, gl_ref, oc_ref, ol_ref,
                ob_scr, sf_scr, sb_scr, *, n_chunks, with_ctx_out):
    c = RET_CHUNK
    h = pl.program_id(1)
    lgf = lg_ref[0, h]
    lgb = lg_ref[1, h]
    row = lax.broadcasted_iota(jnp.int32, (c, 1), 0).astype(F32)
    col = lax.broadcasted_iota(jnp.int32, (1, c), 1).astype(F32)
    diff = row - col
    dmask = jnp.exp(jnp.where(diff >= 0, diff * lgf, -diff * lgb))
    qdec_f = jnp.exp((row + 1.0) * lgf)
    kdec_f = jnp.exp((c - 1.0 - row) * lgf)
    qdec_b = jnp.exp((c - row) * lgb)
    kdec_b = jnp.exp(row * lgb)
    one = jnp.ones((1, 1), F32)
    cdec_f = jnp.exp(one * (c * lgf))
    cdec_b = jnp.exp(one * (c * lgb))

    qc = qc_ref[...]
    kc = kc_ref[...].astype(F32)
    vc = vc_ref[...]
    sf_scr[...] = _dot_tn((kc * kdec_f).astype(BF16), vc)
    sb_scr[...] = _dot_tn((kc * kdec_b).astype(BF16), vc)
    if with_ctx_out:
        inner = (_dot_nt(qc, kc_ref[...]) * dmask).astype(BF16)
        oc_ref[...] = _ret_readout(_dot(inner, vc), gc_ref[...])
    else:
        oc_ref[...] = jnp.zeros(oc_ref.shape, oc_ref.dtype)

    def bwd(i, carry):
        ci = n_chunks - 1 - i
        sl = pl.ds(pl.multiple_of(ci * c, c), c)
        q = ql_ref[sl, :].astype(F32)
        k = kl_ref[sl, :].astype(F32)
        ob_scr[sl, :] = _dot((q * qdec_b).astype(BF16), sb_scr[...].astype(BF16))
        sb_scr[...] = sb_scr[...] * cdec_b + _dot_tn((k * kdec_b).astype(BF16), vl_ref[sl, :])
        return carry

    lax.fori_loop(0, n_chunks, bwd, 0)

    def fwd(ci, carry):
        sl = pl.ds(pl.multiple_of(ci * c, c), c)
        qb = ql_ref[sl, :]
        kb = kl_ref[sl, :]
        v = vl_ref[sl, :]
        q = qb.astype(F32)
        k = kb.astype(F32)
        inner = (_dot_nt(qb, kb) * dmask).astype(BF16)
        o = ob_scr[sl, :] + _dot(inner, v) + _dot((q * qdec_f).astype(BF16), sf_scr[...].astype(BF16))
        sf_scr[...] = sf_scr[...] * cdec_f + _dot_tn((k * kdec_f).astype(BF16), v)
        ol_ref[sl, :] = _ret_readout(o, gl_ref[sl, :])
        return carry

    lax.fori_loop(0, n_chunks, fwd, 0)


def _retention(log_g, p_ctx, p_lat, with_ctx_out):
    b, l, _ = p_lat.shape
    lc = p_ctx.shape[1]
    assert lc == RET_CHUNK and l % RET_CHUNK == 0
    dk, dv, hh = RET_QK_DIM, RET_V_DIM, RET_HEADS
    k_off = hh * dk // dk
    v_off = 2 * hh * dk // dv
    g_off = v_off + hh

    def specs(ln):
        return [
            pl.BlockSpec((None, ln, dk), lambda bi, h: (bi, 0, h)),
            pl.BlockSpec((None, ln, dk), lambda bi, h: (bi, 0, k_off + h)),
            pl.BlockSpec((None, ln, dv), lambda bi, h: (bi, 0, v_off + h)),
            pl.BlockSpec((None, ln, dv), lambda bi, h: (bi, 0, g_off + h)),
        ]

    return pl.pallas_call(
        functools.partial(_ret_kernel, n_chunks=l // RET_CHUNK, with_ctx_out=with_ctx_out),
        grid=(b, hh),
        in_specs=[pl.BlockSpec(memory_space=pltpu.SMEM)] + specs(lc) + specs(l),
        out_specs=[
            pl.BlockSpec((None, lc, dv), lambda bi, h: (bi, 0, h)),
            pl.BlockSpec((None, l, dv), lambda bi, h: (bi, 0, h)),
        ],
        out_shape=[
            jax.ShapeDtypeStruct((b, lc, hh * dv), BF16),
            jax.ShapeDtypeStruct((b, l, hh * dv), BF16),
        ],
        scratch_shapes=[pltpu.VMEM((l, dv), F32), pltpu.VMEM((dk, dv), F32), pltpu.VMEM((dk, dv), F32)],
        compiler_params=_cparams("parallel", "parallel"),
        name="retention",
    )(log_g, p_ctx, p_ctx, p_ctx, p_ctx, p_lat, p_lat, p_lat, p_lat)


def _hy_gate_kernel(x0_ref, x1_ref, v_ref, cw0_ref, cw1_ref, cwv_ref, cb0_ref, cb1_ref, cbv_ref, u_ref, g_ref):
    l = x0_ref.shape[0]
    t = lax.broadcasted_iota(jnp.int32, (l, 1), 0)

    def conv3(z_ref, cw_ref, cb_ref):
        z = z_ref[...].astype(F32)
        prev = jnp.where(t == 0, 0.0, pltpu.roll(z, 1, axis=0))
        nxt = jnp.where(t == l - 1, 0.0, pltpu.roll(z, l - 1, axis=0))
        cw = cw_ref[...]
        return prev * cw[0:1, :] + z * cw[1:2, :] + nxt * cw[2:3, :] + cb_ref[...]

    x1 = conv3(x1_ref, cw1_ref, cb1_ref)
    v = conv3(v_ref, cwv_ref, cbv_ref)
    u_ref[...] = (v * x1).astype(u_ref.dtype)
    g_ref[...] = conv3(x0_ref, cw0_ref, cb0_ref).astype(g_ref.dtype)


def _hyena_gate(z, conv_w, conv_b, *, tn=256):
    b, l, d3 = z.shape
    d = d3 // 3
    nb = d // tn
    cb = conv_b.reshape(1, d3)
    zs = [pl.BlockSpec((None, l, tn), (lambda bi, j, o=o: (bi, 0, j + o * nb))) for o in range(3)]
    ws = [pl.BlockSpec((3, tn), (lambda bi, j, o=o: (0, j + o * nb))) for o in range(3)]
    bs = [pl.BlockSpec((1, tn), (lambda bi, j, o=o: (0, j + o * nb))) for o in range(3)]
    return pl.pallas_call(
        _hy_gate_kernel,
        grid=(b, nb),
        in_specs=zs + ws + bs,
        out_specs=[pl.BlockSpec((None, l, tn), lambda bi, j: (bi, 0, j))] * 2,
        out_shape=[jax.ShapeDtypeStruct((b, l, d), BF16)] * 2,
        compiler_params=_cparams("parallel", "parallel"),
        name="hyena_conv3_gate",
    )(z, z, z, conv_w, conv_w, conv_w, cb, cb, cb)


def _hy_filter_kernel(z_ref, w1_ref, b1_ref, f1_ref, w2_ref, b2_ref, f2_ref, w3f_ref, w3b_ref, dec_ref, o_ref):
    l = z_ref.shape[0]
    h = jnp.sin(f1_ref[...] * (_dot3(z_ref[...], w1_ref[...]) + b1_ref[...]))
    h = jnp.sin(f2_ref[...] * (_dot3(h, w2_ref[...]) + b2_ref[...]))
    decay = dec_ref[...] + HYENA_SHIFT
    hf = _dot3(h, w3f_ref[...]) * decay
    hb = _dot3(h, w3b_ref[...]) * decay
    t = lax.broadcasted_iota(jnp.int32, (l, 1), 0)
    hb = jnp.where(t == 0, 0.0, hb)
    norm = jnp.sum(jnp.abs(hf), axis=0, keepdims=True) + jnp.sum(jnp.abs(hb), axis=0, keepdims=True)
    o_ref[0] = (hf / norm).astype(o_ref.dtype)
    o_ref[1] = (hb / norm).astype(o_ref.dtype)


def _hyena_filter(l, d, w1, b1, fr1, w2, b2, fr2, w3, *, tn=256):
    t = jnp.linspace(0.0, 1.0, l, dtype=F32)[:, None]
    bands = (HYENA_EMB - 1) // 2
    w = 2.0 * math.pi * jnp.arange(l, dtype=F32)[:, None] / l
    f = jnp.linspace(1e-4, bands - 1, bands, dtype=F32)[None, :]
    z = jnp.concatenate([t, jnp.cos(f * w), -jnp.sin(f * w)], axis=-1)
    deltas = jnp.abs(jnp.linspace(math.log(HYENA_TARGET) / HYENA_SLOW, math.log(HYENA_TARGET) / HYENA_FAST, d, dtype=F32))
    decay = jnp.exp(-t * deltas)
    fw = w1.shape[1]
    pad = lambda a, r, c: jnp.pad(a.astype(F32), ((0, r - a.shape[0]), (0, c - a.shape[1])))
    z = pad(z, l, LANE)
    w1p = pad(w1, LANE, LANE)
    w2p = pad(w2, LANE, LANE)
    w3p = pad(w3, LANE, 2 * d)
    vec = lambda a: pad(a.reshape(1, fw), 1, LANE)
    nb = d // tn
    full = lambda shape: pl.BlockSpec(shape, lambda j: (0, 0))
    return pl.pallas_call(
        _hy_filter_kernel,
        grid=(nb,),
        in_specs=[full((l, LANE)), full((LANE, LANE)), full((1, LANE)), full((1, LANE)), full((LANE, LANE)),
                  full((1, LANE)), full((1, LANE)),
                  pl.BlockSpec((LANE, tn), lambda j: (0, j)),
                  pl.BlockSpec((LANE, tn), lambda j: (0, j + nb)),
                  pl.BlockSpec((l, tn), lambda j: (0, j))],
        out_specs=pl.BlockSpec((2, l, tn), lambda j: (0, 0, j)),
        out_shape=jax.ShapeDtypeStruct((2, l, d), BF16),
        compiler_params=_cparams("parallel"),
        name="hyena_filter",
    )(z, w1p, vec(b1), vec(fr1), w2p, vec(b2), vec(fr2), w3p, w3p, decay)


def _dft_tables(l):
    n = 2 * l
    k = jnp.arange(l, dtype=jnp.int32)
    nb = 1 << ((l.bit_length() - 1 + 1) // 2)
    na = l // nb
    theta = lambda m: (m % n).astype(F32) * (2.0 * math.pi / n)
    ang_a = theta(k[:, None] * (jnp.arange(na, dtype=jnp.int32) * nb)[None, :])[:, :, None]
    ang_b = theta(k[:, None] * jnp.arange(nb, dtype=jnp.int32)[None, :])[:, None, :]
    cs = (jnp.cos(ang_a) * jnp.cos(ang_b) - jnp.sin(ang_a) * jnp.sin(ang_b)).reshape(l, l)
    sn = (jnp.sin(ang_a) * jnp.cos(ang_b) + jnp.cos(ang_a) * jnp.sin(ang_b)).reshape(l, l)
    alt = jnp.where(k % 2 == 0, 1.0, -1.0).astype(F32)
    f_b = jnp.where(k[:, None] == 0, alt[None, :], -sn)
    fwd = jnp.concatenate([cs, f_b], axis=0)
    wa = jnp.where(k[None, :] == 0, 1.0, 2.0) * cs.T
    wb = jnp.where(k[None, :] == 0, alt[:, None], -2.0 * sn.T)
    inv = jnp.concatenate([wa, wb], axis=1) * (1.0 / n)
    return fwd.astype(BF16), inv.astype(BF16)


def _dft_raw_kernel(fa_ref, fb_ref, u_ref, o_ref):
    u = u_ref[...]
    o_ref[0] = _dot(fa_ref[...], u)
    o_ref[1] = _dot(fb_ref[...], u)


def _dft_raw(fwd, u, *, tm=512, tn=512):
    b, l, d = u.shape
    tm, tn = min(tm, l), min(tn, d)
    nb = l // tm
    return pl.pallas_call(
        _dft_raw_kernel,
        grid=(b, d // tn, nb),
        in_specs=[pl.BlockSpec((tm, l), lambda bi, j, i: (i, 0)),
                  pl.BlockSpec((tm, l), lambda bi, j, i: (i + nb, 0)),
                  pl.BlockSpec((None, l, tn), lambda bi, j, i: (bi, 0, j))],
        out_specs=pl.BlockSpec((None, 2, tm, tn), lambda bi, j, i: (bi, 0, i, j)),
        out_shape=jax.ShapeDtypeStruct((b, 2, l, d), F32),
        compiler_params=_cparams("parallel", "parallel", "parallel"),
        name="hyena_dft_filter",
    )(fwd, fwd, u)


def _dft_mul_kernel(fa_ref, fb_ref, u_ref, hs_ref, o_ref):
    i = pl.program_id(2)
    u = u_ref[...]
    ua = _dot(fa_ref[...], u)
    ub = _dot(fb_ref[...], u)
    first = jnp.logical_and(lax.broadcasted_iota(jnp.int32, (ua.shape[0], 1), 0) == 0, i == 0)
    ha = hs_ref[0, 0] + hs_ref[1, 0]
    hb = jnp.where(first, hs_ref[0, 1] + hs_ref[1, 1], hs_ref[0, 1] - hs_ref[1, 1])
    pa = jnp.where(first, ua * ha, ua * ha - ub * hb)
    pb = jnp.where(first, ub * hb, ua * hb + ub * ha)
    o_ref[0] = pa.astype(o_ref.dtype)
    o_ref[1] = pb.astype(o_ref.dtype)


def _dft_mul(fwd, u, hspec, *, tm=512, tn=512):
    b, l, d = u.shape
    tm, tn = min(tm, l), min(tn, d)
    nb = l // tm
    out = pl.pallas_call(
        _dft_mul_kernel,
        grid=(b, d // tn, nb),
        in_specs=[pl.BlockSpec((tm, l), lambda bi, j, i: (i, 0)),
                  pl.BlockSpec((tm, l), lambda bi, j, i: (i + nb, 0)),
                  pl.BlockSpec((None, l, tn), lambda bi, j, i: (bi, 0, j)),
                  pl.BlockSpec((2, 2, tm, tn), lambda bi, j, i: (0, 0, i, j))],
        out_specs=pl.BlockSpec((None, 2, tm, tn), lambda bi, j, i: (bi, 0, i, j)),
        out_shape=jax.ShapeDtypeStruct((b, 2, l, d), BF16),
        compiler_params=_cparams("parallel", "parallel", "parallel"),
        name="hyena_dft_forward",
    )(fwd, fwd, u, hspec)
    return out.reshape(b, 2 * l, d)


def _idft_kernel(g_ref, p_ref, u_ref, x0_ref, skip_ref, o_ref):
    y = _dot(g_ref[...], p_ref[...]) + u_ref[...].astype(F32) * skip_ref[...]
    o_ref[...] = (y * x0_ref[...].astype(F32)).astype(o_ref.dtype)


def _idft_gate(inv, p, u, x0, skip, *, tm=512, tn=512):
    b, l, d = u.shape
    tm, tn = min(tm, l), min(tn, d)
    return pl.pallas_call(
        _idft_kernel,
        grid=(b, d // tn, l // tm),
        in_specs=[pl.BlockSpec((tm, 2 * l), lambda bi, j, i: (i, 0)),
                  pl.BlockSpec((None, 2 * l, tn), lambda bi, j, i: (bi, 0, j)),
                  pl.BlockSpec((None, tm, tn), lambda bi, j, i: (bi, i, j)),
                  pl.BlockSpec((None, tm, tn), lambda bi, j, i: (bi, i, j)),
                  pl.BlockSpec((1, tn), lambda bi, j, i: (0, j))],
        out_specs=pl.BlockSpec((None, tm, tn), lambda bi, j, i: (bi, i, j)),
        out_shape=jax.ShapeDtypeStruct((b, l, d), BF16),
        compiler_params=_cparams("parallel", "parallel", "parallel"),
        name="hyena_dft_inverse",
    )(inv, p, u, x0, skip.reshape(1, d))


def _hyena_operator(z, conv_w, conv_b, fparams, skip):
    b, l, d3 = z.shape
    d = d3 // 3
    u, x0 = _hyena_gate(z, conv_w, conv_b)
    taps = _hyena_filter(l, d, *fparams)
    fwd, inv = _dft_tables(l)
    hspec = _dft_raw(fwd, taps)
    p = _dft_mul(fwd, u, hspec)
    return _idft_gate(inv, p, u, x0, skip)


def _router_kernel(x_ref, nw_ref, sh_ref, sc_ref, wr_ref, h_ref, lg_ref):
    h = _norm_mod(x_ref[...], nw_ref[...], sh_ref[...], sc_ref[...])
    h_ref[...] = h.astype(BF16)
    lg_ref[...] = _dot3(h, wr_ref[...])


def _norm_mod_router(x, nw, sh, sc, w_router, *, tm=512):
    b, l, d = x.shape
    n_e = w_router.shape[1]
    e = LANE
    w_router = jnp.pad(w_router, ((0, 0), (0, e - n_e)))
    tm = min(tm, l)
    per_sample = sh.shape[0] != 1
    mod_map = (lambda bi, i: (bi, 0, 0)) if per_sample else (lambda bi, i: (0, 0, 0))
    return pl.pallas_call(
        _router_kernel,
        grid=(b, l // tm),
        in_specs=[
            pl.BlockSpec((None, tm, d), lambda bi, i: (bi, i, 0)),
            pl.BlockSpec((1, d), lambda bi, i: (0, 0)),
            pl.BlockSpec((None, 1, d), mod_map),
            pl.BlockSpec((None, 1, d), mod_map),
            pl.BlockSpec((d, e), lambda bi, i: (0, 0)),
        ],
        out_specs=[pl.BlockSpec((None, tm, d), lambda bi, i: (bi, i, 0)),
                   pl.BlockSpec((None, tm, e), lambda bi, i: (bi, i, 0))],
        out_shape=[jax.ShapeDtypeStruct((b, l, d), BF16), jax.ShapeDtypeStruct((b, l, e), F32)],
        compiler_params=_cparams("parallel", "parallel"),
        name="norm_mod_router",
    )(x, nw.reshape(1, d), sh, sc, w_router)


def _route_kernel(lg_ref, tri_ref, pos_ref, prob_ref, *, cap):
    lg = lg_ref[...]
    m = jnp.max(lg, axis=0, keepdims=True)
    ex = jnp.exp(lg - m)
    probs = ex / jnp.sum(ex, axis=0, keepdims=True)
    bits = lax.bitcast_convert_type(probs, jnp.int32)

    def count(mask):
        return jnp.sum(jnp.where(mask, 1.0, 0.0), axis=1, keepdims=True)

    def step(i, thr):
        trial = thr | lax.shift_left(jnp.int32(1), 30 - i)
        return jnp.where(count(bits >= trial) >= cap, trial, thr)

    thr = lax.fori_loop(0, 31, step, jnp.zeros((lg.shape[0], 1), jnp.int32))
    gt = bits > thr
    eq = bits == thr
    need = cap - count(gt).astype(jnp.int32)
    both = jnp.concatenate([jnp.where(gt, 1.0, 0.0), jnp.where(eq, 1.0, 0.0)], axis=0).astype(BF16)
    csum = _dot(both, tri_ref[...])
    e = lg.shape[0]
    rank_gt = csum[:e].astype(jnp.int32)
    rank_eq = csum[e:].astype(jnp.int32)
    sel = jnp.logical_or(gt, jnp.logical_and(eq, rank_eq < need))
    pos = rank_gt + jnp.minimum(rank_eq, need)
    pos_ref[...] = jnp.where(sel, pos, -1)
    prob_ref[...] = probs


def _route(logits_t, cap):
    b, e, n = logits_t.shape
    idx = jnp.arange(n, dtype=jnp.int32)
    tri = (idx[:, None] < idx[None, :]).astype(BF16)
    return pl.pallas_call(
        functools.partial(_route_kernel, cap=cap),
        grid=(b,),
        in_specs=[pl.BlockSpec((None, e, n), lambda bi: (bi, 0, 0)),
                  pl.BlockSpec((n, n), lambda bi: (0, 0))],
        out_specs=[pl.BlockSpec((None, e, n), lambda bi: (bi, 0, 0))] * 2,
        out_shape=[jax.ShapeDtypeStruct((b, e, n), jnp.int32), jax.ShapeDtypeStruct((b, e, n), F32)],
        compiler_params=_cparams("parallel"),
        name="expert_choice_route",
    )(logits_t, tri)


def _gather_kernel(h_ref, pos_ref, prob_ref, xg_ref, gate_ref, *, cap):
    n = h_ref.shape[0]
    slot = lax.broadcasted_iota(jnp.int32, (cap, n), 0)
    match = slot == pos_ref[...]
    onehot = jnp.where(match, 1.0, 0.0).astype(BF16)
    xg_ref[...] = _dot(onehot, h_ref[...]).astype(xg_ref.dtype)
    gate_ref[...] = jnp.sum(jnp.where(match, prob_ref[...], 0.0), axis=1, keepdims=True)


def _gather_append_kernel(h_ref, pos_ref, prob_ref, xin_ref, gin_ref, xg_ref, gate_ref, *, cap, nb):
    bi = pl.program_id(0)

    @pl.when(bi < nb)
    def _():
        _gather_kernel(h_ref, pos_ref, prob_ref, xg_ref, gate_ref, cap=cap)

    @pl.when(bi >= nb)
    def _():
        xg_ref[...] = xin_ref[...]
        gate_ref[...] = gin_ref[...]


def _moe_gather(h, pos, probs, cap, tail=None):
    b, n, d = h.shape
    e = pos.shape[1]
    row = lambda a: a.reshape(b, e, 1, n)
    if tail is None:
        nx, kern, clamp = 0, functools.partial(_gather_kernel, cap=cap), lambda bi: bi
        tail_specs, tail_args = [], []
    else:
        r = tail[0].shape[1]
        assert r % cap == 0
        nx, kern = r // cap, functools.partial(_gather_append_kernel, cap=cap, nb=b)
        clamp = lambda bi: jnp.minimum(bi, b - 1)
        tail_map = lambda bi, ei: (ei, jnp.maximum(bi - b, 0), 0)
        tail_specs = [pl.BlockSpec((None, cap, d), tail_map), pl.BlockSpec((None, cap, 1), tail_map)]
        tail_args = list(tail)
    m_total = (b + nx) * cap
    return pl.pallas_call(
        kern,
        grid=(b + nx, e),
        in_specs=[pl.BlockSpec((None, n, d), lambda bi, ei: (clamp(bi), 0, 0)),
                  pl.BlockSpec((None, None, 1, n), lambda bi, ei: (clamp(bi), ei, 0, 0)),
                  pl.BlockSpec((None, None, 1, n), lambda bi, ei: (clamp(bi), ei, 0, 0))] + tail_specs,
        out_specs=[pl.BlockSpec((None, cap, d), lambda bi, ei: (ei, bi, 0)),
                   pl.BlockSpec((None, cap, 1), lambda bi, ei: (ei, bi, 0))],
        out_shape=[jax.ShapeDtypeStruct((e, m_total, d), BF16), jax.ShapeDtypeStruct((e, m_total, 1), F32)],
        compiler_params=_cparams("parallel", "parallel"),
        name="moe_gather",
    )(h, row(pos), row(probs), *tail_args)


def _ffn_kernel(x_ref, g_ref, wg_ref, wu_ref, wd_ref, o_ref, acc_ref):
    f = pl.program_id(2)
    x = x_ref[...]
    a = _dot(x, wg_ref[...].astype(BF16))
    u = _dot(x, wu_ref[...].astype(BF16))
    hmid = (_silu(a) * u * g_ref[...]).astype(BF16)
    y = _dot(hmid, wd_ref[...].astype(BF16))
    acc = jnp.where(f == 0, y, acc_ref[...] + y)
    acc_ref[...] = acc
    o_ref[...] = acc.astype(o_ref.dtype)


def _moe_ffn(xg, gate, w_gate, w_up, w_down, layer, *, tf=256):
    e, m, d = xg.shape
    ff = w_gate.shape[3]
    tm = max(t for t in range(16, FFN_MAX_ROWS + 1, 16) if m % t == 0)
    return pl.pallas_call(
        _ffn_kernel,
        grid=(e, m // tm, ff // tf),
        in_specs=[pl.BlockSpec((None, tm, d), lambda ei, i, f: (ei, i, 0)),
                  pl.BlockSpec((None, tm, 1), lambda ei, i, f: (ei, i, 0)),
                  pl.BlockSpec((None, None, d, tf), lambda ei, i, f: (layer, ei, 0, f)),
                  pl.BlockSpec((None, None, d, tf), lambda ei, i, f: (layer, ei, 0, f)),
                  pl.BlockSpec((None, None, tf, d), lambda ei, i, f: (layer, ei, f, 0))],
        out_specs=pl.BlockSpec((None, tm, d), lambda ei, i, f: (ei, i, 0)),
        out_shape=jax.ShapeDtypeStruct((e, m, d), BF16),
        scratch_shapes=[pltpu.VMEM((tm, d), F32)],
        compiler_params=_cparams("parallel", "parallel", "arbitrary", vmem=FFN_VMEM_LIMIT),
        name="moe_expert_ffn",
    )(xg, gate, w_gate, w_up, w_down)


def _combine_kernel(post_ref, y_ref, r_ref, g_ref, o_ref, pt_scr, *, cap):
    j = pl.program_id(2)
    tm, e = post_ref.shape

    @pl.when(j == 0)
    def _():
        post = post_ref[...]
        if cap % LANE == 0:
            slot = lax.broadcasted_iota(jnp.int32, (tm, cap), 1)
            for ei in range(e):
                pt_scr[:, ei * cap:(ei + 1) * cap] = jnp.where(post[:, ei:ei + 1] == slot, 1.0, 0.0).astype(BF16)
        else:
            slot = lax.broadcasted_iota(jnp.int32, (tm, e * cap), 1)
            hit = jnp.zeros((tm, e * cap), jnp.bool_)
            for ei in range(e):
                tgt = jnp.where(post[:, ei:ei + 1] >= 0, post[:, ei:ei + 1] + ei * cap, -1)
                hit = jnp.logical_or(hit, tgt == slot)
            pt_scr[...] = jnp.where(hit, 1.0, 0.0).astype(BF16)

    y = y_ref[...].reshape(e * cap, y_ref.shape[2])
    o_ref[...] = r_ref[...] + g_ref[...] * _dot(pt_scr[...], y)


def _moe_combine(pos_t, y, row0, resid, gate, cap, *, tm=1024, tn=512):
    b, n, e = pos_t.shape
    d = y.shape[2]
    tm = min(tm, n)
    blk0 = row0 // cap
    per_sample = gate.shape[0] != 1
    g_map = (lambda bi, i, j: (bi, 0, j)) if per_sample else (lambda bi, i, j: (0, 0, j))
    return pl.pallas_call(
        functools.partial(_combine_kernel, cap=cap),
        grid=(b, n // tm, d // tn),
        in_specs=[pl.BlockSpec((None, tm, e), lambda bi, i, j: (bi, i, 0)),
                  pl.BlockSpec((e, cap, tn), lambda bi, i, j: (0, blk0 + bi, j)),
                  pl.BlockSpec((None, tm, tn), lambda bi, i, j: (bi, i, j)),
                  pl.BlockSpec((None, 1, tn), g_map)],
        out_specs=pl.BlockSpec((None, tm, tn), lambda bi, i, j: (bi, i, j)),
        out_shape=jax.ShapeDtypeStruct((b, n, d), F32),
        scratch_shapes=[pltpu.VMEM((tm, e * cap), BF16)],
        compiler_params=_cparams("parallel", "parallel", "arbitrary"),
        name="moe_combine_residual",
    )(pos_t, y, resid, gate)


def _moe_block(streams, nw, w_router, w_gate, w_up, w_down, layer):
    e = w_router.shape[1]
    caps = [CAPACITY_FACTOR * s[0].shape[1] // e for s in streams]
    rows = [s[0].shape[0] * cap for s, cap in zip(streams, caps)]
    row0 = [sum(rows[:t]) for t in range(len(streams))]
    routed, buffers = [None] * len(streams), None
    for t in reversed(range(len(streams))):
        x, sh, sc, _ = streams[t]
        h, logits = _norm_mod_router(x, nw, sh, sc, w_router)
        pos, probs = _route(jnp.swapaxes(logits[..., :e], 1, 2), caps[t])
        buffers = _moe_gather(h, pos, probs, caps[t], buffers)
        routed[t] = jnp.swapaxes(pos, 1, 2)
    y = _moe_ffn(buffers[0], buffers[1], w_gate, w_up, w_down, layer)
    return [_moe_combine(pos_t, y, r0, x, gate, cap)
            for (x, _, _, gate), pos_t, cap, r0 in zip(streams, routed, caps, row0)]


def _final_norm_kernel(x_ref, w_ref, o_ref):
    x = x_ref[...]
    o_ref[...] = x * lax.rsqrt(jnp.mean(x * x, axis=-1, keepdims=True) + EPS) * w_ref[...]


def _final_norm(x, w, *, tm=512):
    b, l, d = x.shape
    return pl.pallas_call(
        _final_norm_kernel,
        grid=(b, l // tm),
        in_specs=[pl.BlockSpec((None, tm, d), lambda bi, i: (bi, i, 0)), pl.BlockSpec((1, d), lambda bi, i: (0, 0))],
        out_specs=pl.BlockSpec((None, tm, d), lambda bi, i: (bi, i, 0)),
        out_shape=jax.ShapeDtypeStruct(x.shape, F32),
        compiler_params=_cparams("parallel", "parallel"),
        name="final_rmsnorm",
    )(x, w.reshape(1, d))


def _rope_tables(seq_len, dim):
    rows = seq_len // GRID_W
    row_id = jnp.repeat(jnp.arange(rows, dtype=F32), GRID_W)
    col_id = jnp.tile(jnp.arange(GRID_W, dtype=F32), rows)
    nf = dim // 4
    inv = ROPE_THETA ** (-jnp.arange(nf, dtype=F32) / nf)
    ang_r = row_id[:, None] * inv
    ang_c = col_id[:, None] * inv
    cos = jnp.concatenate([jnp.cos(ang_r)] * 2 + [jnp.cos(ang_c)] * 2, axis=-1)
    sin = jnp.concatenate([-jnp.sin(ang_r), jnp.sin(ang_r), -jnp.sin(ang_c), jnp.sin(ang_c)], axis=-1)
    return cos, sin


def _identity_tables(seq_len, dim):
    return jnp.ones((seq_len, dim), F32), jnp.zeros((seq_len, dim), F32)


def _layer_bf16(w, layer):
    return w[layer:layer + 1].astype(BF16)


def _attention_mixer(x, ctx, nw, mods_lat, mods_ctx, w_qkv, q_norm, k_norm, w_o, layer, with_ctx_out):
    hd = ATTN_HEAD_DIM
    nq, nkv = ATTN_HEADS * hd, ATTN_KV_HEADS * hd
    l, lc = x.shape[1], ctx.shape[1]
    tn = 512
    colw = jnp.concatenate([jnp.tile(q_norm.astype(F32) * (hd ** -0.5), ATTN_HEADS),
                            jnp.tile(k_norm.astype(F32), ATTN_KV_HEADS), jnp.ones((nkv,), F32)]).reshape(1, -1)
    rope_blocks = (nq + nkv) // tn
    cos, sin = _rope_tables(l, hd)
    icos, isin = _identity_tables(lc, hd)
    kw = dict(colw=colw, rope_blocks=rope_blocks, head_norm=True, half=hd // 4, tn=tn)
    w_in = _layer_bf16(w_qkv, layer)
    p_lat = _project(x, nw, mods_lat[0], mods_lat[1], w_in, 0, cos=cos, sin=sin, **kw)
    off = 0 if with_ctx_out else nq // tn
    p_ctx = _project(ctx, nw, mods_ctx[0], mods_ctx[1], w_in, 0, cos=icos, sin=isin, col_off=off, **kw)
    kc0 = nq - off * tn
    ctx_kv = (p_ctx, kc0, kc0 + nkv)
    o_lat = _attention(p_lat, [ctx_kv, (p_lat, nq, nq + nkv)])
    x = _out_project(o_lat, w_o, layer, x, mods_lat[2])
    if with_ctx_out:
        ctx = _out_project(_attention(p_ctx, [ctx_kv]), w_o, layer, ctx, mods_ctx[2])
    return x, ctx


def _retention_mixer(x, ctx, nw, mods_lat, mods_ctx, w_in, decay_logit, w_o, layer, with_ctx_out):
    dk, hh = RET_QK_DIM, RET_HEADS
    l, lc = x.shape[1], ctx.shape[1]
    tn = 512
    log_g = jax.nn.log_sigmoid(decay_logit.astype(F32))
    colw = jnp.concatenate([jnp.ones((hh * dk,), F32), jnp.full((hh * dk,), dk ** -0.5, F32),
                            jnp.ones((w_in.shape[2] - 2 * hh * dk,), F32)]).reshape(1, -1)
    rope_blocks = 2 * hh * dk // tn
    cos, sin = _rope_tables(l, dk)
    icos, isin = _identity_tables(lc, dk)
    kw = dict(colw=colw, rope_blocks=rope_blocks, head_norm=False, half=dk // 4, tn=tn)
    w_in = _layer_bf16(w_in, layer)
    p_lat = _project(x, nw, mods_lat[0], mods_lat[1], w_in, 0, cos=cos, sin=sin, **kw)
    p_ctx = _project(ctx, nw, mods_ctx[0], mods_ctx[1], w_in, 0, cos=icos, sin=isin, **kw)
    r_ctx, r_lat = _retention(log_g, p_ctx, p_lat, with_ctx_out)
    x = _out_project(r_lat, w_o, layer, x, mods_lat[2])
    if with_ctx_out:
        ctx = _out_project(r_ctx, w_o, layer, ctx, mods_ctx[2])
    return x, ctx


def _hyena_mixer(x, ctx, nw, mods_lat, mods_ctx, w_in, conv_w, conv_b, fparams, skip, w_out, layer, with_ctx_out):
    w_in_b = _layer_bf16(w_in, layer)
    z = _project(x, nw, mods_lat[0], mods_lat[1], w_in_b, 0)
    x = _out_project(_hyena_operator(z, conv_w, conv_b, fparams, skip), w_out, layer, x, mods_lat[2])
    if with_ctx_out:
        zc = _project(ctx, nw, mods_ctx[0], mods_ctx[1], w_in_b, 0)
        ctx = _out_project(_hyena_operator(zc, conv_w, conv_b, fparams, skip), w_out, layer, ctx, mods_ctx[2])
    return x, ctx


def kernel(x, c, ctx, c_ctx, w_mod, b_mod, norm_w, attn_w_qkv, attn_q_norm, attn_k_norm, attn_w_o, ret_w_in, ret_decay_logit, ret_w_o, hy_w_in, hy_conv_w, hy_conv_b, hy_f_w1, hy_f_b1, hy_f_freq1, hy_f_w2, hy_f_b2, hy_f_freq2, hy_f_w3, hy_skip, hy_w_out, moe_router, moe_w_gate, moe_w_up, moe_w_down, final_norm_w):
    depth = w_mod.shape[0]
    b, _, d = x.shape
    rows = -(-(b + 1) // 8) * 8
    c_all = jnp.concatenate([c, c_ctx[None, :], jnp.zeros((rows - b - 1, d), F32)], axis=0)
    mod = _modulation(c_all, w_mod, b_mod)

    for i in range(depth):
        kind, j = i % 3, i // 3
        with_ctx = i < depth - 1
        m = mod[i].reshape(rows, N_MOD, d)
        mods_lat = [m[:b, t][:, None, :] for t in range(N_MOD)]
        mods_ctx = [m[b:b + 1, t][:, None, :] for t in range(N_MOD)]
        nw1, nw2 = norm_w[i, 0], norm_w[i, 1]
        if kind == 0:
            x, ctx = _attention_mixer(x, ctx, nw1, mods_lat, mods_ctx, attn_w_qkv, attn_q_norm[j], attn_k_norm[j],
                                      attn_w_o, j, with_ctx)
        elif kind == 1:
            x, ctx = _retention_mixer(x, ctx, nw1, mods_lat, mods_ctx, ret_w_in, ret_decay_logit[j], ret_w_o,
                                      j, with_ctx)
        else:
            fparams = (hy_f_w1[j], hy_f_b1[j], hy_f_freq1[j], hy_f_w2[j], hy_f_b2[j], hy_f_freq2[j], hy_f_w3[j])
            x, ctx = _hyena_mixer(x, ctx, nw1, mods_lat, mods_ctx, hy_w_in, hy_conv_w[j], hy_conv_b[j], fparams,
                                  hy_skip[j], hy_w_out, j, with_ctx)
        streams = [(x, mods_lat[3], mods_lat[4], mods_lat[5])]
        if with_ctx:
            streams.append((ctx, mods_ctx[3], mods_ctx[4], mods_ctx[5]))
        outs = _moe_block(streams, nw2, moe_router[i], moe_w_gate, moe_w_up, moe_w_down, i)
        x = outs[0]
        if with_ctx:
            ctx = outs[1]
    return _final_norm(x, final_norm_w)
```

```python
import functools
import math

import jax
import jax.numpy as jnp
from jax import lax
from jax.experimental import pallas as pl
from jax.experimental.pallas import tpu as pltpu

F32 = jnp.float32
BF16 = jnp.bfloat16

EPS = 1e-6
GRID_W = 64
ROPE_THETA = 10000.0
ATTN_HEADS = 16
ATTN_KV_HEADS = 4
ATTN_HEAD_DIM = 128
RET_HEADS = 8
RET_QK_DIM = 256
RET_V_DIM = 512
N_EXPERTS = 16
CAPACITY_FACTOR = 2
N_MOD = 6
HYENA_EMB = 33
HYENA_TARGET = 1e-2
HYENA_FAST = 0.3
HYENA_SLOW = 1.5
HYENA_SHIFT = 0.0

LANE = 128
VMEM_LIMIT = 52 * 1024 * 1024
FFN_VMEM_LIMIT = 58 * 1024 * 1024
FFN_MAX_ROWS = 1152


def _cparams(*sem, vmem=VMEM_LIMIT):
    return pltpu.CompilerParams(dimension_semantics=sem, vmem_limit_bytes=vmem)


def _silu(v):
    return v / (1.0 + jnp.exp(-v))


def _split_bf16(v):
    hi = v.astype(BF16)
    lo = (v - hi.astype(F32)).astype(BF16)
    return hi, lo


def _dot(a, b):
    return jnp.dot(a, b, preferred_element_type=F32)


def _dot3(a, b):
    ah, al = _split_bf16(a)
    bh, bl = _split_bf16(b)
    return _dot(ah, bh) + _dot(al, bh) + _dot(ah, bl)


def _dot_nt(a, b):
    return lax.dot_general(a, b, (((1,), (1,)), ((), ())), preferred_element_type=F32)


def _dot_tn(a, b):
    return lax.dot_general(a, b, (((0,), (0,)), ((), ())), preferred_element_type=F32)


def _mod_kernel(c_ref, w_ref, b_ref, o_ref):
    a = _silu(c_ref[...])
    o_ref[...] = _dot3(a, w_ref[...]) + b_ref[...]


def _modulation(c_all, w_mod, b_mod):
    depth, d, n = w_mod.shape
    rows = c_all.shape[0]
    tn = 512
    return pl.pallas_call(
        _mod_kernel,
        grid=(depth, n // tn),
        in_specs=[
            pl.BlockSpec((rows, d), lambda i, j: (0, 0)),
            pl.BlockSpec((None, d, tn), lambda i, j: (i, 0, j)),
            pl.BlockSpec((None, 1, tn), lambda i, j: (i, 0, j)),
        ],
        out_specs=pl.BlockSpec((None, rows, tn), lambda i, j: (i, 0, j)),
        out_shape=jax.ShapeDtypeStruct((depth, rows, n), F32),
        compiler_params=_cparams("parallel", "parallel"),
        name="modulation",
    )(c_all, w_mod, b_mod.reshape(depth, 1, n))


def _norm_mod(x, nw, sh, sc):
    ms = jnp.mean(x * x, axis=-1, keepdims=True)
    y = x * lax.rsqrt(ms + EPS) * nw
    return y * (1.0 + sc) + sh


def _rope_slices(acc, colw, cos, sin, *, head_norm, half, table_w):
    tn = acc.shape[1]
    lane = lax.broadcasted_iota(jnp.int32, (1, LANE), 1)
    first_half = (lane % (2 * half)) < half
    outs = []
    for s in range(tn // LANE):
        xs = acc[:, s * LANE:(s + 1) * LANE]
        if head_norm:
            xs = xs * lax.rsqrt(jnp.mean(xs * xs, axis=-1, keepdims=True) + EPS)
        xs = xs * colw[:, s * LANE:(s + 1) * LANE]
        t0 = (s * LANE) % table_w
        cs = cos[:, t0:t0 + LANE]
        sn = sin[:, t0:t0 + LANE]
        if 2 * half == LANE:
            partner = pltpu.roll(xs, half, axis=1)
        else:
            partner = jnp.where(first_half, pltpu.roll(xs, LANE - half, axis=1), pltpu.roll(xs, half, axis=1))
        outs.append(xs * cs + partner * sn)
    return jnp.concatenate(outs, axis=1)


def _proj_kernel(x_ref, nw_ref, sh_ref, sc_ref, w_ref, colw_ref, cos_ref, sin_ref, o_ref, h_scr,
                 *, rope_blocks, col_off, head_norm, half, table_w):
    j = pl.program_id(2)

    @pl.when(j == 0)
    def _():
        h_scr[...] = _norm_mod(x_ref[...], nw_ref[...], sh_ref[...], sc_ref[...]).astype(BF16)

    acc = _dot(h_scr[...], w_ref[...].astype(BF16))
    o_ref[...] = acc.astype(o_ref.dtype)
    if rope_blocks:
        @pl.when(j + col_off < rope_blocks)
        def _():
            o_ref[...] = _rope_slices(acc, colw_ref[...], cos_ref[...], sin_ref[...], head_norm=head_norm,
                                      half=half, table_w=table_w).astype(o_ref.dtype)


def _project(x, nw, sh, sc, w, layer, *, colw=None, cos=None, sin=None, rope_blocks=0, col_off=0, head_norm=False,
             half=32, tn=512, tm=1024):
    b, l, d = x.shape
    n = w.shape[2] - col_off * tn
    per_sample = sh.shape[0] != 1
    if colw is None:
        colw = jnp.ones((1, w.shape[2]), F32)
        cos = jnp.ones((l, LANE), F32)
        sin = jnp.zeros((l, LANE), F32)
    if not per_sample and b > 1:
        out = _project(x.reshape(1, b * l, d), nw, sh, sc, w, layer, colw=colw, cos=jnp.tile(cos, (b, 1)),
                       sin=jnp.tile(sin, (b, 1)), rope_blocks=rope_blocks, col_off=col_off, head_norm=head_norm,
                       half=half, tn=tn, tm=tm)
        return out.reshape(b, l, n)
    tm = min(tm, l)
    mod_map = (lambda bi, i, j: (bi, 0, 0)) if per_sample else (lambda bi, i, j: (0, 0, 0))
    table_w = cos.shape[1]
    kern = functools.partial(_proj_kernel, rope_blocks=rope_blocks, col_off=col_off, head_norm=head_norm,
                             half=half, table_w=table_w)
    return pl.pallas_call(
        kern,
        grid=(b, l // tm, n // tn),
        in_specs=[
            pl.BlockSpec((None, tm, d), lambda bi, i, j: (bi, i, 0)),
            pl.BlockSpec((1, d), lambda bi, i, j: (0, 0)),
            pl.BlockSpec((None, 1, d), mod_map),
            pl.BlockSpec((None, 1, d), mod_map),
            pl.BlockSpec((None, d, tn), lambda bi, i, j: (layer, 0, j + col_off)),
            pl.BlockSpec((1, tn), lambda bi, i, j: (0, j + col_off)),
            pl.BlockSpec((tm, table_w), lambda bi, i, j: (i, 0)),
            pl.BlockSpec((tm, table_w), lambda bi, i, j: (i, 0)),
        ],
        out_specs=pl.BlockSpec((None, tm, tn), lambda bi, i, j: (bi, i, j)),
        out_shape=jax.ShapeDtypeStruct((b, l, n), BF16),
        scratch_shapes=[pltpu.VMEM((tm, d), BF16)],
        compiler_params=_cparams("parallel", "parallel", "arbitrary"),
        name="norm_mod_project",
    )(x, nw.reshape(1, d), sh, sc, w, colw, cos, sin)


def _out_proj_kernel(a_ref, w_ref, r_ref, g_ref, o_ref):
    acc = _dot(a_ref[...], w_ref[...].astype(BF16))
    o_ref[...] = r_ref[...] + g_ref[...] * acc


def _out_project(a, w, layer, resid, gate, *, tm=None, tn=512):
    b, l, k = a.shape
    n = w.shape[2]
    if tm is None:
        tm = (2048 * 2048) // k
    per_sample = gate.shape[0] != 1
    if not per_sample and b > 1:
        out = _out_project(a.reshape(1, b * l, k), w, layer, resid.reshape(1, b * l, n), gate, tm=tm, tn=tn)
        return out.reshape(b, l, n)
    tm = min(tm, l)
    g_map = (lambda bi, i, j: (bi, 0, j)) if per_sample else (lambda bi, i, j: (0, 0, j))
    return pl.pallas_call(
        _out_proj_kernel,
        grid=(b, l // tm, n // tn),
        in_specs=[
            pl.BlockSpec((None, tm, k), lambda bi, i, j: (bi, i, 0)),
            pl.BlockSpec((None, k, tn), lambda bi, i, j: (layer, 0, j)),
            pl.BlockSpec((None, tm, tn), lambda bi, i, j: (bi, i, j)),
            pl.BlockSpec((None, 1, tn), g_map),
        ],
        out_specs=pl.BlockSpec((None, tm, tn), lambda bi, i, j: (bi, i, j)),
        out_shape=jax.ShapeDtypeStruct((b, l, n), F32),
        compiler_params=_cparams("parallel", "parallel", "parallel"),
        name="out_project_residual",
    )(a, w, resid, gate)


def _attn_kernel(q_ref, *refs, groups, n_kv):
    hd = ATTN_HEAD_DIM
    o_ref = refs[2 * n_kv]
    ks = [refs[2 * t][...] for t in range(n_kv)]
    vs = [jnp.concatenate([refs[2 * t + 1][...], jnp.ones((k.shape[0], hd), BF16)], axis=1) for t, k in enumerate(ks)]
    for g in range(groups):
        q = q_ref[:, g * hd:(g + 1) * hd]
        ss = [_dot_nt(q, k) for k in ks]
        m = functools.reduce(jnp.maximum, [jnp.max(s, axis=-1, keepdims=True) for s in ss])
        ps = [jnp.exp(s - m).astype(BF16) for s in ss]
        ov = sum(_dot(p, v) for p, v in zip(ps, vs))
        o_ref[:, g * hd:(g + 1) * hd] = (ov[:, :hd] / ov[:, hd:hd + 1]).astype(o_ref.dtype)


def _attention(pq, kv_sources, *, tq=512):
    b, lq, _ = pq.shape
    hd = ATTN_HEAD_DIM
    groups = ATTN_HEADS // ATTN_KV_HEADS
    gw = groups * hd
    tq = min(tq, lq)
    in_specs = [pl.BlockSpec((None, tq, gw), lambda bi, h, i: (bi, i, h))]
    args = [pq]
    for p, k0, v0 in kv_sources:
        lk = p.shape[1]
        in_specs.append(pl.BlockSpec((None, lk, hd), lambda bi, h, i, o=k0 // hd: (bi, 0, o + h)))
        in_specs.append(pl.BlockSpec((None, lk, hd), lambda bi, h, i, o=v0 // hd: (bi, 0, o + h)))
        args += [p, p]
    return pl.pallas_call(
        functools.partial(_attn_kernel, groups=groups, n_kv=len(kv_sources)),
        grid=(b, ATTN_KV_HEADS, lq // tq),
        in_specs=in_specs,
        out_specs=pl.BlockSpec((None, tq, gw), lambda bi, h, i: (bi, i, h)),
        out_shape=jax.ShapeDtypeStruct((b, lq, ATTN_HEADS * hd), BF16),
        compiler_params=_cparams("parallel", "parallel", "parallel"),
        name="gqa_attention",
    )(*args)


RET_CHUNK = 256


def _ret_readout(o, g):
    of = o * lax.rsqrt(jnp.mean(o * o, axis=-1, keepdims=True) + EPS)
    return (_silu(g.astype(F32)) * of).astype(BF16)


def _ret_kernel(lg_ref, qc_ref, kc_ref, vc_ref, gc_ref, ql_ref, kl_ref, vl_ref, gl_ref, oc_ref, ol_ref,
                of_scr, ob_scr, sf_scr, sb_scr, *, n_chunks, with_ctx_out):
    c = RET_CHUNK
    h = pl.program_id(1)
    lgf = lg_ref[0, h]
    lgb = lg_ref[1, h]
    row = lax.broadcasted_iota(jnp.int32, (c, 1), 0).astype(F32)
    col = lax.broadcasted_iota(jnp.int32, (1, c), 1).astype(F32)
    diff = row - col
    dmask = jnp.exp(jnp.where(diff >= 0, diff * lgf, -diff * lgb))
    qdec_f = jnp.exp((row + 1.0) * lgf)
    kdec_f = jnp.exp((c - 1.0 - row) * lgf)
    qdec_b = jnp.exp((c - row) * lgb)
    kdec_b = jnp.exp(row * lgb)
    one = jnp.ones((1, 1), F32)
    cdec_f = jnp.exp(one * (c * lgf))
    cdec_b = jnp.exp(one * (c * lgb))

    qc = qc_ref[...]
    kc = kc_ref[...].astype(F32)
    vc = vc_ref[...]
    sf_scr[...] = _dot_tn((kc * kdec_f).astype(BF16), vc)
    sb_scr[...] = _dot_tn((kc * kdec_b).astype(BF16), vc)
    if with_ctx_out:
        inner = (_dot_nt(qc, kc_ref[...]) * dmask).astype(BF16)
        oc_ref[...] = _ret_readout(_dot(inner, vc), gc_ref[...])
    else:
        oc_ref[...] = jnp.zeros(oc_ref.shape, oc_ref.dtype)

    def scan(i, carry):
        sl = pl.ds(pl.multiple_of(i * c, c), c)
        qb = ql_ref[sl, :]
        kb = kl_ref[sl, :]
        v = vl_ref[sl, :]
        inner = (_dot_nt(qb, kb) * dmask).astype(BF16)
        of_scr[sl, :] = _dot(inner, v) + _dot((qb.astype(F32) * qdec_f).astype(BF16), sf_scr[...].astype(BF16))
        sf_scr[...] = sf_scr[...] * cdec_f + _dot_tn((kb.astype(F32) * kdec_f).astype(BF16), v)

        sr = pl.ds(pl.multiple_of((n_chunks - 1 - i) * c, c), c)
        q = ql_ref[sr, :].astype(F32)
        k = kl_ref[sr, :].astype(F32)
        ob_scr[sr, :] = _dot((q * qdec_b).astype(BF16), sb_scr[...].astype(BF16))
        sb_scr[...] = sb_scr[...] * cdec_b + _dot_tn((k * kdec_b).astype(BF16), vl_ref[sr, :])
        return carry

    lax.fori_loop(0, n_chunks, scan, 0)

    def readout(i, carry):
        sl = pl.ds(pl.multiple_of(i * c, c), c)
        ol_ref[sl, :] = _ret_readout(of_scr[sl, :] + ob_scr[sl, :], gl_ref[sl, :])
        return carry

    lax.fori_loop(0, n_chunks, readout, 0)


def _retention(log_g, p_ctx, p_lat, with_ctx_out):
    b, l, _ = p_lat.shape
    lc = p_ctx.shape[1]
    assert lc == RET_CHUNK and l % RET_CHUNK == 0
    dk, dv, hh = RET_QK_DIM, RET_V_DIM, RET_HEADS
    k_off = hh * dk // dk
    v_off = 2 * hh * dk // dv
    g_off = v_off + hh

    def specs(ln):
        return [
            pl.BlockSpec((None, ln, dk), lambda bi, h: (bi, 0, h)),
            pl.BlockSpec((None, ln, dk), lambda bi, h: (bi, 0, k_off + h)),
            pl.BlockSpec((None, ln, dv), lambda bi, h: (bi, 0, v_off + h)),
            pl.BlockSpec((None, ln, dv), lambda bi, h: (bi, 0, g_off + h)),
        ]

    return pl.pallas_call(
        functools.partial(_ret_kernel, n_chunks=l // RET_CHUNK, with_ctx_out=with_ctx_out),
        grid=(b, hh),
        in_specs=[pl.BlockSpec(memory_space=pltpu.SMEM)] + specs(lc) + specs(l),
        out_specs=[
            pl.BlockSpec((None, lc, dv), lambda bi, h: (bi, 0, h)),
            pl.BlockSpec((None, l, dv), lambda bi, h: (bi, 0, h)),
        ],
        out_shape=[
            jax.ShapeDtypeStruct((b, lc, hh * dv), BF16),
            jax.ShapeDtypeStruct((b, l, hh * dv), BF16),
        ],
        scratch_shapes=[pltpu.VMEM((l, dv), F32), pltpu.VMEM((l, dv), F32), pltpu.VMEM((dk, dv), F32),
                        pltpu.VMEM((dk, dv), F32)],
        compiler_params=_cparams("parallel", "parallel"),
        name="retention",
    )(log_g, p_ctx, p_ctx, p_ctx, p_ctx, p_lat, p_lat, p_lat, p_lat)


def _hy_gate_kernel(x0_ref, x1_ref, v_ref, cw0_ref, cw1_ref, cwv_ref, cb0_ref, cb1_ref, cbv_ref, u_ref, g_ref):
    l = x0_ref.shape[0]
    t = lax.broadcasted_iota(jnp.int32, (l, 1), 0)

    def conv3(z_ref, cw_ref, cb_ref):
        z = z_ref[...].astype(F32)
        prev = jnp.where(t == 0, 0.0, pltpu.roll(z, 1, axis=0))
        nxt = jnp.where(t == l - 1, 0.0, pltpu.roll(z, l - 1, axis=0))
        cw = cw_ref[...]
        return prev * cw[0:1, :] + z * cw[1:2, :] + nxt * cw[2:3, :] + cb_ref[...]

    x1 = conv3(x1_ref, cw1_ref, cb1_ref)
    v = conv3(v_ref, cwv_ref, cbv_ref)
    u_ref[...] = (v * x1).astype(u_ref.dtype)
    g_ref[...] = conv3(x0_ref, cw0_ref, cb0_ref).astype(g_ref.dtype)


def _hyena_gate(z, conv_w, conv_b, *, tn=256):
    b, l, d3 = z.shape
    d = d3 // 3
    nb = d // tn
    cb = conv_b.reshape(1, d3)
    zs = [pl.BlockSpec((None, l, tn), (lambda bi, j, o=o: (bi, 0, j + o * nb))) for o in range(3)]
    ws = [pl.BlockSpec((3, tn), (lambda bi, j, o=o: (0, j + o * nb))) for o in range(3)]
    bs = [pl.BlockSpec((1, tn), (lambda bi, j, o=o: (0, j + o * nb))) for o in range(3)]
    return pl.pallas_call(
        _hy_gate_kernel,
        grid=(b, nb),
        in_specs=zs + ws + bs,
        out_specs=[pl.BlockSpec((None, l, tn), lambda bi, j: (bi, 0, j))] * 2,
        out_shape=[jax.ShapeDtypeStruct((b, l, d), BF16)] * 2,
        compiler_params=_cparams("parallel", "parallel"),
        name="hyena_conv3_gate",
    )(z, z, z, conv_w, conv_w, conv_w, cb, cb, cb)


def _hy_filter_kernel(z_ref, w1_ref, b1_ref, f1_ref, w2_ref, b2_ref, f2_ref, w3f_ref, w3b_ref, dec_ref, o_ref, h_scr):
    l = z_ref.shape[0]

    @pl.when(pl.program_id(0) == 0)
    def _():
        h1 = jnp.sin(f1_ref[...] * (_dot3(z_ref[...], w1_ref[...]) + b1_ref[...]))
        h_scr[...] = jnp.sin(f2_ref[...] * (_dot3(h1, w2_ref[...]) + b2_ref[...]))

    h = h_scr[...]
    decay = dec_ref[...] + HYENA_SHIFT
    hf = _dot3(h, w3f_ref[...]) * decay
    hb = _dot3(h, w3b_ref[...]) * decay
    t = lax.broadcasted_iota(jnp.int32, (l, 1), 0)
    hb = jnp.where(t == 0, 0.0, hb)
    norm = jnp.sum(jnp.abs(hf), axis=0, keepdims=True) + jnp.sum(jnp.abs(hb), axis=0, keepdims=True)
    o_ref[0] = (hf / norm).astype(o_ref.dtype)
    o_ref[1] = (hb / norm).astype(o_ref.dtype)


def _hyena_filter(l, d, w1, b1, fr1, w2, b2, fr2, w3, *, tn=256):
    t = jnp.linspace(0.0, 1.0, l, dtype=F32)[:, None]
    bands = (HYENA_EMB - 1) // 2
    w = 2.0 * math.pi * jnp.arange(l, dtype=F32)[:, None] / l
    f = jnp.linspace(1e-4, bands - 1, bands, dtype=F32)[None, :]
    z = jnp.concatenate([t, jnp.cos(f * w), -jnp.sin(f * w)], axis=-1)
    deltas = jnp.abs(jnp.linspace(math.log(HYENA_TARGET) / HYENA_SLOW, math.log(HYENA_TARGET) / HYENA_FAST, d, dtype=F32))
    decay = jnp.exp(-t * deltas)
    fw = w1.shape[1]
    pad = lambda a, r, c: jnp.pad(a.astype(F32), ((0, r - a.shape[0]), (0, c - a.shape[1])))
    z = pad(z, l, LANE)
    w1p = pad(w1, LANE, LANE)
    w2p = pad(w2, LANE, LANE)
    w3p = pad(w3, LANE, 2 * d)
    vec = lambda a: pad(a.reshape(1, fw), 1, LANE)
    nb = d // tn
    full = lambda shape: pl.BlockSpec(shape, lambda j: (0, 0))
    return pl.pallas_call(
        _hy_filter_kernel,
        grid=(nb,),
        in_specs=[full((l, LANE)), full((LANE, LANE)), full((1, LANE)), full((1, LANE)), full((LANE, LANE)),
                  full((1, LANE)), full((1, LANE)),
                  pl.BlockSpec((LANE, tn), lambda j: (0, j)),
                  pl.BlockSpec((LANE, tn), lambda j: (0, j + nb)),
                  pl.BlockSpec((l, tn), lambda j: (0, j))],
        out_specs=pl.BlockSpec((2, l, tn), lambda j: (0, 0, j)),
        out_shape=jax.ShapeDtypeStruct((2, l, d), BF16),
        scratch_shapes=[pltpu.VMEM((l, LANE), F32)],
        compiler_params=_cparams("arbitrary"),
        name="hyena_filter",
    )(z, w1p, vec(b1), vec(fr1), w2p, vec(b2), vec(fr2), w3p, w3p, decay)


def _dft_tables(l):
    n = 2 * l
    k = jnp.arange(l, dtype=jnp.int32)
    nb = 1 << ((l.bit_length() - 1 + 1) // 2)
    na = l // nb
    theta = lambda m: (m % n).astype(F32) * (2.0 * math.pi / n)
    ang_a = theta(k[:, None] * (jnp.arange(na, dtype=jnp.int32) * nb)[None, :])[:, :, None]
    ang_b = theta(k[:, None] * jnp.arange(nb, dtype=jnp.int32)[None, :])[:, None, :]
    cs = (jnp.cos(ang_a) * jnp.cos(ang_b) - jnp.sin(ang_a) * jnp.sin(ang_b)).reshape(l, l)
    sn = (jnp.sin(ang_a) * jnp.cos(ang_b) + jnp.cos(ang_a) * jnp.sin(ang_b)).reshape(l, l)
    alt = jnp.where(k % 2 == 0, 1.0, -1.0).astype(F32)
    f_b = jnp.where(k[:, None] == 0, alt[None, :], -sn)
    fwd = jnp.concatenate([cs, f_b], axis=0)
    wa = jnp.where(k[None, :] == 0, 1.0, 2.0) * cs.T
    wb = jnp.where(k[None, :] == 0, alt[:, None], -2.0 * sn.T)
    inv = jnp.concatenate([wa, wb], axis=1) * (1.0 / n)
    return fwd.astype(BF16), inv.astype(BF16)


def _dft_raw_kernel(fa_ref, fb_ref, u_ref, o_ref):
    u = u_ref[...]
    o_ref[0] = _dot(fa_ref[...], u)
    o_ref[1] = _dot(fb_ref[...], u)


def _dft_raw(fwd, u, *, tm=512, tn=512):
    b, l, d = u.shape
    tm, tn = min(tm, l), min(tn, d)
    nb = l // tm
    return pl.pallas_call(
        _dft_raw_kernel,
        grid=(b, d // tn, nb),
        in_specs=[pl.BlockSpec((tm, l), lambda bi, j, i: (i, 0)),
                  pl.BlockSpec((tm, l), lambda bi, j, i: (i + nb, 0)),
                  pl.BlockSpec((None, l, tn), lambda bi, j, i: (bi, 0, j))],
        out_specs=pl.BlockSpec((None, 2, tm, tn), lambda bi, j, i: (bi, 0, i, j)),
        out_shape=jax.ShapeDtypeStruct((b, 2, l, d), F32),
        compiler_params=_cparams("parallel", "parallel", "parallel"),
        name="hyena_dft_filter",
    )(fwd, fwd, u)


def _dft_mul_kernel(fa_ref, fb_ref, u_ref, hs_ref, o_ref):
    i = pl.program_id(0)
    u = u_ref[...]
    ua = _dot(fa_ref[...], u)
    ub = _dot(fb_ref[...], u)
    first = jnp.logical_and(lax.broadcasted_iota(jnp.int32, (ua.shape[0], 1), 0) == 0, i == 0)
    ha = hs_ref[0, 0] + hs_ref[1, 0]
    hb = jnp.where(first, hs_ref[0, 1] + hs_ref[1, 1], hs_ref[0, 1] - hs_ref[1, 1])
    pa = jnp.where(first, ua * ha, ua * ha - ub * hb)
    pb = jnp.where(first, ub * hb, ua * hb + ub * ha)
    o_ref[0] = pa.astype(o_ref.dtype)
    o_ref[1] = pb.astype(o_ref.dtype)


def _dft_mul(fwd, u, hspec, *, tm=512, tn=512):
    b, l, d = u.shape
    tm, tn = min(tm, l), min(tn, d)
    nb = l // tm
    out = pl.pallas_call(
        _dft_mul_kernel,
        grid=(nb, d // tn, b),
        in_specs=[pl.BlockSpec((tm, l), lambda i, j, bi: (i, 0)),
                  pl.BlockSpec((tm, l), lambda i, j, bi: (i + nb, 0)),
                  pl.BlockSpec((None, l, tn), lambda i, j, bi: (bi, 0, j)),
                  pl.BlockSpec((2, 2, tm, tn), lambda i, j, bi: (0, 0, i, j))],
        out_specs=pl.BlockSpec((None, 2, tm, tn), lambda i, j, bi: (bi, 0, i, j)),
        out_shape=jax.ShapeDtypeStruct((b, 2, l, d), BF16),
        compiler_params=_cparams("parallel", "parallel", "parallel"),
        name="hyena_dft_forward",
    )(fwd, fwd, u, hspec)
    return out.reshape(b, 2 * l, d)


def _idft_kernel(g_ref, p_ref, u_ref, x0_ref, skip_ref, o_ref):
    y = _dot(g_ref[...], p_ref[...]) + u_ref[...].astype(F32) * skip_ref[...]
    o_ref[...] = (y * x0_ref[...].astype(F32)).astype(o_ref.dtype)


def _idft_gate(inv, p, u, x0, skip, *, tm=1024, tn=512):
    b, l, d = u.shape
    tm, tn = min(tm, l), min(tn, d)
    return pl.pallas_call(
        _idft_kernel,
        grid=(l // tm, b, d // tn),
        in_specs=[pl.BlockSpec((tm, 2 * l), lambda i, bi, j: (i, 0)),
                  pl.BlockSpec((None, 2 * l, tn), lambda i, bi, j: (bi, 0, j)),
                  pl.BlockSpec((None, tm, tn), lambda i, bi, j: (bi, i, j)),
                  pl.BlockSpec((None, tm, tn), lambda i, bi, j: (bi, i, j)),
                  pl.BlockSpec((1, tn), lambda i, bi, j: (0, j))],
        out_specs=pl.BlockSpec((None, tm, tn), lambda i, bi, j: (bi, i, j)),
        out_shape=jax.ShapeDtypeStruct((b, l, d), BF16),
        compiler_params=_cparams("parallel", "parallel", "parallel"),
        name="hyena_dft_inverse",
    )(inv, p, u, x0, skip.reshape(1, d))


def _hyena_operator(z, conv_w, conv_b, fparams, skip):
    b, l, d3 = z.shape
    d = d3 // 3
    u, x0 = _hyena_gate(z, conv_w, conv_b)
    taps = _hyena_filter(l, d, *fparams)
    fwd, inv = _dft_tables(l)
    hspec = _dft_raw(fwd, taps)
    p = _dft_mul(fwd, u, hspec)
    return _idft_gate(inv, p, u, x0, skip)


def _router_kernel(x_ref, nw_ref, sh_ref, sc_ref, wr_ref, h_ref, lg_ref):
    h = _norm_mod(x_ref[...], nw_ref[...], sh_ref[...], sc_ref[...])
    h_ref[...] = h.astype(BF16)
    lg_ref[...] = _dot3(h, wr_ref[...])


def _norm_mod_router(x, nw, sh, sc, w_router, *, tm=512):
    b, l, d = x.shape
    n_e = w_router.shape[1]
    e = LANE
    w_router = jnp.pad(w_router, ((0, 0), (0, e - n_e)))
    tm = min(tm, l)
    per_sample = sh.shape[0] != 1
    mod_map = (lambda bi, i: (bi, 0, 0)) if per_sample else (lambda bi, i: (0, 0, 0))
    return pl.pallas_call(
        _router_kernel,
        grid=(b, l // tm),
        in_specs=[
            pl.BlockSpec((None, tm, d), lambda bi, i: (bi, i, 0)),
            pl.BlockSpec((1, d), lambda bi, i: (0, 0)),
            pl.BlockSpec((None, 1, d), mod_map),
            pl.BlockSpec((None, 1, d), mod_map),
            pl.BlockSpec((d, e), lambda bi, i: (0, 0)),
        ],
        out_specs=[pl.BlockSpec((None, tm, d), lambda bi, i: (bi, i, 0)),
                   pl.BlockSpec((None, tm, e), lambda bi, i: (bi, i, 0))],
        out_shape=[jax.ShapeDtypeStruct((b, l, d), BF16), jax.ShapeDtypeStruct((b, l, e), F32)],
        compiler_params=_cparams("parallel", "parallel"),
        name="norm_mod_router",
    )(x, nw.reshape(1, d), sh, sc, w_router)


def _route_kernel(lg_ref, tri_ref, pos_ref, prob_ref, *, cap):
    lg = lg_ref[...]
    m = jnp.max(lg, axis=0, keepdims=True)
    ex = jnp.exp(lg - m)
    probs = ex / jnp.sum(ex, axis=0, keepdims=True)
    bits = lax.bitcast_convert_type(probs, jnp.int32)

    def count(mask):
        return jnp.sum(jnp.where(mask, 1.0, 0.0), axis=1, keepdims=True)

    def step(i, thr):
        trial = thr | lax.shift_left(jnp.int32(1), 30 - i)
        return jnp.where(count(bits >= trial) >= cap, trial, thr)

    thr = lax.fori_loop(0, 31, step, jnp.zeros((lg.shape[0], 1), jnp.int32))
    gt = bits > thr
    eq = bits == thr
    need = cap - count(gt).astype(jnp.int32)
    both = jnp.concatenate([jnp.where(gt, 1.0, 0.0), jnp.where(eq, 1.0, 0.0)], axis=0).astype(BF16)
    csum = _dot(both, tri_ref[...])
    e = lg.shape[0]
    rank_gt = csum[:e].astype(jnp.int32)
    rank_eq = csum[e:].astype(jnp.int32)
    sel = jnp.logical_or(gt, jnp.logical_and(eq, rank_eq < need))
    pos = rank_gt + jnp.minimum(rank_eq, need)
    pos_ref[...] = jnp.where(sel, pos, -1)
    prob_ref[...] = probs


def _route(logits_t, cap):
    b, e, n = logits_t.shape
    idx = jnp.arange(n, dtype=jnp.int32)
    tri = (idx[:, None] < idx[None, :]).astype(BF16)
    return pl.pallas_call(
        functools.partial(_route_kernel, cap=cap),
        grid=(b,),
        in_specs=[pl.BlockSpec((None, e, n), lambda bi: (bi, 0, 0)),
                  pl.BlockSpec((n, n), lambda bi: (0, 0))],
        out_specs=[pl.BlockSpec((None, e, n), lambda bi: (bi, 0, 0))] * 2,
        out_shape=[jax.ShapeDtypeStruct((b, e, n), jnp.int32), jax.ShapeDtypeStruct((b, e, n), F32)],
        compiler_params=_cparams("parallel"),
        name="expert_choice_route",
    )(logits_t, tri)


def _gather_kernel(h_ref, pos_ref, prob_ref, xg_ref, gate_ref, *, cap):
    n = h_ref.shape[0]
    slot = lax.broadcasted_iota(jnp.int32, (cap, n), 0)
    match = slot == pos_ref[...]
    onehot = jnp.where(match, 1.0, 0.0).astype(BF16)
    xg_ref[...] = _dot(onehot, h_ref[...]).astype(xg_ref.dtype)
    gate_ref[...] = jnp.sum(jnp.where(match, prob_ref[...], 0.0), axis=1, keepdims=True)


def _gather_append_kernel(h_ref, pos_ref, prob_ref, xin_ref, gin_ref, xg_ref, gate_ref, *, cap, nb):
    bi = pl.program_id(0)

    @pl.when(bi < nb)
    def _():
        _gather_kernel(h_ref, pos_ref, prob_ref, xg_ref, gate_ref, cap=cap)

    @pl.when(bi >= nb)
    def _():
        xg_ref[...] = xin_ref[...]
        gate_ref[...] = gin_ref[...]


def _moe_gather(h, pos, probs, cap, tail=None):
    b, n, d = h.shape
    e = pos.shape[1]
    row = lambda a: a.reshape(b, e, 1, n)
    if tail is None:
        nx, kern, clamp = 0, functools.partial(_gather_kernel, cap=cap), lambda bi: bi
        tail_specs, tail_args = [], []
    else:
        r = tail[0].shape[1]
        assert r % cap == 0
        nx, kern = r // cap, functools.partial(_gather_append_kernel, cap=cap, nb=b)
        clamp = lambda bi: jnp.minimum(bi, b - 1)
        tail_map = lambda bi, ei: (ei, jnp.maximum(bi - b, 0), 0)
        tail_specs = [pl.BlockSpec((None, cap, d), tail_map), pl.BlockSpec((None, cap, 1), tail_map)]
        tail_args = list(tail)
    m_total = (b + nx) * cap
    return pl.pallas_call(
        kern,
        grid=(b + nx, e),
        in_specs=[pl.BlockSpec((None, n, d), lambda bi, ei: (clamp(bi), 0, 0)),
                  pl.BlockSpec((None, None, 1, n), lambda bi, ei: (clamp(bi), ei, 0, 0)),
                  pl.BlockSpec((None, None, 1, n), lambda bi, ei: (clamp(bi), ei, 0, 0))] + tail_specs,
        out_specs=[pl.BlockSpec((None, cap, d), lambda bi, ei: (ei, bi, 0)),
                   pl.BlockSpec((None, cap, 1), lambda bi, ei: (ei, bi, 0))],
        out_shape=[jax.ShapeDtypeStruct((e, m_total, d), BF16), jax.ShapeDtypeStruct((e, m_total, 1), F32)],
        compiler_params=_cparams("parallel", "parallel"),
        name="moe_gather",
    )(h, row(pos), row(probs), *tail_args)


def _ffn_kernel(x_ref, g_ref, wg_ref, wu_ref, wd_ref, o_ref, acc_ref):
    f = pl.program_id(2)
    x = x_ref[...]
    a = _dot(x, wg_ref[...].astype(BF16))
    u = _dot(x, wu_ref[...].astype(BF16))
    hmid = (_silu(a) * u * g_ref[...]).astype(BF16)
    y = _dot(hmid, wd_ref[...].astype(BF16))
    acc = jnp.where(f == 0, y, acc_ref[...] + y)
    acc_ref[...] = acc
    o_ref[...] = acc.astype(o_ref.dtype)


def _moe_ffn(xg, gate, w_gate, w_up, w_down, layer, *, tf=256):
    e, m, d = xg.shape
    ff = w_gate.shape[3]
    tm = max(t for t in range(16, FFN_MAX_ROWS + 1, 16) if m % t == 0)
    return pl.pallas_call(
        _ffn_kernel,
        grid=(e, m // tm, ff // tf),
        in_specs=[pl.BlockSpec((None, tm, d), lambda ei, i, f: (ei, i, 0)),
                  pl.BlockSpec((None, tm, 1), lambda ei, i, f: (ei, i, 0)),
                  pl.BlockSpec((None, None, d, tf), lambda ei, i, f: (layer, ei, 0, f)),
                  pl.BlockSpec((None, None, d, tf), lambda ei, i, f: (layer, ei, 0, f)),
                  pl.BlockSpec((None, None, tf, d), lambda ei, i, f: (layer, ei, f, 0))],
        out_specs=pl.BlockSpec((None, tm, d), lambda ei, i, f: (ei, i, 0)),
        out_shape=jax.ShapeDtypeStruct((e, m, d), BF16),
        scratch_shapes=[pltpu.VMEM((tm, d), F32)],
        compiler_params=_cparams("parallel", "parallel", "arbitrary", vmem=FFN_VMEM_LIMIT),
        name="moe_expert_ffn",
    )(xg, gate, w_gate, w_up, w_down)


def _combine_kernel(post_ref, y_ref, r_ref, g_ref, o_ref, pt_scr, *, cap):
    j = pl.program_id(2)
    tm, e = post_ref.shape

    @pl.when(j == 0)
    def _():
        post = post_ref[...]
        if cap % LANE == 0:
            slot = lax.broadcasted_iota(jnp.int32, (tm, cap), 1)
            for ei in range(e):
                pt_scr[:, ei * cap:(ei + 1) * cap] = jnp.where(post[:, ei:ei + 1] == slot, 1.0, 0.0).astype(BF16)
        else:
            slot = lax.broadcasted_iota(jnp.int32, (tm, e * cap), 1)
            hit = jnp.zeros((tm, e * cap), jnp.bool_)
            for ei in range(e):
                tgt = jnp.where(post[:, ei:ei + 1] >= 0, post[:, ei:ei + 1] + ei * cap, -1)
                hit = jnp.logical_or(hit, tgt == slot)
            pt_scr[...] = jnp.where(hit, 1.0, 0.0).astype(BF16)

    y = y_ref[...].reshape(e * cap, y_ref.shape[2])
    o_ref[...] = r_ref[...] + g_ref[...] * _dot(pt_scr[...], y)


def _moe_combine(pos_t, y, row0, resid, gate, cap, *, tm=1024, tn=512):
    b, n, e = pos_t.shape
    d = y.shape[2]
    tm = min(tm, n)
    blk0 = row0 // cap
    per_sample = gate.shape[0] != 1
    g_map = (lambda bi, i, j: (bi, 0, j)) if per_sample else (lambda bi, i, j: (0, 0, j))
    return pl.pallas_call(
        functools.partial(_combine_kernel, cap=cap),
        grid=(b, n // tm, d // tn),
        in_specs=[pl.BlockSpec((None, tm, e), lambda bi, i, j: (bi, i, 0)),
                  pl.BlockSpec((e, cap, tn), lambda bi, i, j: (0, blk0 + bi, j)),
                  pl.BlockSpec((None, tm, tn), lambda bi, i, j: (bi, i, j)),
                  pl.BlockSpec((None, 1, tn), g_map)],
        out_specs=pl.BlockSpec((None, tm, tn), lambda bi, i, j: (bi, i, j)),
        out_shape=jax.ShapeDtypeStruct((b, n, d), F32),
        scratch_shapes=[pltpu.VMEM((tm, e * cap), BF16)],
        compiler_params=_cparams("parallel", "parallel", "arbitrary"),
        name="moe_combine_residual",
    )(pos_t, y, resid, gate)


def _moe_block(streams, nw, w_router, w_gate, w_up, w_down, layer):
    e = w_router.shape[1]
    caps = [CAPACITY_FACTOR * s[0].shape[1] // e for s in streams]
    rows = [s[0].shape[0] * cap for s, cap in zip(streams, caps)]
    row0 = [sum(rows[:t]) for t in range(len(streams))]
    routed, buffers = [None] * len(streams), None
    for t in reversed(range(len(streams))):
        x, sh, sc, _ = streams[t]
        h, logits = _norm_mod_router(x, nw, sh, sc, w_router)
        pos, probs = _route(jnp.swapaxes(logits[..., :e], 1, 2), caps[t])
        buffers = _moe_gather(h, pos, probs, caps[t], buffers)
        routed[t] = jnp.swapaxes(pos, 1, 2)
    y = _moe_ffn(buffers[0], buffers[1], w_gate, w_up, w_down, layer)
    return [_moe_combine(pos_t, y, r0, x, gate, cap)
            for (x, _, _, gate), pos_t, cap, r0 in zip(streams, routed, caps, row0)]


def _final_norm_kernel(x_ref, w_ref, o_ref):
    x = x_ref[...]
    o_ref[...] = x * lax.rsqrt(jnp.mean(x * x, axis=-1, keepdims=True) + EPS) * w_ref[...]


def _final_norm(x, w, *, tm=512):
    b, l, d = x.shape
    return pl.pallas_call(
        _final_norm_kernel,
        grid=(b, l // tm),
        in_specs=[pl.BlockSpec((None, tm, d), lambda bi, i: (bi, i, 0)), pl.BlockSpec((1, d), lambda bi, i: (0, 0))],
        out_specs=pl.BlockSpec((None, tm, d), lambda bi, i: (bi, i, 0)),
        out_shape=jax.ShapeDtypeStruct(x.shape, F32),
        compiler_params=_cparams("parallel", "parallel"),
        name="final_rmsnorm",
    )(x, w.reshape(1, d))


def _rope_tables(seq_len, dim):
    rows = seq_len // GRID_W
    row_id = jnp.repeat(jnp.arange(rows, dtype=F32), GRID_W)
    col_id = jnp.tile(jnp.arange(GRID_W, dtype=F32), rows)
    nf = dim // 4
    inv = ROPE_THETA ** (-jnp.arange(nf, dtype=F32) / nf)
    ang_r = row_id[:, None] * inv
    ang_c = col_id[:, None] * inv
    cos = jnp.concatenate([jnp.cos(ang_r)] * 2 + [jnp.cos(ang_c)] * 2, axis=-1)
    sin = jnp.concatenate([-jnp.sin(ang_r), jnp.sin(ang_r), -jnp.sin(ang_c), jnp.sin(ang_c)], axis=-1)
    return cos, sin


def _identity_tables(seq_len, dim):
    return jnp.ones((seq_len, dim), F32), jnp.zeros((seq_len, dim), F32)


def _attention_mixer(x, ctx, nw, mods_lat, mods_ctx, w_qkv, q_norm, k_norm, w_o, layer, with_ctx_out):
    hd = ATTN_HEAD_DIM
    nq, nkv = ATTN_HEADS * hd, ATTN_KV_HEADS * hd
    l, lc = x.shape[1], ctx.shape[1]
    tn = 512
    colw = jnp.concatenate([jnp.tile(q_norm.astype(F32) * (hd ** -0.5), ATTN_HEADS),
                            jnp.tile(k_norm.astype(F32), ATTN_KV_HEADS), jnp.ones((nkv,), F32)]).reshape(1, -1)
    rope_blocks = (nq + nkv) // tn
    cos, sin = _rope_tables(l, hd)
    icos, isin = _identity_tables(lc, hd)
    kw = dict(colw=colw, rope_blocks=rope_blocks, head_norm=True, half=hd // 4, tn=tn)
    p_lat = _project(x, nw, mods_lat[0], mods_lat[1], w_qkv, layer, cos=cos, sin=sin, **kw)
    off = 0 if with_ctx_out else nq // tn
    p_ctx = _project(ctx, nw, mods_ctx[0], mods_ctx[1], w_qkv, layer, cos=icos, sin=isin, col_off=off, **kw)
    kc0 = nq - off * tn
    ctx_kv = (p_ctx, kc0, kc0 + nkv)
    o_lat = _attention(p_lat, [ctx_kv, (p_lat, nq, nq + nkv)])
    x = _out_project(o_lat, w_o, layer, x, mods_lat[2])
    if with_ctx_out:
        ctx = _out_project(_attention(p_ctx, [ctx_kv]), w_o, layer, ctx, mods_ctx[2])
    return x, ctx


def _retention_mixer(x, ctx, nw, mods_lat, mods_ctx, w_in, decay_logit, w_o, layer, with_ctx_out):
    dk, hh = RET_QK_DIM, RET_HEADS
    l, lc = x.shape[1], ctx.shape[1]
    tn = 512
    log_g = jax.nn.log_sigmoid(decay_logit.astype(F32))
    colw = jnp.concatenate([jnp.ones((hh * dk,), F32), jnp.full((hh * dk,), dk ** -0.5, F32),
                            jnp.ones((w_in.shape[2] - 2 * hh * dk,), F32)]).reshape(1, -1)
    rope_blocks = 2 * hh * dk // tn
    cos, sin = _rope_tables(l, dk)
    icos, isin = _identity_tables(lc, dk)
    kw = dict(colw=colw, rope_blocks=rope_blocks, head_norm=False, half=dk // 4, tn=tn)
    p_lat = _project(x, nw, mods_lat[0], mods_lat[1], w_in, layer, cos=cos, sin=sin, **kw)
    p_ctx = _project(ctx, nw, mods_ctx[0], mods_ctx[1], w_in, layer, cos=icos, sin=isin, **kw)
    r_ctx, r_lat = _retention(log_g, p_ctx, p_lat, with_ctx_out)
    x = _out_project(r_lat, w_o, layer, x, mods_lat[2])
    if with_ctx_out:
        ctx = _out_project(r_ctx, w_o, layer, ctx, mods_ctx[2])
    return x, ctx


def _hyena_mixer(x, ctx, nw, mods_lat, mods_ctx, w_in, conv_w, conv_b, fparams, skip, w_out, layer, with_ctx_out):
    z = _project(x, nw, mods_lat[0], mods_lat[1], w_in, layer)
    x = _out_project(_hyena_operator(z, conv_w, conv_b, fparams, skip), w_out, layer, x, mods_lat[2])
    if with_ctx_out:
        zc = _project(ctx, nw, mods_ctx[0], mods_ctx[1], w_in, layer)
        ctx = _out_project(_hyena_operator(zc, conv_w, conv_b, fparams, skip), w_out, layer, ctx, mods_ctx[2])
    return x, ctx


def kernel(x, c, ctx, c_ctx, w_mod, b_mod, norm_w, attn_w_qkv, attn_q_norm, attn_k_norm, attn_w_o, ret_w_in, ret_decay_logit, ret_w_o, hy_w_in, hy_conv_w, hy_conv_b, hy_f_w1, hy_f_b1, hy_f_freq1, hy_f_w2, hy_f_b2, hy_f_freq2, hy_f_w3, hy_skip, hy_w_out, moe_router, moe_w_gate, moe_w_up, moe_w_down, final_norm_w):
    depth = w_mod.shape[0]
    b, _, d = x.shape
    rows = -(-(b + 1) // 8) * 8
    c_all = jnp.concatenate([c, c_ctx[None, :], jnp.zeros((rows - b - 1, d), F32)], axis=0)
    mod = _modulation(c_all, w_mod, b_mod)

    for i in range(depth):
        kind, j = i % 3, i // 3
        with_ctx = i < depth - 1
        m = mod[i].reshape(rows, N_MOD, d)
        mods_lat = [m[:b, t][:, None, :] for t in range(N_MOD)]
        mods_ctx = [m[b:b + 1, t][:, None, :] for t in range(N_MOD)]
        nw1, nw2 = norm_w[i, 0], norm_w[i, 1]
        if kind == 0:
            x, ctx = _attention_mixer(x, ctx, nw1, mods_lat, mods_ctx, attn_w_qkv, attn_q_norm[j], attn_k_norm[j],
                                      attn_w_o, j, with_ctx)
        elif kind == 1:
            x, ctx = _retention_mixer(x, ctx, nw1, mods_lat, mods_ctx, ret_w_in, ret_decay_logit[j], ret_w_o,
                                      j, with_ctx)
        else:
            fparams = (hy_f_w1[j], hy_f_b1[j], hy_f_freq1[j], hy_f_w2[j], hy_f_b2[j], hy_f_freq2[j], hy_f_w3[j])
            x, ctx = _hyena_mixer(x, ctx, nw1, mods_lat, mods_ctx, hy_w_in, hy_conv_w[j], hy_conv_b[j], fparams,
                                  hy_skip[j], hy_w_out, j, with_ctx)
        streams = [(x, mods_lat[3], mods_lat[4], mods_lat[5])]
        if with_ctx:
            streams.append((ctx, mods_ctx[3], mods_ctx[4], mods_ctx[5]))
        outs = _moe_block(streams, nw2, moe_router[i], moe_w_gate, moe_w_up, moe_w_down, i)
        x = outs[0]
        if with_ctx:
            ctx = outs[1]
    return _final_norm(x, final_norm_w)
```

```python
import functools
import math

import jax
import jax.numpy as jnp
from jax import lax
from jax.experimental import pallas as pl
from jax.experimental.pallas import tpu as pltpu

F32 = jnp.float32
BF16 = jnp.bfloat16

EPS = 1e-6
GRID_W = 64
ROPE_THETA = 10000.0
ATTN_HEADS = 16
ATTN_KV_HEADS = 4
ATTN_HEAD_DIM = 128
RET_HEADS = 8
RET_QK_DIM = 256
RET_V_DIM = 512
N_EXPERTS = 16
CAPACITY_FACTOR = 2
N_MOD = 6
HYENA_EMB = 33
HYENA_TARGET = 1e-2
HYENA_FAST = 0.3
HYENA_SLOW = 1.5
HYENA_SHIFT = 0.0

LANE = 128
VMEM_LIMIT = 52 * 1024 * 1024
FFN_VMEM_LIMIT = 58 * 1024 * 1024
FFN_MAX_ROWS = 1152


def _cparams(*sem, vmem=VMEM_LIMIT):
    return pltpu.CompilerParams(dimension_semantics=sem, vmem_limit_bytes=vmem)


def _silu(v):
    return v / (1.0 + jnp.exp(-v))


def _split_bf16(v):
    hi = v.astype(BF16)
    lo = (v - hi.astype(F32)).astype(BF16)
    return hi, lo


def _dot(a, b):
    return jnp.dot(a, b, preferred_element_type=F32)


def _dot3(a, b):
    ah, al = _split_bf16(a)
    bh, bl = _split_bf16(b)
    return _dot(ah, bh) + _dot(al, bh) + _dot(ah, bl)


def _dot_nt(a, b):
    return lax.dot_general(a, b, (((1,), (1,)), ((), ())), preferred_element_type=F32)


def _dot_tn(a, b):
    return lax.dot_general(a, b, (((0,), (0,)), ((), ())), preferred_element_type=F32)


def _mod_kernel(c_ref, w_ref, b_ref, o_ref):
    a = _silu(c_ref[...])
    o_ref[...] = _dot3(a, w_ref[...]) + b_ref[...]


def _modulation(c_all, w_mod, b_mod):
    depth, d, n = w_mod.shape
    rows = c_all.shape[0]
    tn = 512
    return pl.pallas_call(
        _mod_kernel,
        grid=(depth, n // tn),
        in_specs=[
            pl.BlockSpec((rows, d), lambda i, j: (0, 0)),
            pl.BlockSpec((None, d, tn), lambda i, j: (i, 0, j)),
            pl.BlockSpec((None, 1, tn), lambda i, j: (i, 0, j)),
        ],
        out_specs=pl.BlockSpec((None, rows, tn), lambda i, j: (i, 0, j)),
        out_shape=jax.ShapeDtypeStruct((depth, rows, n), F32),
        compiler_params=_cparams("parallel", "parallel"),
        name="modulation",
    )(c_all, w_mod, b_mod.reshape(depth, 1, n))


def _norm_mod(x, nw, sh, sc):
    ms = jnp.mean(x * x, axis=-1, keepdims=True)
    y = x * lax.rsqrt(ms + EPS) * nw
    return y * (1.0 + sc) + sh


def _rope_slices(acc, colw, cos, sin, *, head_norm, half, table_w):
    tn = acc.shape[1]
    lane = lax.broadcasted_iota(jnp.int32, (1, LANE), 1)
    first_half = (lane % (2 * half)) < half
    outs = []
    for s in range(tn // LANE):
        xs = acc[:, s * LANE:(s + 1) * LANE]
        if head_norm:
            xs = xs * lax.rsqrt(jnp.mean(xs * xs, axis=-1, keepdims=True) + EPS)
        xs = xs * colw[:, s * LANE:(s + 1) * LANE]
        t0 = (s * LANE) % table_w
        cs = cos[:, t0:t0 + LANE]
        sn = sin[:, t0:t0 + LANE]
        if 2 * half == LANE:
            partner = pltpu.roll(xs, half, axis=1)
        else:
            partner = jnp.where(first_half, pltpu.roll(xs, LANE - half, axis=1), pltpu.roll(xs, half, axis=1))
        outs.append(xs * cs + partner * sn)
    return jnp.concatenate(outs, axis=1)


def _norm_pass_kernel(x_ref, nw_ref, sh_ref, sc_ref, o_ref):
    o_ref[...] = _norm_mod(x_ref[...], nw_ref[...], sh_ref[...], sc_ref[...]).astype(o_ref.dtype)


def _norm_mod_pass(x, nw, sh, sc, *, tm=512):
    b, l, d = x.shape
    tm = min(tm, l)
    per_sample = sh.shape[0] != 1
    mod_map = (lambda bi, i: (bi, 0, 0)) if per_sample else (lambda bi, i: (0, 0, 0))
    return pl.pallas_call(
        _norm_pass_kernel,
        grid=(b, l // tm),
        in_specs=[pl.BlockSpec((None, tm, d), lambda bi, i: (bi, i, 0)),
                  pl.BlockSpec((1, d), lambda bi, i: (0, 0)),
                  pl.BlockSpec((None, 1, d), mod_map),
                  pl.BlockSpec((None, 1, d), mod_map)],
        out_specs=pl.BlockSpec((None, tm, d), lambda bi, i: (bi, i, 0)),
        out_shape=jax.ShapeDtypeStruct((b, l, d), BF16),
        compiler_params=_cparams("parallel", "parallel"),
        name="norm_modulate",
    )(x, nw.reshape(1, d), sh, sc)


def _proj_kernel(h_ref, w_ref, colw_ref, cos_ref, sin_ref, o_ref, *, rope_blocks, col_off, head_norm, half, table_w):
    j = pl.program_id(2)
    acc = _dot(h_ref[...], w_ref[...].astype(BF16))
    o_ref[...] = acc.astype(o_ref.dtype)
    if rope_blocks:
        @pl.when(j + col_off < rope_blocks)
        def _():
            o_ref[...] = _rope_slices(acc, colw_ref[...], cos_ref[...], sin_ref[...], head_norm=head_norm,
                                      half=half, table_w=table_w).astype(o_ref.dtype)


def _project(x, nw, sh, sc, w, layer, *, colw=None, cos=None, sin=None, rope_blocks=0, col_off=0, head_norm=False,
             half=32, tn=512, tm=2048):
    b, l, d = x.shape
    n = w.shape[2] - col_off * tn
    per_sample = sh.shape[0] != 1
    if colw is None:
        colw = jnp.ones((1, w.shape[2]), F32)
        cos = jnp.ones((l, LANE), F32)
        sin = jnp.zeros((l, LANE), F32)
    if not per_sample and b > 1:
        out = _project(x.reshape(1, b * l, d), nw, sh, sc, w, layer, colw=colw, cos=jnp.tile(cos, (b, 1)),
                       sin=jnp.tile(sin, (b, 1)), rope_blocks=rope_blocks, col_off=col_off, head_norm=head_norm,
                       half=half, tn=tn, tm=tm)
        return out.reshape(b, l, n)
    tm = min(tm, l)
    table_w = cos.shape[1]
    kern = functools.partial(_proj_kernel, rope_blocks=rope_blocks, col_off=col_off, head_norm=head_norm,
                             half=half, table_w=table_w)
    return pl.pallas_call(
        kern,
        grid=(b, l // tm, n // tn),
        in_specs=[
            pl.BlockSpec((None, tm, d), lambda bi, i, j: (bi, i, 0)),
            pl.BlockSpec((None, d, tn), lambda bi, i, j: (layer, 0, j + col_off)),
            pl.BlockSpec((1, tn), lambda bi, i, j: (0, j + col_off)),
            pl.BlockSpec((tm, table_w), lambda bi, i, j: (i, 0)),
            pl.BlockSpec((tm, table_w), lambda bi, i, j: (i, 0)),
        ],
        out_specs=pl.BlockSpec((None, tm, tn), lambda bi, i, j: (bi, i, j)),
        out_shape=jax.ShapeDtypeStruct((b, l, n), BF16),
        compiler_params=_cparams("parallel", "parallel", "parallel"),
        name="project",
    )(_norm_mod_pass(x, nw, sh, sc), w, colw, cos, sin)


def _out_proj_kernel(a_ref, w_ref, r_ref, g_ref, o_ref):
    acc = _dot(a_ref[...], w_ref[...].astype(BF16))
    o_ref[...] = r_ref[...] + g_ref[...] * acc


def _out_project(a, w, layer, resid, gate, *, tm=None, tn=512):
    b, l, k = a.shape
    n = w.shape[2]
    if tm is None:
        tm = (2048 * 2048) // k
    per_sample = gate.shape[0] != 1
    if not per_sample and b > 1:
        out = _out_project(a.reshape(1, b * l, k), w, layer, resid.reshape(1, b * l, n), gate, tm=tm, tn=tn)
        return out.reshape(b, l, n)
    tm = min(tm, l)
    g_map = (lambda bi, i, j: (bi, 0, j)) if per_sample else (lambda bi, i, j: (0, 0, j))
    return pl.pallas_call(
        _out_proj_kernel,
        grid=(b, l // tm, n // tn),
        in_specs=[
            pl.BlockSpec((None, tm, k), lambda bi, i, j: (bi, i, 0)),
            pl.BlockSpec((None, k, tn), lambda bi, i, j: (layer, 0, j)),
            pl.BlockSpec((None, tm, tn), lambda bi, i, j: (bi, i, j)),
            pl.BlockSpec((None, 1, tn), g_map),
        ],
        out_specs=pl.BlockSpec((None, tm, tn), lambda bi, i, j: (bi, i, j)),
        out_shape=jax.ShapeDtypeStruct((b, l, n), F32),
        compiler_params=_cparams("parallel", "parallel", "parallel"),
        name="out_project_residual",
    )(a, w, resid, gate)


def _attn_kernel(q_ref, *refs, groups, n_kv):
    hd = ATTN_HEAD_DIM
    o_ref = refs[2 * n_kv]
    ks = [refs[2 * t][...] for t in range(n_kv)]
    vs = [jnp.concatenate([refs[2 * t + 1][...], jnp.ones((k.shape[0], hd), BF16)], axis=1) for t, k in enumerate(ks)]
    for g in range(groups):
        q = q_ref[:, g * hd:(g + 1) * hd]
        ss = [_dot_nt(q, k) for k in ks]
        m = functools.reduce(jnp.maximum, [jnp.max(s, axis=-1, keepdims=True) for s in ss])
        ps = [jnp.exp(s - m).astype(BF16) for s in ss]
        ov = sum(_dot(p, v) for p, v in zip(ps, vs))
        o_ref[:, g * hd:(g + 1) * hd] = (ov[:, :hd] / ov[:, hd:hd + 1]).astype(o_ref.dtype)


def _attention(pq, kv_sources, *, tq=512):
    b, lq, _ = pq.shape
    hd = ATTN_HEAD_DIM
    groups = ATTN_HEADS // ATTN_KV_HEADS
    gw = groups * hd
    tq = min(tq, lq)
    in_specs = [pl.BlockSpec((None, tq, gw), lambda bi, h, i: (bi, i, h))]
    args = [pq]
    for p, k0, v0 in kv_sources:
        lk = p.shape[1]
        in_specs.append(pl.BlockSpec((None, lk, hd), lambda bi, h, i, o=k0 // hd: (bi, 0, o + h)))
        in_specs.append(pl.BlockSpec((None, lk, hd), lambda bi, h, i, o=v0 // hd: (bi, 0, o + h)))
        args += [p, p]
    return pl.pallas_call(
        functools.partial(_attn_kernel, groups=groups, n_kv=len(kv_sources)),
        grid=(b, ATTN_KV_HEADS, lq // tq),
        in_specs=in_specs,
        out_specs=pl.BlockSpec((None, tq, gw), lambda bi, h, i: (bi, i, h)),
        out_shape=jax.ShapeDtypeStruct((b, lq, ATTN_HEADS * hd), BF16),
        compiler_params=_cparams("parallel", "parallel", "parallel"),
        name="gqa_attention",
    )(*args)


RET_CHUNK = 256


def _ret_readout(o, g):
    of = o * lax.rsqrt(jnp.mean(o * o, axis=-1, keepdims=True) + EPS)
    return (_silu(g.astype(F32)) * of).astype(BF16)


def _ret_kernel(lg_ref, qc_ref, kc_ref, vc_ref, gc_ref, ql_ref, kl_ref, vl_ref, gl_ref, oc_ref, ol_ref,
                of_scr, ob_scr, sf_scr, sb_scr, *, n_chunks, with_ctx_out):
    c = RET_CHUNK
    h = pl.program_id(1)
    lgf = lg_ref[0, h]
    lgb = lg_ref[1, h]
    row = lax.broadcasted_iota(jnp.int32, (c, 1), 0).astype(F32)
    col = lax.broadcasted_iota(jnp.int32, (1, c), 1).astype(F32)
    diff = row - col
    dmask = jnp.exp(jnp.where(diff >= 0, diff * lgf, -diff * lgb))
    qdec_f = jnp.exp((row + 1.0) * lgf)
    kdec_f = jnp.exp((c - 1.0 - row) * lgf)
    qdec_b = jnp.exp((c - row) * lgb)
    kdec_b = jnp.exp(row * lgb)
    one = jnp.ones((1, 1), F32)
    cdec_f = jnp.exp(one * (c * lgf))
    cdec_b = jnp.exp(one * (c * lgb))

    qc = qc_ref[...]
    kc = kc_ref[...].astype(F32)
    vc = vc_ref[...]
    sf_scr[...] = _dot_tn((kc * kdec_f).astype(BF16), vc)
    sb_scr[...] = _dot_tn((kc * kdec_b).astype(BF16), vc)
    if with_ctx_out:
        inner = (_dot_nt(qc, kc_ref[...]) * dmask).astype(BF16)
        oc_ref[...] = _ret_readout(_dot(inner, vc), gc_ref[...])
    else:
        oc_ref[...] = jnp.zeros(oc_ref.shape, oc_ref.dtype)

    def scan(i, carry):
        sl = pl.ds(pl.multiple_of(i * c, c), c)
        qb = ql_ref[sl, :]
        kb = kl_ref[sl, :]
        v = vl_ref[sl, :]
        inner = (_dot_nt(qb, kb) * dmask).astype(BF16)
        of_scr[sl, :] = _dot(inner, v) + _dot((qb.astype(F32) * qdec_f).astype(BF16), sf_scr[...].astype(BF16))
        sf_scr[...] = sf_scr[...] * cdec_f + _dot_tn((kb.astype(F32) * kdec_f).astype(BF16), v)

        sr = pl.ds(pl.multiple_of((n_chunks - 1 - i) * c, c), c)
        q = ql_ref[sr, :].astype(F32)
        k = kl_ref[sr, :].astype(F32)
        ob_scr[sr, :] = _dot((q * qdec_b).astype(BF16), sb_scr[...].astype(BF16))
        sb_scr[...] = sb_scr[...] * cdec_b + _dot_tn((k * kdec_b).astype(BF16), vl_ref[sr, :])
        return carry

    lax.fori_loop(0, n_chunks, scan, 0)

    def readout(i, carry):
        sl = pl.ds(pl.multiple_of(i * c, c), c)
        ol_ref[sl, :] = _ret_readout(of_scr[sl, :] + ob_scr[sl, :], gl_ref[sl, :])
        return carry

    lax.fori_loop(0, n_chunks, readout, 0)


def _retention(log_g, p_ctx, p_lat, with_ctx_out):
    b, l, _ = p_lat.shape
    lc = p_ctx.shape[1]
    assert lc == RET_CHUNK and l % RET_CHUNK == 0
    dk, dv, hh = RET_QK_DIM, RET_V_DIM, RET_HEADS
    k_off = hh * dk // dk
    v_off = 2 * hh * dk // dv
    g_off = v_off + hh

    def specs(ln):
        return [
            pl.BlockSpec((None, ln, dk), lambda bi, h: (bi, 0, h)),
            pl.BlockSpec((None, ln, dk), lambda bi, h: (bi, 0, k_off + h)),
            pl.BlockSpec((None, ln, dv), lambda bi, h: (bi, 0, v_off + h)),
            pl.BlockSpec((None, ln, dv), lambda bi, h: (bi, 0, g_off + h)),
        ]

    return pl.pallas_call(
        functools.partial(_ret_kernel, n_chunks=l // RET_CHUNK, with_ctx_out=with_ctx_out),
        grid=(b, hh),
        in_specs=[pl.BlockSpec(memory_space=pltpu.SMEM)] + specs(lc) + specs(l),
        out_specs=[
            pl.BlockSpec((None, lc, dv), lambda bi, h: (bi, 0, h)),
            pl.BlockSpec((None, l, dv), lambda bi, h: (bi, 0, h)),
        ],
        out_shape=[
            jax.ShapeDtypeStruct((b, lc, hh * dv), BF16),
            jax.ShapeDtypeStruct((b, l, hh * dv), BF16),
        ],
        scratch_shapes=[pltpu.VMEM((l, dv), F32), pltpu.VMEM((l, dv), F32), pltpu.VMEM((dk, dv), F32),
                        pltpu.VMEM((dk, dv), F32)],
        compiler_params=_cparams("parallel", "parallel"),
        name="retention",
    )(log_g, p_ctx, p_ctx, p_ctx, p_ctx, p_lat, p_lat, p_lat, p_lat)


def _hy_gate_kernel(x0_ref, x1_ref, v_ref, cw0_ref, cw1_ref, cwv_ref, cb0_ref, cb1_ref, cbv_ref, u_ref, g_ref):
    l = x0_ref.shape[0]
    t = lax.broadcasted_iota(jnp.int32, (l, 1), 0)

    def conv3(z_ref, cw_ref, cb_ref):
        z = z_ref[...].astype(F32)
        prev = jnp.where(t == 0, 0.0, pltpu.roll(z, 1, axis=0))
        nxt = jnp.where(t == l - 1, 0.0, pltpu.roll(z, l - 1, axis=0))
        cw = cw_ref[...]
        return prev * cw[0:1, :] + z * cw[1:2, :] + nxt * cw[2:3, :] + cb_ref[...]

    x1 = conv3(x1_ref, cw1_ref, cb1_ref)
    v = conv3(v_ref, cwv_ref, cbv_ref)
    u_ref[...] = (v * x1).astype(u_ref.dtype)
    g_ref[...] = conv3(x0_ref, cw0_ref, cb0_ref).astype(g_ref.dtype)


def _hyena_gate(z, conv_w, conv_b, *, tn=256):
    b, l, d3 = z.shape
    d = d3 // 3
    nb = d // tn
    cb = conv_b.reshape(1, d3)
    zs = [pl.BlockSpec((None, l, tn), (lambda bi, j, o=o: (bi, 0, j + o * nb))) for o in range(3)]
    ws = [pl.BlockSpec((3, tn), (lambda bi, j, o=o: (0, j + o * nb))) for o in range(3)]
    bs = [pl.BlockSpec((1, tn), (lambda bi, j, o=o: (0, j + o * nb))) for o in range(3)]
    return pl.pallas_call(
        _hy_gate_kernel,
        grid=(b, nb),
        in_specs=zs + ws + bs,
        out_specs=[pl.BlockSpec((None, l, tn), lambda bi, j: (bi, 0, j))] * 2,
        out_shape=[jax.ShapeDtypeStruct((b, l, d), BF16)] * 2,
        compiler_params=_cparams("parallel", "parallel"),
        name="hyena_conv3_gate",
    )(z, z, z, conv_w, conv_w, conv_w, cb, cb, cb)


def _hy_filter_kernel(z_ref, w1_ref, b1_ref, f1_ref, w2_ref, b2_ref, f2_ref, w3f_ref, w3b_ref, dec_ref, o_ref, h_scr):
    l = z_ref.shape[0]

    @pl.when(pl.program_id(0) == 0)
    def _():
        h1 = jnp.sin(f1_ref[...] * (_dot3(z_ref[...], w1_ref[...]) + b1_ref[...]))
        h_scr[...] = jnp.sin(f2_ref[...] * (_dot3(h1, w2_ref[...]) + b2_ref[...]))

    h = h_scr[...]
    decay = dec_ref[...] + HYENA_SHIFT
    hf = _dot3(h, w3f_ref[...]) * decay
    hb = _dot3(h, w3b_ref[...]) * decay
    t = lax.broadcasted_iota(jnp.int32, (l, 1), 0)
    hb = jnp.where(t == 0, 0.0, hb)
    norm = jnp.sum(jnp.abs(hf), axis=0, keepdims=True) + jnp.sum(jnp.abs(hb), axis=0, keepdims=True)
    o_ref[0] = (hf / norm).astype(o_ref.dtype)
    o_ref[1] = (hb / norm).astype(o_ref.dtype)


def _hyena_filter(l, d, w1, b1, fr1, w2, b2, fr2, w3, *, tn=256):
    t = jnp.linspace(0.0, 1.0, l, dtype=F32)[:, None]
    bands = (HYENA_EMB - 1) // 2
    w = 2.0 * math.pi * jnp.arange(l, dtype=F32)[:, None] / l
    f = jnp.linspace(1e-4, bands - 1, bands, dtype=F32)[None, :]
    z = jnp.concatenate([t, jnp.cos(f * w), -jnp.sin(f * w)], axis=-1)
    deltas = jnp.abs(jnp.linspace(math.log(HYENA_TARGET) / HYENA_SLOW, math.log(HYENA_TARGET) / HYENA_FAST, d, dtype=F32))
    decay = jnp.exp(-t * deltas)
    fw = w1.shape[1]
    pad = lambda a, r, c: jnp.pad(a.astype(F32), ((0, r - a.shape[0]), (0, c - a.shape[1])))
    z = pad(z, l, LANE)
    w1p = pad(w1, LANE, LANE)
    w2p = pad(w2, LANE, LANE)
    w3p = pad(w3, LANE, 2 * d)
    vec = lambda a: pad(a.reshape(1, fw), 1, LANE)
    nb = d // tn
    full = lambda shape: pl.BlockSpec(shape, lambda j: (0, 0))
    return pl.pallas_call(
        _hy_filter_kernel,
        grid=(nb,),
        in_specs=[full((l, LANE)), full((LANE, LANE)), full((1, LANE)), full((1, LANE)), full((LANE, LANE)),
                  full((1, LANE)), full((1, LANE)),
                  pl.BlockSpec((LANE, tn), lambda j: (0, j)),
                  pl.BlockSpec((LANE, tn), lambda j: (0, j + nb)),
                  pl.BlockSpec((l, tn), lambda j: (0, j))],
        out_specs=pl.BlockSpec((2, l, tn), lambda j: (0, 0, j)),
        out_shape=jax.ShapeDtypeStruct((2, l, d), BF16),
        scratch_shapes=[pltpu.VMEM((l, LANE), F32)],
        compiler_params=_cparams("arbitrary"),
        name="hyena_filter",
    )(z, w1p, vec(b1), vec(fr1), w2p, vec(b2), vec(fr2), w3p, w3p, decay)


def _dft_tables(l):
    n = 2 * l
    k = jnp.arange(l, dtype=jnp.int32)
    nb = 1 << ((l.bit_length() - 1 + 1) // 2)
    na = l // nb
    theta = lambda m: (m % n).astype(F32) * (2.0 * math.pi / n)
    ang_a = theta(k[:, None] * (jnp.arange(na, dtype=jnp.int32) * nb)[None, :])[:, :, None]
    ang_b = theta(k[:, None] * jnp.arange(nb, dtype=jnp.int32)[None, :])[:, None, :]
    cs = (jnp.cos(ang_a) * jnp.cos(ang_b) - jnp.sin(ang_a) * jnp.sin(ang_b)).reshape(l, l)
    sn = (jnp.sin(ang_a) * jnp.cos(ang_b) + jnp.cos(ang_a) * jnp.sin(ang_b)).reshape(l, l)
    alt = jnp.where(k % 2 == 0, 1.0, -1.0).astype(F32)
    f_b = jnp.where(k[:, None] == 0, alt[None, :], -sn)
    fwd = jnp.concatenate([cs, f_b], axis=0)
    wa = jnp.where(k[None, :] == 0, 1.0, 2.0) * cs.T
    wb = jnp.where(k[None, :] == 0, alt[:, None], -2.0 * sn.T)
    inv = jnp.concatenate([wa, wb], axis=1) * (1.0 / n)
    return fwd.astype(BF16), inv.astype(BF16)


def _dft_raw_kernel(fa_ref, fb_ref, u_ref, o_ref):
    u = u_ref[...]
    o_ref[0] = _dot(fa_ref[...], u)
    o_ref[1] = _dot(fb_ref[...], u)


def _dft_raw(fwd, u, *, tm=512, tn=512):
    b, l, d = u.shape
    tm, tn = min(tm, l), min(tn, d)
    nb = l // tm
    return pl.pallas_call(
        _dft_raw_kernel,
        grid=(b, d // tn, nb),
        in_specs=[pl.BlockSpec((tm, l), lambda bi, j, i: (i, 0)),
                  pl.BlockSpec((tm, l), lambda bi, j, i: (i + nb, 0)),
                  pl.BlockSpec((None, l, tn), lambda bi, j, i: (bi, 0, j))],
        out_specs=pl.BlockSpec((None, 2, tm, tn), lambda bi, j, i: (bi, 0, i, j)),
        out_shape=jax.ShapeDtypeStruct((b, 2, l, d), F32),
        compiler_params=_cparams("parallel", "parallel", "parallel"),
        name="hyena_dft_filter",
    )(fwd, fwd, u)


def _dft_mul_kernel(fa_ref, fb_ref, u_ref, hs_ref, o_ref):
    i = pl.program_id(0)
    u = u_ref[...]
    ua = _dot(fa_ref[...], u)
    ub = _dot(fb_ref[...], u)
    first = jnp.logical_and(lax.broadcasted_iota(jnp.int32, (ua.shape[0], 1), 0) == 0, i == 0)
    ha = hs_ref[0, 0] + hs_ref[1, 0]
    hb = jnp.where(first, hs_ref[0, 1] + hs_ref[1, 1], hs_ref[0, 1] - hs_ref[1, 1])
    pa = jnp.where(first, ua * ha, ua * ha - ub * hb)
    pb = jnp.where(first, ub * hb, ua * hb + ub * ha)
    o_ref[0] = pa.astype(o_ref.dtype)
    o_ref[1] = pb.astype(o_ref.dtype)


def _dft_mul(fwd, u, hspec, *, tm=512, tn=512):
    b, l, d = u.shape
    tm, tn = min(tm, l), min(tn, d)
    nb = l // tm
    out = pl.pallas_call(
        _dft_mul_kernel,
        grid=(nb, d // tn, b),
        in_specs=[pl.BlockSpec((tm, l), lambda i, j, bi: (i, 0)),
                  pl.BlockSpec((tm, l), lambda i, j, bi: (i + nb, 0)),
                  pl.BlockSpec((None, l, tn), lambda i, j, bi: (bi, 0, j)),
                  pl.BlockSpec((2, 2, tm, tn), lambda i, j, bi: (0, 0, i, j))],
        out_specs=pl.BlockSpec((None, 2, tm, tn), lambda i, j, bi: (bi, 0, i, j)),
        out_shape=jax.ShapeDtypeStruct((b, 2, l, d), BF16),
        compiler_params=_cparams("parallel", "parallel", "parallel"),
        name="hyena_dft_forward",
    )(fwd, fwd, u, hspec)
    return out.reshape(b, 2 * l, d)


def _idft_kernel(g_ref, p_ref, u_ref, x0_ref, skip_ref, o_ref):
    y = _dot(g_ref[...], p_ref[...]) + u_ref[...].astype(F32) * skip_ref[...]
    o_ref[...] = (y * x0_ref[...].astype(F32)).astype(o_ref.dtype)


def _idft_gate(inv, p, u, x0, skip, *, tm=1024, tn=512):
    b, l, d = u.shape
    tm, tn = min(tm, l), min(tn, d)
    return pl.pallas_call(
        _idft_kernel,
        grid=(l // tm, b, d // tn),
        in_specs=[pl.BlockSpec((tm, 2 * l), lambda i, bi, j: (i, 0)),
                  pl.BlockSpec((None, 2 * l, tn), lambda i, bi, j: (bi, 0, j)),
                  pl.BlockSpec((None, tm, tn), lambda i, bi, j: (bi, i, j)),
                  pl.BlockSpec((None, tm, tn), lambda i, bi, j: (bi, i, j)),
                  pl.BlockSpec((1, tn), lambda i, bi, j: (0, j))],
        out_specs=pl.BlockSpec((None, tm, tn), lambda i, bi, j: (bi, i, j)),
        out_shape=jax.ShapeDtypeStruct((b, l, d), BF16),
        compiler_params=_cparams("parallel", "parallel", "parallel"),
        name="hyena_dft_inverse",
    )(inv, p, u, x0, skip.reshape(1, d))


def _hyena_operator(z, conv_w, conv_b, fparams, skip):
    b, l, d3 = z.shape
    d = d3 // 3
    u, x0 = _hyena_gate(z, conv_w, conv_b)
    taps = _hyena_filter(l, d, *fparams)
    fwd, inv = _dft_tables(l)
    hspec = _dft_raw(fwd, taps)
    p = _dft_mul(fwd, u, hspec)
    return _idft_gate(inv, p, u, x0, skip)


def _router_kernel(x_ref, nw_ref, sh_ref, sc_ref, wr_ref, h_ref, lg_ref):
    h = _norm_mod(x_ref[...], nw_ref[...], sh_ref[...], sc_ref[...])
    h_ref[...] = h.astype(BF16)
    lg_ref[...] = _dot3(h, wr_ref[...])


def _norm_mod_router(x, nw, sh, sc, w_router, *, tm=512):
    b, l, d = x.shape
    n_e = w_router.shape[1]
    e = LANE
    w_router = jnp.pad(w_router, ((0, 0), (0, e - n_e)))
    tm = min(tm, l)
    per_sample = sh.shape[0] != 1
    mod_map = (lambda bi, i: (bi, 0, 0)) if per_sample else (lambda bi, i: (0, 0, 0))
    return pl.pallas_call(
        _router_kernel,
        grid=(b, l // tm),
        in_specs=[
            pl.BlockSpec((None, tm, d), lambda bi, i: (bi, i, 0)),
            pl.BlockSpec((1, d), lambda bi, i: (0, 0)),
            pl.BlockSpec((None, 1, d), mod_map),
            pl.BlockSpec((None, 1, d), mod_map),
            pl.BlockSpec((d, e), lambda bi, i: (0, 0)),
        ],
        out_specs=[pl.BlockSpec((None, tm, d), lambda bi, i: (bi, i, 0)),
                   pl.BlockSpec((None, tm, e), lambda bi, i: (bi, i, 0))],
        out_shape=[jax.ShapeDtypeStruct((b, l, d), BF16), jax.ShapeDtypeStruct((b, l, e), F32)],
        compiler_params=_cparams("parallel", "parallel"),
        name="norm_mod_router",
    )(x, nw.reshape(1, d), sh, sc, w_router)


def _route_kernel(lg_ref, tri_ref, pos_ref, prob_ref, *, cap):
    lg = lg_ref[...]
    m = jnp.max(lg, axis=0, keepdims=True)
    ex = jnp.exp(lg - m)
    probs = ex / jnp.sum(ex, axis=0, keepdims=True)
    bits = lax.bitcast_convert_type(probs, jnp.int32)

    def count(mask):
        return jnp.sum(jnp.where(mask, 1.0, 0.0), axis=1, keepdims=True)

    def step(i, thr):
        trial = thr | lax.shift_left(jnp.int32(1), 30 - i)
        return jnp.where(count(bits >= trial) >= cap, trial, thr)

    thr = lax.fori_loop(0, 31, step, jnp.zeros((lg.shape[0], 1), jnp.int32))
    gt = bits > thr
    eq = bits == thr
    need = cap - count(gt).astype(jnp.int32)
    both = jnp.concatenate([jnp.where(gt, 1.0, 0.0), jnp.where(eq, 1.0, 0.0)], axis=0).astype(BF16)
    csum = _dot(both, tri_ref[...])
    e = lg.shape[0]
    rank_gt = csum[:e].astype(jnp.int32)
    rank_eq = csum[e:].astype(jnp.int32)
    sel = jnp.logical_or(gt, jnp.logical_and(eq, rank_eq < need))
    pos = rank_gt + jnp.minimum(rank_eq, need)
    pos_ref[...] = jnp.where(sel, pos, -1)
    prob_ref[...] = probs


def _route(logits_t, cap):
    b, e, n = logits_t.shape
    idx = jnp.arange(n, dtype=jnp.int32)
    tri = (idx[:, None] < idx[None, :]).astype(BF16)
    return pl.pallas_call(
        functools.partial(_route_kernel, cap=cap),
        grid=(b,),
        in_specs=[pl.BlockSpec((None, e, n), lambda bi: (bi, 0, 0)),
                  pl.BlockSpec((n, n), lambda bi: (0, 0))],
        out_specs=[pl.BlockSpec((None, e, n), lambda bi: (bi, 0, 0))] * 2,
        out_shape=[jax.ShapeDtypeStruct((b, e, n), jnp.int32), jax.ShapeDtypeStruct((b, e, n), F32)],
        compiler_params=_cparams("parallel"),
        name="expert_choice_route",
    )(logits_t, tri)


def _gather_kernel(h_ref, pos_ref, prob_ref, xg_ref, gate_ref, *, cap):
    n = h_ref.shape[0]
    slot = lax.broadcasted_iota(jnp.int32, (cap, n), 0)
    match = slot == pos_ref[...]
    onehot = jnp.where(match, 1.0, 0.0).astype(BF16)
    xg_ref[...] = _dot(onehot, h_ref[...]).astype(xg_ref.dtype)
    gate_ref[...] = jnp.sum(jnp.where(match, prob_ref[...], 0.0), axis=1, keepdims=True)


def _gather_append_kernel(h_ref, pos_ref, prob_ref, xin_ref, gin_ref, xg_ref, gate_ref, *, cap, nb):
    bi = pl.program_id(0)

    @pl.when(bi < nb)
    def _():
        _gather_kernel(h_ref, pos_ref, prob_ref, xg_ref, gate_ref, cap=cap)

    @pl.when(bi >= nb)
    def _():
        xg_ref[...] = xin_ref[...]
        gate_ref[...] = gin_ref[...]


def _moe_gather(h, pos, probs, cap, tail=None):
    b, n, d = h.shape
    e = pos.shape[1]
    row = lambda a: a.reshape(b, e, 1, n)
    if tail is None:
        nx, kern, clamp = 0, functools.partial(_gather_kernel, cap=cap), lambda bi: bi
        tail_specs, tail_args = [], []
    else:
        r = tail[0].shape[1]
        assert r % cap == 0
        nx, kern = r // cap, functools.partial(_gather_append_kernel, cap=cap, nb=b)
        clamp = lambda bi: jnp.minimum(bi, b - 1)
        tail_map = lambda bi, ei: (ei, jnp.maximum(bi - b, 0), 0)
        tail_specs = [pl.BlockSpec((None, cap, d), tail_map), pl.BlockSpec((None, cap, 1), tail_map)]
        tail_args = list(tail)
    m_total = (b + nx) * cap
    return pl.pallas_call(
        kern,
        grid=(b + nx, e),
        in_specs=[pl.BlockSpec((None, n, d), lambda bi, ei: (clamp(bi), 0, 0)),
                  pl.BlockSpec((None, None, 1, n), lambda bi, ei: (clamp(bi), ei, 0, 0)),
                  pl.BlockSpec((None, None, 1, n), lambda bi, ei: (clamp(bi), ei, 0, 0))] + tail_specs,
        out_specs=[pl.BlockSpec((None, cap, d), lambda bi, ei: (ei, bi, 0)),
                   pl.BlockSpec((None, cap, 1), lambda bi, ei: (ei, bi, 0))],
        out_shape=[jax.ShapeDtypeStruct((e, m_total, d), BF16), jax.ShapeDtypeStruct((e, m_total, 1), F32)],
        compiler_params=_cparams("parallel", "parallel"),
        name="moe_gather",
    )(h, row(pos), row(probs), *tail_args)


def _ffn_kernel(x_ref, g_ref, wg_ref, wu_ref, wd_ref, o_ref, acc_ref):
    f = pl.program_id(2)
    x = x_ref[...]
    a = _dot(x, wg_ref[...].astype(BF16))
    u = _dot(x, wu_ref[...].astype(BF16))
    hmid = (_silu(a) * u * g_ref[...]).astype(BF16)
    y = _dot(hmid, wd_ref[...].astype(BF16))
    acc = jnp.where(f == 0, y, acc_ref[...] + y)
    acc_ref[...] = acc
    o_ref[...] = acc.astype(o_ref.dtype)


def _moe_ffn(xg, gate, w_gate, w_up, w_down, layer, *, tf=256):
    e, m, d = xg.shape
    ff = w_gate.shape[3]
    tm = max(t for t in range(16, FFN_MAX_ROWS + 1, 16) if m % t == 0)
    return pl.pallas_call(
        _ffn_kernel,
        grid=(e, m // tm, ff // tf),
        in_specs=[pl.BlockSpec((None, tm, d), lambda ei, i, f: (ei, i, 0)),
                  pl.BlockSpec((None, tm, 1), lambda ei, i, f: (ei, i, 0)),
                  pl.BlockSpec((None, None, d, tf), lambda ei, i, f: (layer, ei, 0, f)),
                  pl.BlockSpec((None, None, d, tf), lambda ei, i, f: (layer, ei, 0, f)),
                  pl.BlockSpec((None, None, tf, d), lambda ei, i, f: (layer, ei, f, 0))],
        out_specs=pl.BlockSpec((None, tm, d), lambda ei, i, f: (ei, i, 0)),
        out_shape=jax.ShapeDtypeStruct((e, m, d), BF16),
        scratch_shapes=[pltpu.VMEM((tm, d), F32)],
        compiler_params=_cparams("parallel", "parallel", "arbitrary", vmem=FFN_VMEM_LIMIT),
        name="moe_expert_ffn",
    )(xg, gate, w_gate, w_up, w_down)


def _combine_kernel(post_ref, y_ref, r_ref, g_ref, o_ref, pt_scr, *, cap):
    j = pl.program_id(2)
    tm, e = post_ref.shape

    @pl.when(j == 0)
    def _():
        post = post_ref[...]
        if cap % LANE == 0:
            slot = lax.broadcasted_iota(jnp.int32, (tm, cap), 1)
            for ei in range(e):
                pt_scr[:, ei * cap:(ei + 1) * cap] = jnp.where(post[:, ei:ei + 1] == slot, 1.0, 0.0).astype(BF16)
        else:
            slot = lax.broadcasted_iota(jnp.int32, (tm, e * cap), 1)
            hit = jnp.zeros((tm, e * cap), jnp.bool_)
            for ei in range(e):
                tgt = jnp.where(post[:, ei:ei + 1] >= 0, post[:, ei:ei + 1] + ei * cap, -1)
                hit = jnp.logical_or(hit, tgt == slot)
            pt_scr[...] = jnp.where(hit, 1.0, 0.0).astype(BF16)

    y = y_ref[...].reshape(e * cap, y_ref.shape[2])
    o_ref[...] = r_ref[...] + g_ref[...] * _dot(pt_scr[...], y)


def _moe_combine(pos_t, y, row0, resid, gate, cap, *, tm=1024, tn=512):
    b, n, e = pos_t.shape
    d = y.shape[2]
    tm = min(tm, n)
    blk0 = row0 // cap
    per_sample = gate.shape[0] != 1
    g_map = (lambda bi, i, j: (bi, 0, j)) if per_sample else (lambda bi, i, j: (0, 0, j))
    return pl.pallas_call(
        functools.partial(_combine_kernel, cap=cap),
        grid=(b, n // tm, d // tn),
        in_specs=[pl.BlockSpec((None, tm, e), lambda bi, i, j: (bi, i, 0)),
                  pl.BlockSpec((e, cap, tn), lambda bi, i, j: (0, blk0 + bi, j)),
                  pl.BlockSpec((None, tm, tn), lambda bi, i, j: (bi, i, j)),
                  pl.BlockSpec((None, 1, tn), g_map)],
        out_specs=pl.BlockSpec((None, tm, tn), lambda bi, i, j: (bi, i, j)),
        out_shape=jax.ShapeDtypeStruct((b, n, d), F32),
        scratch_shapes=[pltpu.VMEM((tm, e * cap), BF16)],
        compiler_params=_cparams("parallel", "parallel", "arbitrary"),
        name="moe_combine_residual",
    )(pos_t, y, resid, gate)


def _moe_block(streams, nw, w_router, w_gate, w_up, w_down, layer):
    e = w_router.shape[1]
    caps = [CAPACITY_FACTOR * s[0].shape[1] // e for s in streams]
    rows = [s[0].shape[0] * cap for s, cap in zip(streams, caps)]
    row0 = [sum(rows[:t]) for t in range(len(streams))]
    routed, buffers = [None] * len(streams), None
    for t in reversed(range(len(streams))):
        x, sh, sc, _ = streams[t]
        h, logits = _norm_mod_router(x, nw, sh, sc, w_router)
        pos, probs = _route(jnp.swapaxes(logits[..., :e], 1, 2), caps[t])
        buffers = _moe_gather(h, pos, probs, caps[t], buffers)
        routed[t] = jnp.swapaxes(pos, 1, 2)
    y = _moe_ffn(buffers[0], buffers[1], w_gate, w_up, w_down, layer)
    return [_moe_combine(pos_t, y, r0, x, gate, cap)
            for (x, _, _, gate), pos_t, cap, r0 in zip(streams, routed, caps, row0)]


def _final_norm_kernel(x_ref, w_ref, o_ref):
    x = x_ref[...]
    o_ref[...] = x * lax.rsqrt(jnp.mean(x * x, axis=-1, keepdims=True) + EPS) * w_ref[...]


def _final_norm(x, w, *, tm=512):
    b, l, d = x.shape
    return pl.pallas_call(
        _final_norm_kernel,
        grid=(b, l // tm),
        in_specs=[pl.BlockSpec((None, tm, d), lambda bi, i: (bi, i, 0)), pl.BlockSpec((1, d), lambda bi, i: (0, 0))],
        out_specs=pl.BlockSpec((None, tm, d), lambda bi, i: (bi, i, 0)),
        out_shape=jax.ShapeDtypeStruct(x.shape, F32),
        compiler_params=_cparams("parallel", "parallel"),
        name="final_rmsnorm",
    )(x, w.reshape(1, d))


def _rope_tables(seq_len, dim):
    rows = seq_len // GRID_W
    row_id = jnp.repeat(jnp.arange(rows, dtype=F32), GRID_W)
    col_id = jnp.tile(jnp.arange(GRID_W, dtype=F32), rows)
    nf = dim // 4
    inv = ROPE_THETA ** (-jnp.arange(nf, dtype=F32) / nf)
    ang_r = row_id[:, None] * inv
    ang_c = col_id[:, None] * inv
    cos = jnp.concatenate([jnp.cos(ang_r)] * 2 + [jnp.cos(ang_c)] * 2, axis=-1)
    sin = jnp.concatenate([-jnp.sin(ang_r), jnp.sin(ang_r), -jnp.sin(ang_c), jnp.sin(ang_c)], axis=-1)
    return cos, sin


def _identity_tables(seq_len, dim):
    return jnp.ones((seq_len, dim), F32), jnp.zeros((seq_len, dim), F32)


def _attention_mixer(x, ctx, nw, mods_lat, mods_ctx, w_qkv, q_norm, k_norm, w_o, layer, with_ctx_out):
    hd = ATTN_HEAD_DIM
    nq, nkv = ATTN_HEADS * hd, ATTN_KV_HEADS * hd
    l, lc = x.shape[1], ctx.shape[1]
    tn = 512
    colw = jnp.concatenate([jnp.tile(q_norm.astype(F32) * (hd ** -0.5), ATTN_HEADS),
                            jnp.tile(k_norm.astype(F32), ATTN_KV_HEADS), jnp.ones((nkv,), F32)]).reshape(1, -1)
    rope_blocks = (nq + nkv) // tn
    cos, sin = _rope_tables(l, hd)
    icos, isin = _identity_tables(lc, hd)
    kw = dict(colw=colw, rope_blocks=rope_blocks, head_norm=True, half=hd // 4, tn=tn)
    p_lat = _project(x, nw, mods_lat[0], mods_lat[1], w_qkv, layer, cos=cos, sin=sin, **kw)
    off = 0 if with_ctx_out else nq // tn
    p_ctx = _project(ctx, nw, mods_ctx[0], mods_ctx[1], w_qkv, layer, cos=icos, sin=isin, col_off=off, **kw)
    kc0 = nq - off * tn
    ctx_kv = (p_ctx, kc0, kc0 + nkv)
    o_lat = _attention(p_lat, [ctx_kv, (p_lat, nq, nq + nkv)])
    x = _out_project(o_lat, w_o, layer, x, mods_lat[2])
    if with_ctx_out:
        ctx = _out_project(_attention(p_ctx, [ctx_kv]), w_o, layer, ctx, mods_ctx[2])
    return x, ctx


def _retention_mixer(x, ctx, nw, mods_lat, mods_ctx, w_in, decay_logit, w_o, layer, with_ctx_out):
    dk, hh = RET_QK_DIM, RET_HEADS
    l, lc = x.shape[1], ctx.shape[1]
    tn = 512
    log_g = jax.nn.log_sigmoid(decay_logit.astype(F32))
    colw = jnp.concatenate([jnp.ones((hh * dk,), F32), jnp.full((hh * dk,), dk ** -0.5, F32),
                            jnp.ones((w_in.shape[2] - 2 * hh * dk,), F32)]).reshape(1, -1)
    rope_blocks = 2 * hh * dk // tn
    cos, sin = _rope_tables(l, dk)
    icos, isin = _identity_tables(lc, dk)
    kw = dict(colw=colw, rope_blocks=rope_blocks, head_norm=False, half=dk // 4, tn=tn)
    p_lat = _project(x, nw, mods_lat[0], mods_lat[1], w_in, layer, cos=cos, sin=sin, **kw)
    p_ctx = _project(ctx, nw, mods_ctx[0], mods_ctx[1], w_in, layer, cos=icos, sin=isin, **kw)
    r_ctx, r_lat = _retention(log_g, p_ctx, p_lat, with_ctx_out)
    x = _out_project(r_lat, w_o, layer, x, mods_lat[2])
    if with_ctx_out:
        ctx = _out_project(r_ctx, w_o, layer, ctx, mods_ctx[2])
    return x, ctx


def _hyena_mixer(x, ctx, nw, mods_lat, mods_ctx, w_in, conv_w, conv_b, fparams, skip, w_out, layer, with_ctx_out):
    z = _project(x, nw, mods_lat[0], mods_lat[1], w_in, layer)
    x = _out_project(_hyena_operator(z, conv_w, conv_b, fparams, skip), w_out, layer, x, mods_lat[2])
    if with_ctx_out:
        zc = _project(ctx, nw, mods_ctx[0], mods_ctx[1], w_in, layer)
        ctx = _out_project(_hyena_operator(zc, conv_w, conv_b, fparams, skip), w_out, layer, ctx, mods_ctx[2])
    return x, ctx


def kernel(x, c, ctx, c_ctx, w_mod, b_mod, norm_w, attn_w_qkv, attn_q_norm, attn_k_norm, attn_w_o, ret_w_in, ret_decay_logit, ret_w_o, hy_w_in, hy_conv_w, hy_conv_b, hy_f_w1, hy_f_b1, hy_f_freq1, hy_f_w2, hy_f_b2, hy_f_freq2, hy_f_w3, hy_skip, hy_w_out, moe_router, moe_w_gate, moe_w_up, moe_w_down, final_norm_w):
    depth = w_mod.shape[0]
    b, _, d = x.shape
    rows = -(-(b + 1) // 8) * 8
    c_all = jnp.concatenate([c, c_ctx[None, :], jnp.zeros((rows - b - 1, d), F32)], axis=0)
    mod = _modulation(c_all, w_mod, b_mod)

    for i in range(depth):
        kind, j = i % 3, i // 3
        with_ctx = i < depth - 1
        m = mod[i].reshape(rows, N_MOD, d)
        mods_lat = [m[:b, t][:, None, :] for t in range(N_MOD)]
        mods_ctx = [m[b:b + 1, t][:, None, :] for t in range(N_MOD)]
        nw1, nw2 = norm_w[i, 0], norm_w[i, 1]
        if kind == 0:
            x, ctx = _attention_mixer(x, ctx, nw1, mods_lat, mods_ctx, attn_w_qkv, attn_q_norm[j], attn_k_norm[j],
                                      attn_w_o, j, with_ctx)
        elif kind == 1:
            x, ctx = _retention_mixer(x, ctx, nw1, mods_lat, mods_ctx, ret_w_in, ret_decay_logit[j], ret_w_o,
                                      j, with_ctx)
        else:
            fparams = (hy_f_w1[j], hy_f_b1[j], hy_f_freq1[j], hy_f_w2[j], hy_f_b2[j], hy_f_freq2[j], hy_f_w3[j])
            x, ctx = _hyena_mixer(x, ctx, nw1, mods_lat, mods_ctx, hy_w_in, hy_conv_w[j], hy_conv_b[j], fparams,
                                  hy_skip[j], hy_w_out, j, with_ctx)
        streams = [(x, mods_lat[3], mods_lat[4], mods_lat[5])]
        if with_ctx:
            streams.append((ctx, mods_ctx[3], mods_ctx[4], mods_ctx[5]))
        outs = _moe_block(streams, nw2, moe_router[i], moe_w_gate, moe_w_up, moe_w_down, i)
        x = outs[0]
        if with_ctx:
            ctx = outs[1]
    return _final_norm(x, final_norm_w)
```

```python
import functools
import math

import jax
import jax.numpy as jnp
from jax import lax
from jax.experimental import pallas as pl
from jax.experimental.pallas import tpu as pltpu

F32 = jnp.float32
BF16 = jnp.bfloat16

EPS = 1e-6
GRID_W = 64
ROPE_THETA = 10000.0
ATTN_HEADS = 16
ATTN_KV_HEADS = 4
ATTN_HEAD_DIM = 128
RET_HEADS = 8
RET_QK_DIM = 256
RET_V_DIM = 512
N_EXPERTS = 16
CAPACITY_FACTOR = 2
N_MOD = 6
HYENA_EMB = 33
HYENA_TARGET = 1e-2
HYENA_FAST = 0.3
HYENA_SLOW = 1.5
HYENA_SHIFT = 0.0

LANE = 128
VMEM_LIMIT = 52 * 1024 * 1024
FFN_VMEM_LIMIT = 58 * 1024 * 1024
FFN_MAX_ROWS = 1152


def _cparams(*sem, vmem=VMEM_LIMIT):
    return pltpu.CompilerParams(dimension_semantics=sem, vmem_limit_bytes=vmem)


def _silu(v):
    return v / (1.0 + jnp.exp(-v))


def _split_bf16(v):
    hi = v.astype(BF16)
    lo = (v - hi.astype(F32)).astype(BF16)
    return hi, lo


def _dot(a, b):
    return jnp.dot(a, b, preferred_element_type=F32)


def _dot3(a, b):
    ah, al = _split_bf16(a)
    bh, bl = _split_bf16(b)
    return _dot(ah, bh) + _dot(al, bh) + _dot(ah, bl)


def _dot_nt(a, b):
    return lax.dot_general(a, b, (((1,), (1,)), ((), ())), preferred_element_type=F32)


def _dot_tn(a, b):
    return lax.dot_general(a, b, (((0,), (0,)), ((), ())), preferred_element_type=F32)


def _mod_kernel(c_ref, w_ref, b_ref, o_ref):
    a = _silu(c_ref[...])
    o_ref[...] = _dot3(a, w_ref[...]) + b_ref[...]


def _modulation(c_all, w_mod, b_mod):
    depth, d, n = w_mod.shape
    rows = c_all.shape[0]
    tn = 1024
    return pl.pallas_call(
        _mod_kernel,
        grid=(depth, n // tn),
        in_specs=[
            pl.BlockSpec((rows, d), lambda i, j: (0, 0)),
            pl.BlockSpec((None, d, tn), lambda i, j: (i, 0, j)),
            pl.BlockSpec((None, 1, tn), lambda i, j: (i, 0, j)),
        ],
        out_specs=pl.BlockSpec((None, rows, tn), lambda i, j: (i, 0, j)),
        out_shape=jax.ShapeDtypeStruct((depth, rows, n), F32),
        compiler_params=_cparams("parallel", "parallel"),
        name="modulation",
    )(c_all, w_mod, b_mod.reshape(depth, 1, n))


def _norm_mod(x, nw, sh, sc):
    ms = jnp.mean(x * x, axis=-1, keepdims=True)
    y = x * lax.rsqrt(ms + EPS) * nw
    return y * (1.0 + sc) + sh


def _rope_slices(acc, colw, cos, sin, *, head_norm, half, table_w):
    tn = acc.shape[1]
    lane = lax.broadcasted_iota(jnp.int32, (1, LANE), 1)
    first_half = (lane % (2 * half)) < half
    outs = []
    for s in range(tn // LANE):
        xs = acc[:, s * LANE:(s + 1) * LANE]
        if head_norm:
            xs = xs * lax.rsqrt(jnp.mean(xs * xs, axis=-1, keepdims=True) + EPS)
        xs = xs * colw[:, s * LANE:(s + 1) * LANE]
        t0 = (s * LANE) % table_w
        cs = cos[:, t0:t0 + LANE]
        sn = sin[:, t0:t0 + LANE]
        if 2 * half == LANE:
            partner = pltpu.roll(xs, half, axis=1)
        else:
            partner = jnp.where(first_half, pltpu.roll(xs, LANE - half, axis=1), pltpu.roll(xs, half, axis=1))
        outs.append(xs * cs + partner * sn)
    return jnp.concatenate(outs, axis=1)


def _norm_pass_kernel(x_ref, nw_ref, sh_ref, sc_ref, o_ref):
    o_ref[...] = _norm_mod(x_ref[...], nw_ref[...], sh_ref[...], sc_ref[...]).astype(o_ref.dtype)


def _norm_mod_pass(x, nw, sh, sc, *, tm=512):
    b, l, d = x.shape
    tm = min(tm, l)
    per_sample = sh.shape[0] != 1
    mod_map = (lambda bi, i: (bi, 0, 0)) if per_sample else (lambda bi, i: (0, 0, 0))
    return pl.pallas_call(
        _norm_pass_kernel,
        grid=(b, l // tm),
        in_specs=[pl.BlockSpec((None, tm, d), lambda bi, i: (bi, i, 0)),
                  pl.BlockSpec((1, d), lambda bi, i: (0, 0)),
                  pl.BlockSpec((None, 1, d), mod_map),
                  pl.BlockSpec((None, 1, d), mod_map)],
        out_specs=pl.BlockSpec((None, tm, d), lambda bi, i: (bi, i, 0)),
        out_shape=jax.ShapeDtypeStruct((b, l, d), BF16),
        compiler_params=_cparams("parallel", "parallel"),
        name="norm_modulate",
    )(x, nw.reshape(1, d), sh, sc)


def _proj_kernel(h_ref, w_ref, colw_ref, cos_ref, sin_ref, o_ref, *, rope_blocks, col_off, head_norm, half, table_w):
    j = pl.program_id(2)
    acc = _dot(h_ref[...], w_ref[...].astype(BF16))
    o_ref[...] = acc.astype(o_ref.dtype)
    if rope_blocks:
        @pl.when(j + col_off < rope_blocks)
        def _():
            o_ref[...] = _rope_slices(acc, colw_ref[...], cos_ref[...], sin_ref[...], head_norm=head_norm,
                                      half=half, table_w=table_w).astype(o_ref.dtype)


def _project(x, nw, sh, sc, w, layer, *, colw=None, cos=None, sin=None, rope_blocks=0, col_off=0, head_norm=False,
             half=32, tn=512, tm=2048):
    b, l, d = x.shape
    n = w.shape[2] - col_off * tn
    per_sample = sh.shape[0] != 1
    if colw is None:
        colw = jnp.ones((1, w.shape[2]), F32)
        cos = jnp.ones((l, LANE), F32)
        sin = jnp.zeros((l, LANE), F32)
    if not per_sample and b > 1:
        out = _project(x.reshape(1, b * l, d), nw, sh, sc, w, layer, colw=colw, cos=jnp.tile(cos, (b, 1)),
                       sin=jnp.tile(sin, (b, 1)), rope_blocks=rope_blocks, col_off=col_off, head_norm=head_norm,
                       half=half, tn=tn, tm=tm)
        return out.reshape(b, l, n)
    tm = min(tm, l)
    table_w = cos.shape[1]
    kern = functools.partial(_proj_kernel, rope_blocks=rope_blocks, col_off=col_off, head_norm=head_norm,
                             half=half, table_w=table_w)
    return pl.pallas_call(
        kern,
        grid=(b, l // tm, n // tn),
        in_specs=[
            pl.BlockSpec((None, tm, d), lambda bi, i, j: (bi, i, 0)),
            pl.BlockSpec((None, d, tn), lambda bi, i, j: (layer, 0, j + col_off)),
            pl.BlockSpec((1, tn), lambda bi, i, j: (0, j + col_off)),
            pl.BlockSpec((tm, table_w), lambda bi, i, j: (i, 0)),
            pl.BlockSpec((tm, table_w), lambda bi, i, j: (i, 0)),
        ],
        out_specs=pl.BlockSpec((None, tm, tn), lambda bi, i, j: (bi, i, j)),
        out_shape=jax.ShapeDtypeStruct((b, l, n), BF16),
        compiler_params=_cparams("parallel", "parallel", "parallel"),
        name="project",
    )(_norm_mod_pass(x, nw, sh, sc), w, colw, cos, sin)


def _out_proj_kernel(a_ref, w_ref, r_ref, g_ref, o_ref):
    acc = _dot(a_ref[...], w_ref[...].astype(BF16))
    o_ref[...] = r_ref[...] + g_ref[...] * acc


def _out_project(a, w, layer, resid, gate, *, tm=None, tn=512):
    b, l, k = a.shape
    n = w.shape[2]
    if tm is None:
        tm = (2048 * 2048) // k
    per_sample = gate.shape[0] != 1
    if not per_sample and b > 1:
        out = _out_project(a.reshape(1, b * l, k), w, layer, resid.reshape(1, b * l, n), gate, tm=tm, tn=tn)
        return out.reshape(b, l, n)
    tm = min(tm, l)
    g_map = (lambda bi, i, j: (bi, 0, j)) if per_sample else (lambda bi, i, j: (0, 0, j))
    return pl.pallas_call(
        _out_proj_kernel,
        grid=(b, l // tm, n // tn),
        in_specs=[
            pl.BlockSpec((None, tm, k), lambda bi, i, j: (bi, i, 0)),
            pl.BlockSpec((None, k, tn), lambda bi, i, j: (layer, 0, j)),
            pl.BlockSpec((None, tm, tn), lambda bi, i, j: (bi, i, j)),
            pl.BlockSpec((None, 1, tn), g_map),
        ],
        out_specs=pl.BlockSpec((None, tm, tn), lambda bi, i, j: (bi, i, j)),
        out_shape=jax.ShapeDtypeStruct((b, l, n), F32),
        compiler_params=_cparams("parallel", "parallel", "parallel"),
        name="out_project_residual",
    )(a, w, resid, gate)


def _attn_kernel(q_ref, *refs, groups, n_kv):
    hd = ATTN_HEAD_DIM
    o_ref = refs[2 * n_kv]
    ks = [refs[2 * t][...] for t in range(n_kv)]
    vs = [jnp.concatenate([refs[2 * t + 1][...], jnp.ones((k.shape[0], hd), BF16)], axis=1) for t, k in enumerate(ks)]
    for g in range(groups):
        q = q_ref[:, g * hd:(g + 1) * hd]
        ss = [_dot_nt(q, k) for k in ks]
        m = functools.reduce(jnp.maximum, [jnp.max(s, axis=-1, keepdims=True) for s in ss])
        ps = [jnp.exp(s - m).astype(BF16) for s in ss]
        ov = sum(_dot(p, v) for p, v in zip(ps, vs))
        o_ref[:, g * hd:(g + 1) * hd] = (ov[:, :hd] / ov[:, hd:hd + 1]).astype(o_ref.dtype)


def _attention(pq, kv_sources, *, tq=512):
    b, lq, _ = pq.shape
    hd = ATTN_HEAD_DIM
    groups = ATTN_HEADS // ATTN_KV_HEADS
    gw = groups * hd
    tq = min(tq, lq)
    in_specs = [pl.BlockSpec((None, tq, gw), lambda bi, h, i: (bi, i, h))]
    args = [pq]
    for p, k0, v0 in kv_sources:
        lk = p.shape[1]
        in_specs.append(pl.BlockSpec((None, lk, hd), lambda bi, h, i, o=k0 // hd: (bi, 0, o + h)))
        in_specs.append(pl.BlockSpec((None, lk, hd), lambda bi, h, i, o=v0 // hd: (bi, 0, o + h)))
        args += [p, p]
    return pl.pallas_call(
        functools.partial(_attn_kernel, groups=groups, n_kv=len(kv_sources)),
        grid=(b, ATTN_KV_HEADS, lq // tq),
        in_specs=in_specs,
        out_specs=pl.BlockSpec((None, tq, gw), lambda bi, h, i: (bi, i, h)),
        out_shape=jax.ShapeDtypeStruct((b, lq, ATTN_HEADS * hd), BF16),
        compiler_params=_cparams("parallel", "parallel", "parallel"),
        name="gqa_attention",
    )(*args)


RET_CHUNK = 256


def _ret_readout(o, g):
    of = o * lax.rsqrt(jnp.mean(o * o, axis=-1, keepdims=True) + EPS)
    return (_silu(g.astype(F32)) * of).astype(BF16)


def _ret_kernel(lg_ref, qc_ref, kc_ref, vc_ref, gc_ref, ql_ref, kl_ref, vl_ref, gl_ref, oc_ref, ol_ref,
                of_scr, ob_scr, sf_scr, sb_scr, *, n_chunks, with_ctx_out):
    c = RET_CHUNK
    h = pl.program_id(1)
    lgf = lg_ref[0, h]
    lgb = lg_ref[1, h]
    row = lax.broadcasted_iota(jnp.int32, (c, 1), 0).astype(F32)
    col = lax.broadcasted_iota(jnp.int32, (1, c), 1).astype(F32)
    diff = row - col
    dmask = jnp.exp(jnp.where(diff >= 0, diff * lgf, -diff * lgb))
    qdec_f = jnp.exp((row + 1.0) * lgf)
    kdec_f = jnp.exp((c - 1.0 - row) * lgf)
    qdec_b = jnp.exp((c - row) * lgb)
    kdec_b = jnp.exp(row * lgb)
    one = jnp.ones((1, 1), F32)
    cdec_f = jnp.exp(one * (c * lgf))
    cdec_b = jnp.exp(one * (c * lgb))

    qc = qc_ref[...]
    kc = kc_ref[...].astype(F32)
    vc = vc_ref[...]
    sf_scr[...] = _dot_tn((kc * kdec_f).astype(BF16), vc)
    sb_scr[...] = _dot_tn((kc * kdec_b).astype(BF16), vc)
    if with_ctx_out:
        inner = (_dot_nt(qc, kc_ref[...]) * dmask).astype(BF16)
        oc_ref[...] = _ret_readout(_dot(inner, vc), gc_ref[...])
    else:
        oc_ref[...] = jnp.zeros(oc_ref.shape, oc_ref.dtype)

    def scan(i, carry):
        sl = pl.ds(pl.multiple_of(i * c, c), c)
        qb = ql_ref[sl, :]
        kb = kl_ref[sl, :]
        v = vl_ref[sl, :]
        inner = (_dot_nt(qb, kb) * dmask).astype(BF16)
        of_scr[sl, :] = _dot(inner, v) + _dot((qb.astype(F32) * qdec_f).astype(BF16), sf_scr[...].astype(BF16))
        sf_scr[...] = sf_scr[...] * cdec_f + _dot_tn((kb.astype(F32) * kdec_f).astype(BF16), v)

        sr = pl.ds(pl.multiple_of((n_chunks - 1 - i) * c, c), c)
        q = ql_ref[sr, :].astype(F32)
        k = kl_ref[sr, :].astype(F32)
        ob_scr[sr, :] = _dot((q * qdec_b).astype(BF16), sb_scr[...].astype(BF16))
        sb_scr[...] = sb_scr[...] * cdec_b + _dot_tn((k * kdec_b).astype(BF16), vl_ref[sr, :])
        return carry

    lax.fori_loop(0, n_chunks, scan, 0)

    def readout(i, carry):
        sl = pl.ds(pl.multiple_of(i * c, c), c)
        ol_ref[sl, :] = _ret_readout(of_scr[sl, :] + ob_scr[sl, :], gl_ref[sl, :])
        return carry

    lax.fori_loop(0, n_chunks, readout, 0)


def _retention(log_g, p_ctx, p_lat, with_ctx_out):
    b, l, _ = p_lat.shape
    lc = p_ctx.shape[1]
    assert lc == RET_CHUNK and l % RET_CHUNK == 0
    dk, dv, hh = RET_QK_DIM, RET_V_DIM, RET_HEADS
    k_off = hh * dk // dk
    v_off = 2 * hh * dk // dv
    g_off = v_off + hh

    def specs(ln):
        return [
            pl.BlockSpec((None, ln, dk), lambda bi, h: (bi, 0, h)),
            pl.BlockSpec((None, ln, dk), lambda bi, h: (bi, 0, k_off + h)),
            pl.BlockSpec((None, ln, dv), lambda bi, h: (bi, 0, v_off + h)),
            pl.BlockSpec((None, ln, dv), lambda bi, h: (bi, 0, g_off + h)),
        ]

    return pl.pallas_call(
        functools.partial(_ret_kernel, n_chunks=l // RET_CHUNK, with_ctx_out=with_ctx_out),
        grid=(b, hh),
        in_specs=[pl.BlockSpec(memory_space=pltpu.SMEM)] + specs(lc) + specs(l),
        out_specs=[
            pl.BlockSpec((None, lc, dv), lambda bi, h: (bi, 0, h)),
            pl.BlockSpec((None, l, dv), lambda bi, h: (bi, 0, h)),
        ],
        out_shape=[
            jax.ShapeDtypeStruct((b, lc, hh * dv), BF16),
            jax.ShapeDtypeStruct((b, l, hh * dv), BF16),
        ],
        scratch_shapes=[pltpu.VMEM((l, dv), F32), pltpu.VMEM((l, dv), F32), pltpu.VMEM((dk, dv), F32),
                        pltpu.VMEM((dk, dv), F32)],
        compiler_params=_cparams("parallel", "parallel"),
        name="retention",
    )(log_g, p_ctx, p_ctx, p_ctx, p_ctx, p_lat, p_lat, p_lat, p_lat)


def _hy_gate_kernel(x0_ref, x1_ref, v_ref, cw0_ref, cw1_ref, cwv_ref, cb0_ref, cb1_ref, cbv_ref, u_ref, g_ref):
    l = x0_ref.shape[0]
    t = lax.broadcasted_iota(jnp.int32, (l, 1), 0)

    def conv3(z_ref, cw_ref, cb_ref):
        z = z_ref[...].astype(F32)
        prev = jnp.where(t == 0, 0.0, pltpu.roll(z, 1, axis=0))
        nxt = jnp.where(t == l - 1, 0.0, pltpu.roll(z, l - 1, axis=0))
        cw = cw_ref[...]
        return prev * cw[0:1, :] + z * cw[1:2, :] + nxt * cw[2:3, :] + cb_ref[...]

    x1 = conv3(x1_ref, cw1_ref, cb1_ref)
    v = conv3(v_ref, cwv_ref, cbv_ref)
    u_ref[...] = (v * x1).astype(u_ref.dtype)
    g_ref[...] = conv3(x0_ref, cw0_ref, cb0_ref).astype(g_ref.dtype)


def _hyena_gate(z, conv_w, conv_b, *, tn=256):
    b, l, d3 = z.shape
    d = d3 // 3
    nb = d // tn
    cb = conv_b.reshape(1, d3)
    zs = [pl.BlockSpec((None, l, tn), (lambda bi, j, o=o: (bi, 0, j + o * nb))) for o in range(3)]
    ws = [pl.BlockSpec((3, tn), (lambda bi, j, o=o: (0, j + o * nb))) for o in range(3)]
    bs = [pl.BlockSpec((1, tn), (lambda bi, j, o=o: (0, j + o * nb))) for o in range(3)]
    return pl.pallas_call(
        _hy_gate_kernel,
        grid=(b, nb),
        in_specs=zs + ws + bs,
        out_specs=[pl.BlockSpec((None, l, tn), lambda bi, j: (bi, 0, j))] * 2,
        out_shape=[jax.ShapeDtypeStruct((b, l, d), BF16)] * 2,
        compiler_params=_cparams("parallel", "parallel"),
        name="hyena_conv3_gate",
    )(z, z, z, conv_w, conv_w, conv_w, cb, cb, cb)


def _hy_filter_kernel(z_ref, w1_ref, b1_ref, f1_ref, w2_ref, b2_ref, f2_ref, w3f_ref, w3b_ref, dec_ref, o_ref, h_scr):
    l = z_ref.shape[0]

    @pl.when(pl.program_id(0) == 0)
    def _():
        h1 = jnp.sin(f1_ref[...] * (_dot3(z_ref[...], w1_ref[...]) + b1_ref[...]))
        h_scr[...] = jnp.sin(f2_ref[...] * (_dot3(h1, w2_ref[...]) + b2_ref[...]))

    h = h_scr[...]
    decay = dec_ref[...] + HYENA_SHIFT
    hf = _dot3(h, w3f_ref[...]) * decay
    hb = _dot3(h, w3b_ref[...]) * decay
    t = lax.broadcasted_iota(jnp.int32, (l, 1), 0)
    hb = jnp.where(t == 0, 0.0, hb)
    norm = jnp.sum(jnp.abs(hf), axis=0, keepdims=True) + jnp.sum(jnp.abs(hb), axis=0, keepdims=True)
    o_ref[0] = (hf / norm).astype(o_ref.dtype)
    o_ref[1] = (hb / norm).astype(o_ref.dtype)


def _hyena_filter(l, d, w1, b1, fr1, w2, b2, fr2, w3, *, tn=256):
    t = jnp.linspace(0.0, 1.0, l, dtype=F32)[:, None]
    bands = (HYENA_EMB - 1) // 2
    w = 2.0 * math.pi * jnp.arange(l, dtype=F32)[:, None] / l
    f = jnp.linspace(1e-4, bands - 1, bands, dtype=F32)[None, :]
    z = jnp.concatenate([t, jnp.cos(f * w), -jnp.sin(f * w)], axis=-1)
    deltas = jnp.abs(jnp.linspace(math.log(HYENA_TARGET) / HYENA_SLOW, math.log(HYENA_TARGET) / HYENA_FAST, d, dtype=F32))
    decay = jnp.exp(-t * deltas)
    fw = w1.shape[1]
    pad = lambda a, r, c: jnp.pad(a.astype(F32), ((0, r - a.shape[0]), (0, c - a.shape[1])))
    z = pad(z, l, LANE)
    w1p = pad(w1, LANE, LANE)
    w2p = pad(w2, LANE, LANE)
    w3p = pad(w3, LANE, 2 * d)
    vec = lambda a: pad(a.reshape(1, fw), 1, LANE)
    nb = d // tn
    full = lambda shape: pl.BlockSpec(shape, lambda j: (0, 0))
    return pl.pallas_call(
        _hy_filter_kernel,
        grid=(nb,),
        in_specs=[full((l, LANE)), full((LANE, LANE)), full((1, LANE)), full((1, LANE)), full((LANE, LANE)),
                  full((1, LANE)), full((1, LANE)),
                  pl.BlockSpec((LANE, tn), lambda j: (0, j)),
                  pl.BlockSpec((LANE, tn), lambda j: (0, j + nb)),
                  pl.BlockSpec((l, tn), lambda j: (0, j))],
        out_specs=pl.BlockSpec((2, l, tn), lambda j: (0, 0, j)),
        out_shape=jax.ShapeDtypeStruct((2, l, d), BF16),
        scratch_shapes=[pltpu.VMEM((l, LANE), F32)],
        compiler_params=_cparams("arbitrary"),
        name="hyena_filter",
    )(z, w1p, vec(b1), vec(fr1), w2p, vec(b2), vec(fr2), w3p, w3p, decay)


def _dft_tables(l):
    n = 2 * l
    k = jnp.arange(l, dtype=jnp.int32)
    nb = 1 << ((l.bit_length() - 1 + 1) // 2)
    na = l // nb
    theta = lambda m: (m % n).astype(F32) * (2.0 * math.pi / n)
    ang_a = theta(k[:, None] * (jnp.arange(na, dtype=jnp.int32) * nb)[None, :])[:, :, None]
    ang_b = theta(k[:, None] * jnp.arange(nb, dtype=jnp.int32)[None, :])[:, None, :]
    cs = (jnp.cos(ang_a) * jnp.cos(ang_b) - jnp.sin(ang_a) * jnp.sin(ang_b)).reshape(l, l)
    sn = (jnp.sin(ang_a) * jnp.cos(ang_b) + jnp.cos(ang_a) * jnp.sin(ang_b)).reshape(l, l)
    alt = jnp.where(k % 2 == 0, 1.0, -1.0).astype(F32)
    f_b = jnp.where(k[:, None] == 0, alt[None, :], -sn)
    fwd = jnp.concatenate([cs, f_b], axis=0)
    wa = jnp.where(k[None, :] == 0, 1.0, 2.0) * cs.T
    wb = jnp.where(k[None, :] == 0, alt[:, None], -2.0 * sn.T)
    inv = jnp.concatenate([wa, wb], axis=1) * (1.0 / n)
    return fwd.astype(BF16), inv.astype(BF16)


def _dft_raw_kernel(fa_ref, fb_ref, u_ref, o_ref):
    u = u_ref[...]
    o_ref[0] = _dot(fa_ref[...], u)
    o_ref[1] = _dot(fb_ref[...], u)


def _dft_raw(fwd, u, *, tm=512, tn=512):
    b, l, d = u.shape
    tm, tn = min(tm, l), min(tn, d)
    nb = l // tm
    return pl.pallas_call(
        _dft_raw_kernel,
        grid=(b, d // tn, nb),
        in_specs=[pl.BlockSpec((tm, l), lambda bi, j, i: (i, 0)),
                  pl.BlockSpec((tm, l), lambda bi, j, i: (i + nb, 0)),
                  pl.BlockSpec((None, l, tn), lambda bi, j, i: (bi, 0, j))],
        out_specs=pl.BlockSpec((None, 2, tm, tn), lambda bi, j, i: (bi, 0, i, j)),
        out_shape=jax.ShapeDtypeStruct((b, 2, l, d), F32),
        compiler_params=_cparams("parallel", "parallel", "parallel"),
        name="hyena_dft_filter",
    )(fwd, fwd, u)


def _dft_mul_kernel(fa_ref, fb_ref, u_ref, hs_ref, o_ref):
    i = pl.program_id(0)
    u = u_ref[...]
    ua = _dot(fa_ref[...], u)
    ub = _dot(fb_ref[...], u)
    first = jnp.logical_and(lax.broadcasted_iota(jnp.int32, (ua.shape[0], 1), 0) == 0, i == 0)
    ha = hs_ref[0, 0] + hs_ref[1, 0]
    hb = jnp.where(first, hs_ref[0, 1] + hs_ref[1, 1], hs_ref[0, 1] - hs_ref[1, 1])
    pa = jnp.where(first, ua * ha, ua * ha - ub * hb)
    pb = jnp.where(first, ub * hb, ua * hb + ub * ha)
    o_ref[0] = pa.astype(o_ref.dtype)
    o_ref[1] = pb.astype(o_ref.dtype)


def _dft_mul(fwd, u, hspec, *, tm=512, tn=512):
    b, l, d = u.shape
    tm, tn = min(tm, l), min(tn, d)
    nb = l // tm
    out = pl.pallas_call(
        _dft_mul_kernel,
        grid=(nb, d // tn, b),
        in_specs=[pl.BlockSpec((tm, l), lambda i, j, bi: (i, 0)),
                  pl.BlockSpec((tm, l), lambda i, j, bi: (i + nb, 0)),
                  pl.BlockSpec((None, l, tn), lambda i, j, bi: (bi, 0, j)),
                  pl.BlockSpec((2, 2, tm, tn), lambda i, j, bi: (0, 0, i, j))],
        out_specs=pl.BlockSpec((None, 2, tm, tn), lambda i, j, bi: (bi, 0, i, j)),
        out_shape=jax.ShapeDtypeStruct((b, 2, l, d), BF16),
        compiler_params=_cparams("parallel", "parallel", "parallel"),
        name="hyena_dft_forward",
    )(fwd, fwd, u, hspec)
    return out.reshape(b, 2 * l, d)


def _idft_kernel(g_ref, p_ref, u_ref, x0_ref, skip_ref, o_ref):
    y = _dot(g_ref[...], p_ref[...]) + u_ref[...].astype(F32) * skip_ref[...]
    o_ref[...] = (y * x0_ref[...].astype(F32)).astype(o_ref.dtype)


def _idft_gate(inv, p, u, x0, skip, *, tm=1024, tn=512):
    b, l, d = u.shape
    tm, tn = min(tm, l), min(tn, d)
    return pl.pallas_call(
        _idft_kernel,
        grid=(l // tm, b, d // tn),
        in_specs=[pl.BlockSpec((tm, 2 * l), lambda i, bi, j: (i, 0)),
                  pl.BlockSpec((None, 2 * l, tn), lambda i, bi, j: (bi, 0, j)),
                  pl.BlockSpec((None, tm, tn), lambda i, bi, j: (bi, i, j)),
                  pl.BlockSpec((None, tm, tn), lambda i, bi, j: (bi, i, j)),
                  pl.BlockSpec((1, tn), lambda i, bi, j: (0, j))],
        out_specs=pl.BlockSpec((None, tm, tn), lambda i, bi, j: (bi, i, j)),
        out_shape=jax.ShapeDtypeStruct((b, l, d), BF16),
        compiler_params=_cparams("parallel", "parallel", "parallel"),
        name="hyena_dft_inverse",
    )(inv, p, u, x0, skip.reshape(1, d))


def _hyena_operator(z, conv_w, conv_b, fparams, skip):
    b, l, d3 = z.shape
    d = d3 // 3
    u, x0 = _hyena_gate(z, conv_w, conv_b)
    taps = _hyena_filter(l, d, *fparams)
    fwd, inv = _dft_tables(l)
    hspec = _dft_raw(fwd, taps)
    p = _dft_mul(fwd, u, hspec)
    return _idft_gate(inv, p, u, x0, skip)


def _router_kernel(x_ref, nw_ref, sh_ref, sc_ref, wr_ref, h_ref, lg_ref):
    h = _norm_mod(x_ref[...], nw_ref[...], sh_ref[...], sc_ref[...])
    hh, hl = _split_bf16(h)
    h_ref[...] = hh
    r = _dot(hh, wr_ref[...])
    lg_ref[...] = r[:, :LANE] + r[:, LANE:] + _dot(hl, wr_ref[:, :LANE])


def _norm_mod_router(x, nw, sh, sc, w_router, *, tm=512):
    b, l, d = x.shape
    n_e = w_router.shape[1]
    e = LANE
    w_hi, w_lo = _split_bf16(jnp.pad(w_router.astype(F32), ((0, 0), (0, e - n_e))))
    w_router = jnp.concatenate([w_hi, w_lo], axis=1)
    tm = min(tm, l)
    per_sample = sh.shape[0] != 1
    mod_map = (lambda bi, i: (bi, 0, 0)) if per_sample else (lambda bi, i: (0, 0, 0))
    return pl.pallas_call(
        _router_kernel,
        grid=(b, l // tm),
        in_specs=[
            pl.BlockSpec((None, tm, d), lambda bi, i: (bi, i, 0)),
            pl.BlockSpec((1, d), lambda bi, i: (0, 0)),
            pl.BlockSpec((None, 1, d), mod_map),
            pl.BlockSpec((None, 1, d), mod_map),
            pl.BlockSpec((d, 2 * e), lambda bi, i: (0, 0)),
        ],
        out_specs=[pl.BlockSpec((None, tm, d), lambda bi, i: (bi, i, 0)),
                   pl.BlockSpec((None, tm, e), lambda bi, i: (bi, i, 0))],
        out_shape=[jax.ShapeDtypeStruct((b, l, d), BF16), jax.ShapeDtypeStruct((b, l, e), F32)],
        compiler_params=_cparams("parallel", "parallel"),
        name="norm_mod_router",
    )(x, nw.reshape(1, d), sh, sc, w_router)


def _route_kernel(lg_ref, tri_ref, pos_ref, prob_ref, *, cap):
    lg = lg_ref[...]
    m = jnp.max(lg, axis=0, keepdims=True)
    ex = jnp.exp(lg - m)
    probs = ex / jnp.sum(ex, axis=0, keepdims=True)
    bits = lax.bitcast_convert_type(probs, jnp.int32)

    def count(mask):
        return jnp.sum(jnp.where(mask, 1.0, 0.0), axis=1, keepdims=True)

    def step(i, thr):
        trial = thr | lax.shift_left(jnp.int32(1), 30 - i)
        return jnp.where(count(bits >= trial) >= cap, trial, thr)

    thr = lax.fori_loop(0, 31, step, jnp.zeros((lg.shape[0], 1), jnp.int32))
    gt = bits > thr
    eq = bits == thr
    need = cap - count(gt).astype(jnp.int32)
    both = jnp.concatenate([jnp.where(gt, 1.0, 0.0), jnp.where(eq, 1.0, 0.0)], axis=0).astype(BF16)
    csum = _dot(both, tri_ref[...])
    e = lg.shape[0]
    rank_gt = csum[:e].astype(jnp.int32)
    rank_eq = csum[e:].astype(jnp.int32)
    sel = jnp.logical_or(gt, jnp.logical_and(eq, rank_eq < need))
    pos = rank_gt + jnp.minimum(rank_eq, need)
    pos_ref[...] = jnp.where(sel, pos, -1)
    prob_ref[...] = probs


def _route(logits_t, cap):
    b, e, n = logits_t.shape
    idx = jnp.arange(n, dtype=jnp.int32)
    tri = (idx[:, None] < idx[None, :]).astype(BF16)
    return pl.pallas_call(
        functools.partial(_route_kernel, cap=cap),
        grid=(b,),
        in_specs=[pl.BlockSpec((None, e, n), lambda bi: (bi, 0, 0)),
                  pl.BlockSpec((n, n), lambda bi: (0, 0))],
        out_specs=[pl.BlockSpec((None, e, n), lambda bi: (bi, 0, 0))] * 2,
        out_shape=[jax.ShapeDtypeStruct((b, e, n), jnp.int32), jax.ShapeDtypeStruct((b, e, n), F32)],
        compiler_params=_cparams("parallel"),
        name="expert_choice_route",
    )(logits_t, tri)


def _gather_kernel(h_ref, pos_ref, prob_ref, xg_ref, gate_ref, *, cap):
    n = h_ref.shape[0]
    slot = lax.broadcasted_iota(jnp.int32, (cap, n), 0)
    match = slot == pos_ref[...]
    onehot = jnp.where(match, 1.0, 0.0).astype(BF16)
    xg_ref[...] = _dot(onehot, h_ref[...]).astype(xg_ref.dtype)
    gate_ref[...] = jnp.sum(jnp.where(match, prob_ref[...], 0.0), axis=1, keepdims=True)


def _gather_append_kernel(h_ref, pos_ref, prob_ref, xin_ref, gin_ref, xg_ref, gate_ref, *, cap, nb):
    bi = pl.program_id(0)

    @pl.when(bi < nb)
    def _():
        _gather_kernel(h_ref, pos_ref, prob_ref, xg_ref, gate_ref, cap=cap)

    @pl.when(bi >= nb)
    def _():
        xg_ref[...] = xin_ref[...]
        gate_ref[...] = gin_ref[...]


def _moe_gather(h, pos, probs, cap, tail=None):
    b, n, d = h.shape
    e = pos.shape[1]
    row = lambda a: a.reshape(b, e, 1, n)
    if tail is None:
        nx, kern, clamp = 0, functools.partial(_gather_kernel, cap=cap), lambda bi: bi
        tail_specs, tail_args = [], []
    else:
        r = tail[0].shape[1]
        assert r % cap == 0
        nx, kern = r // cap, functools.partial(_gather_append_kernel, cap=cap, nb=b)
        clamp = lambda bi: jnp.minimum(bi, b - 1)
        tail_map = lambda bi, ei: (ei, jnp.maximum(bi - b, 0), 0)
        tail_specs = [pl.BlockSpec((None, cap, d), tail_map), pl.BlockSpec((None, cap, 1), tail_map)]
        tail_args = list(tail)
    m_total = (b + nx) * cap
    return pl.pallas_call(
        kern,
        grid=(b + nx, e),
        in_specs=[pl.BlockSpec((None, n, d), lambda bi, ei: (clamp(bi), 0, 0)),
                  pl.BlockSpec((None, None, 1, n), lambda bi, ei: (clamp(bi), ei, 0, 0)),
                  pl.BlockSpec((None, None, 1, n), lambda bi, ei: (clamp(bi), ei, 0, 0))] + tail_specs,
        out_specs=[pl.BlockSpec((None, cap, d), lambda bi, ei: (ei, bi, 0)),
                   pl.BlockSpec((None, cap, 1), lambda bi, ei: (ei, bi, 0))],
        out_shape=[jax.ShapeDtypeStruct((e, m_total, d), BF16), jax.ShapeDtypeStruct((e, m_total, 1), F32)],
        compiler_params=_cparams("parallel", "parallel"),
        name="moe_gather",
    )(h, row(pos), row(probs), *tail_args)


def _ffn_kernel(x_ref, g_ref, wg_ref, wu_ref, wd_ref, o_ref, acc_ref):
    f = pl.program_id(2)
    x = x_ref[...]
    a = _dot(x, wg_ref[...].astype(BF16))
    u = _dot(x, wu_ref[...].astype(BF16))
    hmid = (_silu(a) * u * g_ref[...]).astype(BF16)
    y = _dot(hmid, wd_ref[...].astype(BF16))
    acc = jnp.where(f == 0, y, acc_ref[...] + y)
    acc_ref[...] = acc
    o_ref[...] = acc.astype(o_ref.dtype)


def _moe_ffn(xg, gate, w_gate, w_up, w_down, layer, *, tf=256):
    e, m, d = xg.shape
    ff = w_gate.shape[3]
    tm = max(t for t in range(16, FFN_MAX_ROWS + 1, 16) if m % t == 0)
    return pl.pallas_call(
        _ffn_kernel,
        grid=(e, m // tm, ff // tf),
        in_specs=[pl.BlockSpec((None, tm, d), lambda ei, i, f: (ei, i, 0)),
                  pl.BlockSpec((None, tm, 1), lambda ei, i, f: (ei, i, 0)),
                  pl.BlockSpec((None, None, d, tf), lambda ei, i, f: (layer, ei, 0, f)),
                  pl.BlockSpec((None, None, d, tf), lambda ei, i, f: (layer, ei, 0, f)),
                  pl.BlockSpec((None, None, tf, d), lambda ei, i, f: (layer, ei, f, 0))],
        out_specs=pl.BlockSpec((None, tm, d), lambda ei, i, f: (ei, i, 0)),
        out_shape=jax.ShapeDtypeStruct((e, m, d), BF16),
        scratch_shapes=[pltpu.VMEM((tm, d), F32)],
        compiler_params=_cparams("parallel", "parallel", "arbitrary", vmem=FFN_VMEM_LIMIT),
        name="moe_expert_ffn",
    )(xg, gate, w_gate, w_up, w_down)


def _combine_kernel(post_ref, y_ref, r_ref, g_ref, fw_ref, o_ref, pt_scr, *, cap, final_norm):
    j = pl.program_id(2)
    tm, e = post_ref.shape

    @pl.when(j == 0)
    def _():
        post = post_ref[...]
        if cap % LANE == 0:
            slot = lax.broadcasted_iota(jnp.int32, (tm, cap), 1)
            for ei in range(e):
                pt_scr[:, ei * cap:(ei + 1) * cap] = jnp.where(post[:, ei:ei + 1] == slot, 1.0, 0.0).astype(BF16)
        else:
            slot = lax.broadcasted_iota(jnp.int32, (tm, e * cap), 1)
            hit = jnp.zeros((tm, e * cap), jnp.bool_)
            for ei in range(e):
                tgt = jnp.where(post[:, ei:ei + 1] >= 0, post[:, ei:ei + 1] + ei * cap, -1)
                hit = jnp.logical_or(hit, tgt == slot)
            pt_scr[...] = jnp.where(hit, 1.0, 0.0).astype(BF16)

    y = y_ref[...].reshape(e * cap, y_ref.shape[2])
    out = r_ref[...] + g_ref[...] * _dot(pt_scr[...], y)
    if final_norm:
        out = out * lax.rsqrt(jnp.mean(out * out, axis=-1, keepdims=True) + EPS) * fw_ref[...]
    o_ref[...] = out


def _moe_combine(pos_t, y, row0, resid, gate, cap, final_w=None, *, tm=1024, tn=512):
    b, n, e = pos_t.shape
    d = y.shape[2]
    final_norm = final_w is not None
    if final_norm:
        tm, tn = 256, d
    else:
        final_w = jnp.ones((d,), F32)
    tm = min(tm, n)
    blk0 = row0 // cap
    per_sample = gate.shape[0] != 1
    g_map = (lambda bi, i, j: (bi, 0, j)) if per_sample else (lambda bi, i, j: (0, 0, j))
    return pl.pallas_call(
        functools.partial(_combine_kernel, cap=cap, final_norm=final_norm),
        grid=(b, n // tm, d // tn),
        in_specs=[pl.BlockSpec((None, tm, e), lambda bi, i, j: (bi, i, 0)),
                  pl.BlockSpec((e, cap, tn), lambda bi, i, j: (0, blk0 + bi, j)),
                  pl.BlockSpec((None, tm, tn), lambda bi, i, j: (bi, i, j)),
                  pl.BlockSpec((None, 1, tn), g_map),
                  pl.BlockSpec((1, tn), lambda bi, i, j: (0, j))],
        out_specs=pl.BlockSpec((None, tm, tn), lambda bi, i, j: (bi, i, j)),
        out_shape=jax.ShapeDtypeStruct((b, n, d), F32),
        scratch_shapes=[pltpu.VMEM((tm, e * cap), BF16)],
        compiler_params=_cparams("parallel", "parallel", "arbitrary", vmem=FFN_VMEM_LIMIT if final_norm else VMEM_LIMIT),
        name="moe_combine_residual",
    )(pos_t, y, resid, gate, final_w.astype(F32).reshape(1, d))


def _moe_block(streams, nw, w_router, w_gate, w_up, w_down, layer, final_w=None):
    e = w_router.shape[1]
    caps = [CAPACITY_FACTOR * s[0].shape[1] // e for s in streams]
    rows = [s[0].shape[0] * cap for s, cap in zip(streams, caps)]
    row0 = [sum(rows[:t]) for t in range(len(streams))]
    routed, buffers = [None] * len(streams), None
    for t in reversed(range(len(streams))):
        x, sh, sc, _ = streams[t]
        h, logits = _norm_mod_router(x, nw, sh, sc, w_router)
        pos, probs = _route(jnp.swapaxes(logits[..., :e], 1, 2), caps[t])
        buffers = _moe_gather(h, pos, probs, caps[t], buffers)
        routed[t] = jnp.swapaxes(pos, 1, 2)
    y = _moe_ffn(buffers[0], buffers[1], w_gate, w_up, w_down, layer)
    assert final_w is None or len(streams) == 1
    return [_moe_combine(pos_t, y, r0, x, gate, cap, final_w)
            for (x, _, _, gate), pos_t, cap, r0 in zip(streams, routed, caps, row0)]


def _rope_tables(seq_len, dim):
    rows = seq_len // GRID_W
    row_id = jnp.repeat(jnp.arange(rows, dtype=F32), GRID_W)
    col_id = jnp.tile(jnp.arange(GRID_W, dtype=F32), rows)
    nf = dim // 4
    inv = ROPE_THETA ** (-jnp.arange(nf, dtype=F32) / nf)
    ang_r = row_id[:, None] * inv
    ang_c = col_id[:, None] * inv
    cos = jnp.concatenate([jnp.cos(ang_r)] * 2 + [jnp.cos(ang_c)] * 2, axis=-1)
    sin = jnp.concatenate([-jnp.sin(ang_r), jnp.sin(ang_r), -jnp.sin(ang_c), jnp.sin(ang_c)], axis=-1)
    return cos, sin


def _identity_tables(seq_len, dim):
    return jnp.ones((seq_len, dim), F32), jnp.zeros((seq_len, dim), F32)


def _attention_mixer(x, ctx, nw, mods_lat, mods_ctx, w_qkv, q_norm, k_norm, w_o, layer, with_ctx_out):
    hd = ATTN_HEAD_DIM
    nq, nkv = ATTN_HEADS * hd, ATTN_KV_HEADS * hd
    l, lc = x.shape[1], ctx.shape[1]
    tn = 512
    colw = jnp.concatenate([jnp.tile(q_norm.astype(F32) * (hd ** -0.5), ATTN_HEADS),
                            jnp.tile(k_norm.astype(F32), ATTN_KV_HEADS), jnp.ones((nkv,), F32)]).reshape(1, -1)
    cos, sin = _rope_tables(l, hd)
    icos, isin = _identity_tables(lc, hd)
    kw = dict(colw=colw, rope_blocks=(nq + nkv) // tn, head_norm=True, half=hd // 4, tn=tn)
    p_lat = _project(x, nw, mods_lat[0], mods_lat[1], w_qkv, layer, cos=cos, sin=sin, **kw)
    off = 0 if with_ctx_out else nq // tn
    p_ctx = _project(ctx, nw, mods_ctx[0], mods_ctx[1], w_qkv, layer, cos=icos, sin=isin, col_off=off, **kw)
    kc0 = nq - off * tn
    ctx_kv = (p_ctx, kc0, kc0 + nkv)
    o_lat = _attention(p_lat, [ctx_kv, (p_lat, nq, nq + nkv)])
    x = _out_project(o_lat, w_o, layer, x, mods_lat[2])
    if with_ctx_out:
        ctx = _out_project(_attention(p_ctx, [ctx_kv]), w_o, layer, ctx, mods_ctx[2])
    return x, ctx


def _retention_mixer(x, ctx, nw, mods_lat, mods_ctx, w_in, decay_logit, w_o, layer, with_ctx_out):
    dk, hh = RET_QK_DIM, RET_HEADS
    l, lc = x.shape[1], ctx.shape[1]
    tn = 512
    log_g = jax.nn.log_sigmoid(decay_logit.astype(F32))
    colw = jnp.concatenate([jnp.ones((hh * dk,), F32), jnp.full((hh * dk,), dk ** -0.5, F32),
                            jnp.ones((w_in.shape[2] - 2 * hh * dk,), F32)]).reshape(1, -1)
    rope_blocks = 2 * hh * dk // tn
    cos, sin = _rope_tables(l, dk)
    icos, isin = _identity_tables(lc, dk)
    kw = dict(colw=colw, rope_blocks=rope_blocks, head_norm=False, half=dk // 4, tn=tn)
    p_lat = _project(x, nw, mods_lat[0], mods_lat[1], w_in, layer, cos=cos, sin=sin, **kw)
    p_ctx = _project(ctx, nw, mods_ctx[0], mods_ctx[1], w_in, layer, cos=icos, sin=isin, **kw)
    r_ctx, r_lat = _retention(log_g, p_ctx, p_lat, with_ctx_out)
    x = _out_project(r_lat, w_o, layer, x, mods_lat[2])
    if with_ctx_out:
        ctx = _out_project(r_ctx, w_o, layer, ctx, mods_ctx[2])
    return x, ctx


def _hyena_mixer(x, ctx, nw, mods_lat, mods_ctx, w_in, conv_w, conv_b, fparams, skip, w_out, layer, with_ctx_out):
    z = _project(x, nw, mods_lat[0], mods_lat[1], w_in, layer)
    x = _out_project(_hyena_operator(z, conv_w, conv_b, fparams, skip), w_out, layer, x, mods_lat[2])
    if with_ctx_out:
        zc = _project(ctx, nw, mods_ctx[0], mods_ctx[1], w_in, layer)
        ctx = _out_project(_hyena_operator(zc, conv_w, conv_b, fparams, skip), w_out, layer, ctx, mods_ctx[2])
    return x, ctx


def kernel(x, c, ctx, c_ctx, w_mod, b_mod, norm_w, attn_w_qkv, attn_q_norm, attn_k_norm, attn_w_o, ret_w_in, ret_decay_logit, ret_w_o, hy_w_in, hy_conv_w, hy_conv_b, hy_f_w1, hy_f_b1, hy_f_freq1, hy_f_w2, hy_f_b2, hy_f_freq2, hy_f_w3, hy_skip, hy_w_out, moe_router, moe_w_gate, moe_w_up, moe_w_down, final_norm_w):
    depth = w_mod.shape[0]
    b, _, d = x.shape
    rows = -(-(b + 1) // 8) * 8
    c_all = jnp.concatenate([c, c_ctx[None, :], jnp.zeros((rows - b - 1, d), F32)], axis=0)
    mod = _modulation(c_all, w_mod, b_mod)

    for i in range(depth):
        kind, j = i % 3, i // 3
        with_ctx = i < depth - 1
        m = mod[i].reshape(rows, N_MOD, d)
        mods_lat = [m[:b, t][:, None, :] for t in range(N_MOD)]
        mods_ctx = [m[b:b + 1, t][:, None, :] for t in range(N_MOD)]
        nw1, nw2 = norm_w[i, 0], norm_w[i, 1]
        if kind == 0:
            x, ctx = _attention_mixer(x, ctx, nw1, mods_lat, mods_ctx, attn_w_qkv, attn_q_norm[j], attn_k_norm[j],
                                      attn_w_o, j, with_ctx)
        elif kind == 1:
            x, ctx = _retention_mixer(x, ctx, nw1, mods_lat, mods_ctx, ret_w_in, ret_decay_logit[j], ret_w_o,
                                      j, with_ctx)
        else:
            fparams = (hy_f_w1[j], hy_f_b1[j], hy_f_freq1[j], hy_f_w2[j], hy_f_b2[j], hy_f_freq2[j], hy_f_w3[j])
            x, ctx = _hyena_mixer(x, ctx, nw1, mods_lat, mods_ctx, hy_w_in, hy_conv_w[j], hy_conv_b[j], fparams,
                                  hy_skip[j], hy_w_out, j, with_ctx)
        streams = [(x, mods_lat[3], mods_lat[4], mods_lat[5])]
        if with_ctx:
            streams.append((ctx, mods_ctx[3], mods_ctx[4], mods_ctx[5]))
        outs = _moe_block(streams, nw2, moe_router[i], moe_w_gate, moe_w_up, moe_w_down, i,
                          final_norm_w if i == depth - 1 else None)
        x = outs[0]
        if with_ctx:
            ctx = outs[1]
    return x
```

```python
import functools
import math

import jax
import jax.numpy as jnp
from jax import lax
from jax.experimental import pallas as pl
from jax.experimental.pallas import tpu as pltpu

F32 = jnp.float32
BF16 = jnp.bfloat16

EPS = 1e-6
GRID_W = 64
ROPE_THETA = 10000.0
ATTN_HEADS = 16
ATTN_KV_HEADS = 4
ATTN_HEAD_DIM = 128
RET_HEADS = 8
RET_QK_DIM = 256
RET_V_DIM = 512
N_EXPERTS = 16
CAPACITY_FACTOR = 2
N_MOD = 6
HYENA_EMB = 33
HYENA_TARGET = 1e-2
HYENA_FAST = 0.3
HYENA_SLOW = 1.5
HYENA_SHIFT = 0.0

LANE = 128
VMEM_LIMIT = 52 * 1024 * 1024
FFN_VMEM_LIMIT = 58 * 1024 * 1024
FFN_MAX_ROWS = 1152


def _cparams(*sem, vmem=VMEM_LIMIT):
    return pltpu.CompilerParams(dimension_semantics=sem, vmem_limit_bytes=vmem)


def _silu(v):
    return v / (1.0 + jnp.exp(-v))


def _split_bf16(v):
    hi = v.astype(BF16)
    lo = (v - hi.astype(F32)).astype(BF16)
    return hi, lo


def _dot(a, b):
    return jnp.dot(a, b, preferred_element_type=F32)


def _dot3(a, b):
    ah, al = _split_bf16(a)
    bh, bl = _split_bf16(b)
    return _dot(ah, bh) + _dot(al, bh) + _dot(ah, bl)


def _dot_nt(a, b):
    return lax.dot_general(a, b, (((1,), (1,)), ((), ())), preferred_element_type=F32)


def _dot_tn(a, b):
    return lax.dot_general(a, b, (((0,), (0,)), ((), ())), preferred_element_type=F32)


def _mod_kernel(c_ref, w_ref, b_ref, o_ref):
    a = _silu(c_ref[...])
    o_ref[...] = _dot3(a, w_ref[...]) + b_ref[...]


def _modulation(c_all, w_mod, b_mod):
    depth, d, n = w_mod.shape
    rows = c_all.shape[0]
    tn = 1024
    return pl.pallas_call(
        _mod_kernel,
        grid=(depth, n // tn),
        in_specs=[
            pl.BlockSpec((rows, d), lambda i, j: (0, 0)),
            pl.BlockSpec((None, d, tn), lambda i, j: (i, 0, j)),
            pl.BlockSpec((None, 1, tn), lambda i, j: (i, 0, j)),
        ],
        out_specs=pl.BlockSpec((None, rows, tn), lambda i, j: (i, 0, j)),
        out_shape=jax.ShapeDtypeStruct((depth, rows, n), F32),
        compiler_params=_cparams("parallel", "parallel"),
        name="modulation",
    )(c_all, w_mod, b_mod.reshape(depth, 1, n))


def _norm_mod(x, nw, sh, sc):
    ms = jnp.mean(x * x, axis=-1, keepdims=True)
    y = x * lax.rsqrt(ms + EPS) * nw
    return y * (1.0 + sc) + sh


def _rope_slices(acc, colw, cos, sin, *, head_norm, half, table_w):
    tn = acc.shape[1]
    lane = lax.broadcasted_iota(jnp.int32, (1, LANE), 1)
    first_half = (lane % (2 * half)) < half
    outs = []
    for s in range(tn // LANE):
        xs = acc[:, s * LANE:(s + 1) * LANE]
        if head_norm:
            xs = xs * lax.rsqrt(jnp.mean(xs * xs, axis=-1, keepdims=True) + EPS)
        xs = xs * colw[:, s * LANE:(s + 1) * LANE]
        t0 = (s * LANE) % table_w
        cs = cos[:, t0:t0 + LANE]
        sn = sin[:, t0:t0 + LANE]
        if 2 * half == LANE:
            partner = pltpu.roll(xs, half, axis=1)
        else:
            partner = jnp.where(first_half, pltpu.roll(xs, LANE - half, axis=1), pltpu.roll(xs, half, axis=1))
        outs.append(xs * cs + partner * sn)
    return jnp.concatenate(outs, axis=1)


def _norm_pass_kernel(x_ref, nw_ref, sh_ref, sc_ref, o_ref):
    o_ref[...] = _norm_mod(x_ref[...], nw_ref[...], sh_ref[...], sc_ref[...]).astype(o_ref.dtype)


def _norm_mod_pass(x, nw, sh, sc, *, tm=512):
    b, l, d = x.shape
    tm = min(tm, l)
    per_sample = sh.shape[0] != 1
    mod_map = (lambda bi, i: (bi, 0, 0)) if per_sample else (lambda bi, i: (0, 0, 0))
    return pl.pallas_call(
        _norm_pass_kernel,
        grid=(b, l // tm),
        in_specs=[pl.BlockSpec((None, tm, d), lambda bi, i: (bi, i, 0)),
                  pl.BlockSpec((1, d), lambda bi, i: (0, 0)),
                  pl.BlockSpec((None, 1, d), mod_map),
                  pl.BlockSpec((None, 1, d), mod_map)],
        out_specs=pl.BlockSpec((None, tm, d), lambda bi, i: (bi, i, 0)),
        out_shape=jax.ShapeDtypeStruct((b, l, d), BF16),
        compiler_params=_cparams("parallel", "parallel"),
        name="norm_modulate",
    )(x, nw.reshape(1, d), sh, sc)


def _proj_kernel(h_ref, w_ref, colw_ref, cos_ref, sin_ref, o_ref, *, rope_blocks, col_off, head_norm, half, table_w):
    j = pl.program_id(2)
    acc = _dot(h_ref[...], w_ref[...].astype(BF16))
    o_ref[...] = acc.astype(o_ref.dtype)
    if rope_blocks:
        @pl.when(j + col_off < rope_blocks)
        def _():
            o_ref[...] = _rope_slices(acc, colw_ref[...], cos_ref[...], sin_ref[...], head_norm=head_norm,
                                      half=half, table_w=table_w).astype(o_ref.dtype)


def _project(x, nw, sh, sc, w, layer, *, h=None, colw=None, cos=None, sin=None, rope_blocks=0, col_off=0,
             head_norm=False, half=32, tn=512, tm=2048):
    b, l, d = x.shape
    n = w.shape[2] - col_off * tn
    per_sample = sh.shape[0] != 1
    if colw is None:
        colw = jnp.ones((1, w.shape[2]), F32)
        cos = jnp.ones((l, LANE), F32)
        sin = jnp.zeros((l, LANE), F32)
    if not per_sample and b > 1:
        out = _project(x.reshape(1, b * l, d), nw, sh, sc, w, layer, colw=colw, cos=jnp.tile(cos, (b, 1)),
                       sin=jnp.tile(sin, (b, 1)), rope_blocks=rope_blocks, col_off=col_off, head_norm=head_norm,
                       half=half, tn=tn, tm=tm)
        return out.reshape(b, l, n)
    tm = min(tm, l)
    table_w = cos.shape[1]
    kern = functools.partial(_proj_kernel, rope_blocks=rope_blocks, col_off=col_off, head_norm=head_norm,
                             half=half, table_w=table_w)
    return pl.pallas_call(
        kern,
        grid=(b, l // tm, n // tn),
        in_specs=[
            pl.BlockSpec((None, tm, d), lambda bi, i, j: (bi, i, 0)),
            pl.BlockSpec((None, d, tn), lambda bi, i, j: (layer, 0, j + col_off)),
            pl.BlockSpec((1, tn), lambda bi, i, j: (0, j + col_off)),
            pl.BlockSpec((tm, table_w), lambda bi, i, j: (i, 0)),
            pl.BlockSpec((tm, table_w), lambda bi, i, j: (i, 0)),
        ],
        out_specs=pl.BlockSpec((None, tm, tn), lambda bi, i, j: (bi, i, j)),
        out_shape=jax.ShapeDtypeStruct((b, l, n), BF16),
        compiler_params=_cparams("parallel", "parallel", "parallel"),
        name="project",
    )(_norm_mod_pass(x, nw, sh, sc) if h is None else h, w, colw, cos, sin)


def _out_proj_kernel(a_ref, w_ref, r_ref, g_ref, o_ref):
    acc = _dot(a_ref[...], w_ref[...].astype(BF16))
    o_ref[...] = r_ref[...] + g_ref[...] * acc


def _out_project(a, w, layer, resid, gate, *, tm=None, tn=512):
    b, l, k = a.shape
    n = w.shape[2]
    if tm is None:
        tm = (2048 * 2048) // k
    per_sample = gate.shape[0] != 1
    if not per_sample and b > 1:
        out = _out_project(a.reshape(1, b * l, k), w, layer, resid.reshape(1, b * l, n), gate, tm=tm, tn=tn)
        return out.reshape(b, l, n)
    tm = min(tm, l)
    g_map = (lambda bi, i, j: (bi, 0, j)) if per_sample else (lambda bi, i, j: (0, 0, j))
    return pl.pallas_call(
        _out_proj_kernel,
        grid=(b, l // tm, n // tn),
        in_specs=[
            pl.BlockSpec((None, tm, k), lambda bi, i, j: (bi, i, 0)),
            pl.BlockSpec((None, k, tn), lambda bi, i, j: (layer, 0, j)),
            pl.BlockSpec((None, tm, tn), lambda bi, i, j: (bi, i, j)),
            pl.BlockSpec((None, 1, tn), g_map),
        ],
        out_specs=pl.BlockSpec((None, tm, tn), lambda bi, i, j: (bi, i, j)),
        out_shape=jax.ShapeDtypeStruct((b, l, n), F32),
        compiler_params=_cparams("parallel", "parallel", "parallel"),
        name="out_project_residual",
    )(a, w, resid, gate)


def _attn_kernel(q_ref, *refs, groups, n_kv):
    hd = ATTN_HEAD_DIM
    o_ref = refs[2 * n_kv]
    ks = [refs[2 * t][...] for t in range(n_kv)]
    vs = [jnp.concatenate([refs[2 * t + 1][...], jnp.ones((k.shape[0], hd), BF16)], axis=1) for t, k in enumerate(ks)]
    for g in range(groups):
        q = q_ref[:, g * hd:(g + 1) * hd]
        ss = [_dot_nt(q, k) for k in ks]
        m = functools.reduce(jnp.maximum, [jnp.max(s, axis=-1, keepdims=True) for s in ss])
        ps = [jnp.exp(s - m).astype(BF16) for s in ss]
        ov = sum(_dot(p, v) for p, v in zip(ps, vs))
        o_ref[:, g * hd:(g + 1) * hd] = (ov[:, :hd] / ov[:, hd:hd + 1]).astype(o_ref.dtype)


def _attention(pq, kv_sources, *, tq=512):
    b, lq, _ = pq.shape
    hd = ATTN_HEAD_DIM
    groups = ATTN_HEADS // ATTN_KV_HEADS
    gw = groups * hd
    tq = min(tq, lq)
    in_specs = [pl.BlockSpec((None, tq, gw), lambda bi, h, i: (bi, i, h))]
    args = [pq]
    for p, k0, v0 in kv_sources:
        lk = p.shape[1]
        in_specs.append(pl.BlockSpec((None, lk, hd), lambda bi, h, i, o=k0 // hd: (bi, 0, o + h)))
        in_specs.append(pl.BlockSpec((None, lk, hd), lambda bi, h, i, o=v0 // hd: (bi, 0, o + h)))
        args += [p, p]
    return pl.pallas_call(
        functools.partial(_attn_kernel, groups=groups, n_kv=len(kv_sources)),
        grid=(b, ATTN_KV_HEADS, lq // tq),
        in_specs=in_specs,
        out_specs=pl.BlockSpec((None, tq, gw), lambda bi, h, i: (bi, i, h)),
        out_shape=jax.ShapeDtypeStruct((b, lq, ATTN_HEADS * hd), BF16),
        compiler_params=_cparams("parallel", "parallel", "parallel"),
        name="gqa_attention",
    )(*args)


RET_CHUNK = 256


def _ret_readout(o, g):
    of = o * lax.rsqrt(jnp.mean(o * o, axis=-1, keepdims=True) + EPS)
    return (_silu(g.astype(F32)) * of).astype(BF16)


def _ret_kernel(lg_ref, qc_ref, kc_ref, vc_ref, gc_ref, ql_ref, kl_ref, vl_ref, gl_ref, oc_ref, ol_ref,
                of_scr, ob_scr, sf_scr, sb_scr, *, n_chunks, with_ctx_out):
    c = RET_CHUNK
    h = pl.program_id(1)
    lgf = lg_ref[0, h]
    lgb = lg_ref[1, h]
    row = lax.broadcasted_iota(jnp.int32, (c, 1), 0).astype(F32)
    col = lax.broadcasted_iota(jnp.int32, (1, c), 1).astype(F32)
    diff = row - col
    dmask = jnp.exp(jnp.where(diff >= 0, diff * lgf, -diff * lgb))
    qdec_f = jnp.exp((row + 1.0) * lgf)
    kdec_f = jnp.exp((c - 1.0 - row) * lgf)
    qdec_b = jnp.exp((c - row) * lgb)
    kdec_b = jnp.exp(row * lgb)
    one = jnp.ones((1, 1), F32)
    cdec_f = jnp.exp(one * (c * lgf))
    cdec_b = jnp.exp(one * (c * lgb))

    qc = qc_ref[...]
    kc = kc_ref[...].astype(F32)
    vc = vc_ref[...]
    sf_scr[...] = _dot_tn((kc * kdec_f).astype(BF16), vc)
    sb_scr[...] = _dot_tn((kc * kdec_b).astype(BF16), vc)
    if with_ctx_out:
        inner = (_dot_nt(qc, kc_ref[...]) * dmask).astype(BF16)
        oc_ref[...] = _ret_readout(_dot(inner, vc), gc_ref[...])
    else:
        oc_ref[...] = jnp.zeros(oc_ref.shape, oc_ref.dtype)

    def scan(i, carry):
        sl = pl.ds(pl.multiple_of(i * c, c), c)
        qb = ql_ref[sl, :]
        kb = kl_ref[sl, :]
        v = vl_ref[sl, :]
        inner = (_dot_nt(qb, kb) * dmask).astype(BF16)
        of_scr[sl, :] = _dot(inner, v) + _dot((qb.astype(F32) * qdec_f).astype(BF16), sf_scr[...].astype(BF16))
        sf_scr[...] = sf_scr[...] * cdec_f + _dot_tn((kb.astype(F32) * kdec_f).astype(BF16), v)

        sr = pl.ds(pl.multiple_of((n_chunks - 1 - i) * c, c), c)
        q = ql_ref[sr, :].astype(F32)
        k = kl_ref[sr, :].astype(F32)
        ob_scr[sr, :] = _dot((q * qdec_b).astype(BF16), sb_scr[...].astype(BF16))
        sb_scr[...] = sb_scr[...] * cdec_b + _dot_tn((k * kdec_b).astype(BF16), vl_ref[sr, :])
        return carry

    lax.fori_loop(0, n_chunks, scan, 0)

    def readout(i, carry):
        sl = pl.ds(pl.multiple_of(i * c, c), c)
        ol_ref[sl, :] = _ret_readout(of_scr[sl, :] + ob_scr[sl, :], gl_ref[sl, :])
        return carry

    lax.fori_loop(0, n_chunks, readout, 0)


def _retention(log_g, p_ctx, p_lat, with_ctx_out):
    b, l, _ = p_lat.shape
    lc = p_ctx.shape[1]
    assert lc == RET_CHUNK and l % RET_CHUNK == 0
    dk, dv, hh = RET_QK_DIM, RET_V_DIM, RET_HEADS
    k_off = hh * dk // dk
    v_off = 2 * hh * dk // dv
    g_off = v_off + hh

    def specs(ln):
        return [
            pl.BlockSpec((None, ln, dk), lambda bi, h: (bi, 0, h)),
            pl.BlockSpec((None, ln, dk), lambda bi, h: (bi, 0, k_off + h)),
            pl.BlockSpec((None, ln, dv), lambda bi, h: (bi, 0, v_off + h)),
            pl.BlockSpec((None, ln, dv), lambda bi, h: (bi, 0, g_off + h)),
        ]

    return pl.pallas_call(
        functools.partial(_ret_kernel, n_chunks=l // RET_CHUNK, with_ctx_out=with_ctx_out),
        grid=(b, hh),
        in_specs=[pl.BlockSpec(memory_space=pltpu.SMEM)] + specs(lc) + specs(l),
        out_specs=[
            pl.BlockSpec((None, lc, dv), lambda bi, h: (bi, 0, h)),
            pl.BlockSpec((None, l, dv), lambda bi, h: (bi, 0, h)),
        ],
        out_shape=[
            jax.ShapeDtypeStruct((b, lc, hh * dv), BF16),
            jax.ShapeDtypeStruct((b, l, hh * dv), BF16),
        ],
        scratch_shapes=[pltpu.VMEM((l, dv), F32), pltpu.VMEM((l, dv), F32), pltpu.VMEM((dk, dv), F32),
                        pltpu.VMEM((dk, dv), F32)],
        compiler_params=_cparams("parallel", "parallel"),
        name="retention",
    )(log_g, p_ctx, p_ctx, p_ctx, p_ctx, p_lat, p_lat, p_lat, p_lat)


def _hy_gate_kernel(x0_ref, x1_ref, v_ref, cw0_ref, cw1_ref, cwv_ref, cb0_ref, cb1_ref, cbv_ref, u_ref, g_ref):
    l = x0_ref.shape[0]
    t = lax.broadcasted_iota(jnp.int32, (l, 1), 0)

    def conv3(z_ref, cw_ref, cb_ref):
        z = z_ref[...].astype(F32)
        prev = jnp.where(t == 0, 0.0, pltpu.roll(z, 1, axis=0))
        nxt = jnp.where(t == l - 1, 0.0, pltpu.roll(z, l - 1, axis=0))
        cw = cw_ref[...]
        return prev * cw[0:1, :] + z * cw[1:2, :] + nxt * cw[2:3, :] + cb_ref[...]

    x1 = conv3(x1_ref, cw1_ref, cb1_ref)
    v = conv3(v_ref, cwv_ref, cbv_ref)
    u_ref[...] = (v * x1).astype(u_ref.dtype)
    g_ref[...] = conv3(x0_ref, cw0_ref, cb0_ref).astype(g_ref.dtype)


def _hyena_gate(z, conv_w, conv_b, *, tn=256):
    b, l, d3 = z.shape
    d = d3 // 3
    nb = d // tn
    cb = conv_b.reshape(1, d3)
    zs = [pl.BlockSpec((None, l, tn), (lambda bi, j, o=o: (bi, 0, j + o * nb))) for o in range(3)]
    ws = [pl.BlockSpec((3, tn), (lambda bi, j, o=o: (0, j + o * nb))) for o in range(3)]
    bs = [pl.BlockSpec((1, tn), (lambda bi, j, o=o: (0, j + o * nb))) for o in range(3)]
    return pl.pallas_call(
        _hy_gate_kernel,
        grid=(b, nb),
        in_specs=zs + ws + bs,
        out_specs=[pl.BlockSpec((None, l, tn), lambda bi, j: (bi, 0, j))] * 2,
        out_shape=[jax.ShapeDtypeStruct((b, l, d), BF16)] * 2,
        compiler_params=_cparams("parallel", "parallel"),
        name="hyena_conv3_gate",
    )(z, z, z, conv_w, conv_w, conv_w, cb, cb, cb)


def _hy_filter_kernel(z_ref, w1_ref, b1_ref, f1_ref, w2_ref, b2_ref, f2_ref, w3f_ref, w3b_ref, dec_ref, o_ref, h_scr):
    l = z_ref.shape[0]

    @pl.when(pl.program_id(0) == 0)
    def _():
        h1 = jnp.sin(f1_ref[...] * (_dot3(z_ref[...], w1_ref[...]) + b1_ref[...]))
        h_scr[...] = jnp.sin(f2_ref[...] * (_dot3(h1, w2_ref[...]) + b2_ref[...]))

    h = h_scr[...]
    decay = dec_ref[...] + HYENA_SHIFT
    hf = _dot3(h, w3f_ref[...]) * decay
    hb = _dot3(h, w3b_ref[...]) * decay
    t = lax.broadcasted_iota(jnp.int32, (l, 1), 0)
    hb = jnp.where(t == 0, 0.0, hb)
    norm = jnp.sum(jnp.abs(hf), axis=0, keepdims=True) + jnp.sum(jnp.abs(hb), axis=0, keepdims=True)
    o_ref[0] = (hf / norm).astype(o_ref.dtype)
    o_ref[1] = (hb / norm).astype(o_ref.dtype)


def _hyena_filter(l, d, w1, b1, fr1, w2, b2, fr2, w3, *, tn=256):
    t = jnp.linspace(0.0, 1.0, l, dtype=F32)[:, None]
    bands = (HYENA_EMB - 1) // 2
    w = 2.0 * math.pi * jnp.arange(l, dtype=F32)[:, None] / l
    f = jnp.linspace(1e-4, bands - 1, bands, dtype=F32)[None, :]
    z = jnp.concatenate([t, jnp.cos(f * w), -jnp.sin(f * w)], axis=-1)
    deltas = jnp.abs(jnp.linspace(math.log(HYENA_TARGET) / HYENA_SLOW, math.log(HYENA_TARGET) / HYENA_FAST, d, dtype=F32))
    decay = jnp.exp(-t * deltas)
    fw = w1.shape[1]
    pad = lambda a, r, c: jnp.pad(a.astype(F32), ((0, r - a.shape[0]), (0, c - a.shape[1])))
    z = pad(z, l, LANE)
    w1p = pad(w1, LANE, LANE)
    w2p = pad(w2, LANE, LANE)
    w3p = pad(w3, LANE, 2 * d)
    vec = lambda a: pad(a.reshape(1, fw), 1, LANE)
    nb = d // tn
    full = lambda shape: pl.BlockSpec(shape, lambda j: (0, 0))
    return pl.pallas_call(
        _hy_filter_kernel,
        grid=(nb,),
        in_specs=[full((l, LANE)), full((LANE, LANE)), full((1, LANE)), full((1, LANE)), full((LANE, LANE)),
                  full((1, LANE)), full((1, LANE)),
                  pl.BlockSpec((LANE, tn), lambda j: (0, j)),
                  pl.BlockSpec((LANE, tn), lambda j: (0, j + nb)),
                  pl.BlockSpec((l, tn), lambda j: (0, j))],
        out_specs=pl.BlockSpec((2, l, tn), lambda j: (0, 0, j)),
        out_shape=jax.ShapeDtypeStruct((2, l, d), BF16),
        scratch_shapes=[pltpu.VMEM((l, LANE), F32)],
        compiler_params=_cparams("arbitrary"),
        name="hyena_filter",
    )(z, w1p, vec(b1), vec(fr1), w2p, vec(b2), vec(fr2), w3p, w3p, decay)


def _dft_tables(l):
    n = 2 * l
    k = jnp.arange(l, dtype=jnp.int32)
    nb = 1 << ((l.bit_length() - 1 + 1) // 2)
    na = l // nb
    theta = lambda m: (m % n).astype(F32) * (2.0 * math.pi / n)
    ang_a = theta(k[:, None] * (jnp.arange(na, dtype=jnp.int32) * nb)[None, :])[:, :, None]
    ang_b = theta(k[:, None] * jnp.arange(nb, dtype=jnp.int32)[None, :])[:, None, :]
    cs = (jnp.cos(ang_a) * jnp.cos(ang_b) - jnp.sin(ang_a) * jnp.sin(ang_b)).reshape(l, l)
    sn = (jnp.sin(ang_a) * jnp.cos(ang_b) + jnp.cos(ang_a) * jnp.sin(ang_b)).reshape(l, l)
    alt = jnp.where(k % 2 == 0, 1.0, -1.0).astype(F32)
    f_b = jnp.where(k[:, None] == 0, alt[None, :], -sn)
    fwd = jnp.concatenate([cs, f_b], axis=0)
    wa = jnp.where(k[None, :] == 0, 1.0, 2.0) * cs.T
    wb = jnp.where(k[None, :] == 0, alt[:, None], -2.0 * sn.T)
    inv = jnp.concatenate([wa, wb], axis=1) * (1.0 / n)
    return fwd.astype(BF16), inv.astype(BF16)


def _dft_raw_kernel(fa_ref, fb_ref, u_ref, o_ref):
    u = u_ref[...]
    o_ref[0] = _dot(fa_ref[...], u)
    o_ref[1] = _dot(fb_ref[...], u)


def _dft_raw(fwd, u, *, tm=512, tn=512):
    b, l, d = u.shape
    tm, tn = min(tm, l), min(tn, d)
    nb = l // tm
    return pl.pallas_call(
        _dft_raw_kernel,
        grid=(b, d // tn, nb),
        in_specs=[pl.BlockSpec((tm, l), lambda bi, j, i: (i, 0)),
                  pl.BlockSpec((tm, l), lambda bi, j, i: (i + nb, 0)),
                  pl.BlockSpec((None, l, tn), lambda bi, j, i: (bi, 0, j))],
        out_specs=pl.BlockSpec((None, 2, tm, tn), lambda bi, j, i: (bi, 0, i, j)),
        out_shape=jax.ShapeDtypeStruct((b, 2, l, d), F32),
        compiler_params=_cparams("parallel", "parallel", "parallel"),
        name="hyena_dft_filter",
    )(fwd, fwd, u)


def _dft_mul_kernel(fa_ref, fb_ref, u_ref, hs_ref, o_ref):
    i = pl.program_id(0)
    u = u_ref[...]
    ua = _dot(fa_ref[...], u)
    ub = _dot(fb_ref[...], u)
    first = jnp.logical_and(lax.broadcasted_iota(jnp.int32, (ua.shape[0], 1), 0) == 0, i == 0)
    ha = hs_ref[0, 0] + hs_ref[1, 0]
    hb = jnp.where(first, hs_ref[0, 1] + hs_ref[1, 1], hs_ref[0, 1] - hs_ref[1, 1])
    pa = jnp.where(first, ua * ha, ua * ha - ub * hb)
    pb = jnp.where(first, ub * hb, ua * hb + ub * ha)
    o_ref[0] = pa.astype(o_ref.dtype)
    o_ref[1] = pb.astype(o_ref.dtype)


def _dft_mul(fwd, u, hspec, *, tm=512, tn=512):
    b, l, d = u.shape
    tm, tn = min(tm, l), min(tn, d)
    nb = l // tm
    out = pl.pallas_call(
        _dft_mul_kernel,
        grid=(nb, d // tn, b),
        in_specs=[pl.BlockSpec((tm, l), lambda i, j, bi: (i, 0)),
                  pl.BlockSpec((tm, l), lambda i, j, bi: (i + nb, 0)),
                  pl.BlockSpec((None, l, tn), lambda i, j, bi: (bi, 0, j)),
                  pl.BlockSpec((2, 2, tm, tn), lambda i, j, bi: (0, 0, i, j))],
        out_specs=pl.BlockSpec((None, 2, tm, tn), lambda i, j, bi: (bi, 0, i, j)),
        out_shape=jax.ShapeDtypeStruct((b, 2, l, d), BF16),
        compiler_params=_cparams("parallel", "parallel", "parallel"),
        name="hyena_dft_forward",
    )(fwd, fwd, u, hspec)
    return out.reshape(b, 2 * l, d)


def _idft_kernel(g_ref, p_ref, u_ref, x0_ref, skip_ref, o_ref):
    y = _dot(g_ref[...], p_ref[...]) + u_ref[...].astype(F32) * skip_ref[...]
    o_ref[...] = (y * x0_ref[...].astype(F32)).astype(o_ref.dtype)


def _idft_gate(inv, p, u, x0, skip, *, tm=1024, tn=512):
    b, l, d = u.shape
    tm, tn = min(tm, l), min(tn, d)
    return pl.pallas_call(
        _idft_kernel,
        grid=(l // tm, b, d // tn),
        in_specs=[pl.BlockSpec((tm, 2 * l), lambda i, bi, j: (i, 0)),
                  pl.BlockSpec((None, 2 * l, tn), lambda i, bi, j: (bi, 0, j)),
                  pl.BlockSpec((None, tm, tn), lambda i, bi, j: (bi, i, j)),
                  pl.BlockSpec((None, tm, tn), lambda i, bi, j: (bi, i, j)),
                  pl.BlockSpec((1, tn), lambda i, bi, j: (0, j))],
        out_specs=pl.BlockSpec((None, tm, tn), lambda i, bi, j: (bi, i, j)),
        out_shape=jax.ShapeDtypeStruct((b, l, d), BF16),
        compiler_params=_cparams("parallel", "parallel", "parallel"),
        name="hyena_dft_inverse",
    )(inv, p, u, x0, skip.reshape(1, d))


def _hyena_operator(z, conv_w, conv_b, fparams, skip):
    b, l, d3 = z.shape
    d = d3 // 3
    u, x0 = _hyena_gate(z, conv_w, conv_b)
    taps = _hyena_filter(l, d, *fparams)
    fwd, inv = _dft_tables(l)
    hspec = _dft_raw(fwd, taps)
    p = _dft_mul(fwd, u, hspec)
    return _idft_gate(inv, p, u, x0, skip)


def _router_kernel(x_ref, nw_ref, sh_ref, sc_ref, wr_ref, h_ref, lg_ref):
    h = _norm_mod(x_ref[...], nw_ref[...], sh_ref[...], sc_ref[...])
    hh, hl = _split_bf16(h)
    h_ref[...] = hh
    wh, wl = _split_bf16(wr_ref[...])
    r = _dot(hh, jnp.concatenate([wh, wl], axis=1))
    lg_ref[...] = r[:, :LANE] + r[:, LANE:] + _dot(hl, wh)


def _norm_mod_router(x, nw, sh, sc, w_router, *, tm=512):
    b, l, d = x.shape
    n_e = w_router.shape[1]
    e = LANE
    w_router = jnp.pad(w_router.astype(F32), ((0, 0), (0, e - n_e)))
    tm = min(tm, l)
    per_sample = sh.shape[0] != 1
    mod_map = (lambda bi, i: (bi, 0, 0)) if per_sample else (lambda bi, i: (0, 0, 0))
    return pl.pallas_call(
        _router_kernel,
        grid=(b, l // tm),
        in_specs=[
            pl.BlockSpec((None, tm, d), lambda bi, i: (bi, i, 0)),
            pl.BlockSpec((1, d), lambda bi, i: (0, 0)),
            pl.BlockSpec((None, 1, d), mod_map),
            pl.BlockSpec((None, 1, d), mod_map),
            pl.BlockSpec((d, e), lambda bi, i: (0, 0)),
        ],
        out_specs=[pl.BlockSpec((None, tm, d), lambda bi, i: (bi, i, 0)),
                   pl.BlockSpec((None, tm, e), lambda bi, i: (bi, i, 0))],
        out_shape=[jax.ShapeDtypeStruct((b, l, d), BF16), jax.ShapeDtypeStruct((b, l, e), F32)],
        compiler_params=_cparams("parallel", "parallel"),
        name="norm_mod_router",
    )(x, nw.reshape(1, d), sh, sc, w_router)


def _route_kernel(lg_ref, tri_ref, pos_ref, prob_ref, *, cap):
    lg = lg_ref[...]
    m = jnp.max(lg, axis=0, keepdims=True)
    ex = jnp.exp(lg - m)
    probs = ex / jnp.sum(ex, axis=0, keepdims=True)
    bits = lax.bitcast_convert_type(probs, jnp.int32)

    def count(mask):
        return jnp.sum(jnp.where(mask, 1.0, 0.0), axis=1, keepdims=True)

    def step(i, thr):
        trial = thr | lax.shift_left(jnp.int32(1), 30 - i)
        return jnp.where(count(bits >= trial) >= cap, trial, thr)

    thr = lax.fori_loop(0, 31, step, jnp.zeros((lg.shape[0], 1), jnp.int32))
    gt = bits > thr
    eq = bits == thr
    need = cap - count(gt).astype(jnp.int32)
    both = jnp.concatenate([jnp.where(gt, 1.0, 0.0), jnp.where(eq, 1.0, 0.0)], axis=0).astype(BF16)
    csum = _dot(both, tri_ref[...])
    e = lg.shape[0]
    rank_gt = csum[:e].astype(jnp.int32)
    rank_eq = csum[e:].astype(jnp.int32)
    sel = jnp.logical_or(gt, jnp.logical_and(eq, rank_eq < need))
    pos = rank_gt + jnp.minimum(rank_eq, need)
    pos_ref[...] = jnp.where(sel, pos, -1)
    prob_ref[...] = probs


def _route(logits_t, cap):
    b, e, n = logits_t.shape
    idx = jnp.arange(n, dtype=jnp.int32)
    tri = (idx[:, None] < idx[None, :]).astype(BF16)
    return pl.pallas_call(
        functools.partial(_route_kernel, cap=cap),
        grid=(b,),
        in_specs=[pl.BlockSpec((None, e, n), lambda bi: (bi, 0, 0)),
                  pl.BlockSpec((n, n), lambda bi: (0, 0))],
        out_specs=[pl.BlockSpec((None, e, n), lambda bi: (bi, 0, 0))] * 2,
        out_shape=[jax.ShapeDtypeStruct((b, e, n), jnp.int32), jax.ShapeDtypeStruct((b, e, n), F32)],
        compiler_params=_cparams("parallel"),
        name="expert_choice_route",
    )(logits_t, tri)


def _gather_kernel(h_ref, pos_ref, prob_ref, xg_ref, gate_ref, *, cap):
    n = h_ref.shape[0]
    slot = lax.broadcasted_iota(jnp.int32, (cap, n), 0)
    match = slot == pos_ref[...]
    onehot = jnp.where(match, 1.0, 0.0).astype(BF16)
    xg_ref[...] = _dot(onehot, h_ref[...]).astype(xg_ref.dtype)
    gate_ref[...] = jnp.sum(jnp.where(match, prob_ref[...], 0.0), axis=1, keepdims=True)


def _gather_append_kernel(h_ref, pos_ref, prob_ref, xin_ref, gin_ref, xg_ref, gate_ref, *, cap, nb):
    bi = pl.program_id(0)

    @pl.when(bi < nb)
    def _():
        _gather_kernel(h_ref, pos_ref, prob_ref, xg_ref, gate_ref, cap=cap)

    @pl.when(bi >= nb)
    def _():
        xg_ref[...] = xin_ref[...]
        gate_ref[...] = gin_ref[...]


def _moe_gather(h, pos, probs, cap, tail=None):
    b, n, d = h.shape
    e = pos.shape[1]
    row = lambda a: a.reshape(b, e, 1, n)
    if tail is None:
        nx, kern, clamp = 0, functools.partial(_gather_kernel, cap=cap), lambda bi: bi
        tail_specs, tail_args = [], []
    else:
        r = tail[0].shape[1]
        assert r % cap == 0
        nx, kern = r // cap, functools.partial(_gather_append_kernel, cap=cap, nb=b)
        clamp = lambda bi: jnp.minimum(bi, b - 1)
        tail_map = lambda bi, ei: (ei, jnp.maximum(bi - b, 0), 0)
        tail_specs = [pl.BlockSpec((None, cap, d), tail_map), pl.BlockSpec((None, cap, 1), tail_map)]
        tail_args = list(tail)
    m_total = (b + nx) * cap
    return pl.pallas_call(
        kern,
        grid=(b + nx, e),
        in_specs=[pl.BlockSpec((None, n, d), lambda bi, ei: (clamp(bi), 0, 0)),
                  pl.BlockSpec((None, None, 1, n), lambda bi, ei: (clamp(bi), ei, 0, 0)),
                  pl.BlockSpec((None, None, 1, n), lambda bi, ei: (clamp(bi), ei, 0, 0))] + tail_specs,
        out_specs=[pl.BlockSpec((None, cap, d), lambda bi, ei: (ei, bi, 0)),
                   pl.BlockSpec((None, cap, 1), lambda bi, ei: (ei, bi, 0))],
        out_shape=[jax.ShapeDtypeStruct((e, m_total, d), BF16), jax.ShapeDtypeStruct((e, m_total, 1), F32)],
        compiler_params=_cparams("parallel", "parallel"),
        name="moe_gather",
    )(h, row(pos), row(probs), *tail_args)


def _ffn_kernel(x_ref, g_ref, wg_ref, wu_ref, wd_ref, o_ref, acc_ref):
    f = pl.program_id(2)
    x = x_ref[...]
    a = _dot(x, wg_ref[...].astype(BF16))
    u = _dot(x, wu_ref[...].astype(BF16))
    hmid = (_silu(a) * u * g_ref[...]).astype(BF16)
    y = _dot(hmid, wd_ref[...].astype(BF16))
    acc = jnp.where(f == 0, y, acc_ref[...] + y)
    acc_ref[...] = acc
    o_ref[...] = acc.astype(o_ref.dtype)


def _moe_ffn(xg, gate, w_gate, w_up, w_down, layer, *, tf=256):
    e, m, d = xg.shape
    ff = w_gate.shape[3]
    tm = max(t for t in range(16, FFN_MAX_ROWS + 1, 16) if m % t == 0)
    return pl.pallas_call(
        _ffn_kernel,
        grid=(e, m // tm, ff // tf),
        in_specs=[pl.BlockSpec((None, tm, d), lambda ei, i, f: (ei, i, 0)),
                  pl.BlockSpec((None, tm, 1), lambda ei, i, f: (ei, i, 0)),
                  pl.BlockSpec((None, None, d, tf), lambda ei, i, f: (layer, ei, 0, f)),
                  pl.BlockSpec((None, None, d, tf), lambda ei, i, f: (layer, ei, 0, f)),
                  pl.BlockSpec((None, None, tf, d), lambda ei, i, f: (layer, ei, f, 0))],
        out_specs=pl.BlockSpec((None, tm, d), lambda ei, i, f: (ei, i, 0)),
        out_shape=jax.ShapeDtypeStruct((e, m, d), BF16),
        scratch_shapes=[pltpu.VMEM((tm, d), F32)],
        compiler_params=_cparams("parallel", "parallel", "arbitrary", vmem=FFN_VMEM_LIMIT),
        name="moe_expert_ffn",
    )(xg, gate, w_gate, w_up, w_down)


def _combine_kernel(post_ref, y_ref, r_ref, g_ref, *refs, cap, tail):
    if tail == "final":
        fw_ref, o_ref, pt_scr = refs
    elif tail == "next":
        nw_ref, sh_ref, sc_ref, o_ref, h_ref, pt_scr = refs
    else:
        o_ref, pt_scr = refs
    j = pl.program_id(2)
    tm, e = post_ref.shape

    @pl.when(j == 0)
    def _():
        post = post_ref[...]
        if cap % LANE == 0:
            slot = lax.broadcasted_iota(jnp.int32, (tm, cap), 1)
            for ei in range(e):
                pt_scr[:, ei * cap:(ei + 1) * cap] = jnp.where(post[:, ei:ei + 1] == slot, 1.0, 0.0).astype(BF16)
        else:
            slot = lax.broadcasted_iota(jnp.int32, (tm, e * cap), 1)
            hit = jnp.zeros((tm, e * cap), jnp.bool_)
            for ei in range(e):
                tgt = jnp.where(post[:, ei:ei + 1] >= 0, post[:, ei:ei + 1] + ei * cap, -1)
                hit = jnp.logical_or(hit, tgt == slot)
            pt_scr[...] = jnp.where(hit, 1.0, 0.0).astype(BF16)

    y = y_ref[...].reshape(e * cap, y_ref.shape[2])
    out = r_ref[...] + g_ref[...] * _dot(pt_scr[...], y)
    if tail == "final":
        out = out * lax.rsqrt(jnp.mean(out * out, axis=-1, keepdims=True) + EPS) * fw_ref[...]
    o_ref[...] = out
    if tail == "next":
        h_ref[...] = _norm_mod(out, nw_ref[...], sh_ref[...], sc_ref[...]).astype(h_ref.dtype)


def _moe_combine(pos_t, y, row0, resid, gate, cap, final_w=None, next_norm=None, *, tm=1024, tn=512):
    b, n, e = pos_t.shape
    d = y.shape[2]
    assert final_w is None or next_norm is None
    tail = "final" if final_w is not None else "next" if next_norm is not None else None
    if tail:
        tm, tn = 256, d
    tm = min(tm, n)
    blk0 = row0 // cap
    per_sample = gate.shape[0] != 1
    g_map = (lambda bi, i, j: (bi, 0, j)) if per_sample else (lambda bi, i, j: (0, 0, j))
    row_spec = pl.BlockSpec((1, tn), lambda bi, i, j: (0, j))
    out_spec = pl.BlockSpec((None, tm, tn), lambda bi, i, j: (bi, i, j))
    out_specs, out_shape = out_spec, jax.ShapeDtypeStruct((b, n, d), F32)
    extra_specs, extra_args = [], []
    if tail == "final":
        extra_specs, extra_args = [row_spec], [final_w.astype(F32).reshape(1, d)]
    elif tail == "next":
        mod_spec = pl.BlockSpec((None, 1, tn), lambda bi, i, j: (bi, 0, j))
        extra_specs = [row_spec, mod_spec, mod_spec]
        extra_args = [next_norm[0].astype(F32).reshape(1, d), next_norm[1], next_norm[2]]
        out_specs, out_shape = [out_spec, out_spec], [out_shape, jax.ShapeDtypeStruct((b, n, d), BF16)]
    return pl.pallas_call(
        functools.partial(_combine_kernel, cap=cap, tail=tail),
        grid=(b, n // tm, d // tn),
        in_specs=[pl.BlockSpec((None, tm, e), lambda bi, i, j: (bi, i, 0)),
                  pl.BlockSpec((e, cap, tn), lambda bi, i, j: (0, blk0 + bi, j)),
                  out_spec,
                  pl.BlockSpec((None, 1, tn), g_map)] + extra_specs,
        out_specs=out_specs,
        out_shape=out_shape,
        scratch_shapes=[pltpu.VMEM((tm, e * cap), BF16)],
        compiler_params=_cparams("parallel", "parallel", "arbitrary", vmem=FFN_VMEM_LIMIT if tail else VMEM_LIMIT),
        name="moe_combine_residual",
    )(pos_t, y, resid, gate, *extra_args)


def _moe_block(streams, nw, w_router, w_gate, w_up, w_down, layer, final_w=None, next_norm=None):
    e = w_router.shape[1]
    caps = [CAPACITY_FACTOR * s[0].shape[1] // e for s in streams]
    rows = [s[0].shape[0] * cap for s, cap in zip(streams, caps)]
    row0 = [sum(rows[:t]) for t in range(len(streams))]
    routed, buffers = [None] * len(streams), None
    for t in reversed(range(len(streams))):
        x, sh, sc, _ = streams[t]
        h, logits = _norm_mod_router(x, nw, sh, sc, w_router)
        pos, probs = _route(jnp.swapaxes(logits[..., :e], 1, 2), caps[t])
        buffers = _moe_gather(h, pos, probs, caps[t], buffers)
        routed[t] = jnp.swapaxes(pos, 1, 2)
    y = _moe_ffn(buffers[0], buffers[1], w_gate, w_up, w_down, layer)
    return [_moe_combine(routed[t], y, row0[t], streams[t][0], streams[t][3], caps[t],
                         final_w if t == 0 else None, next_norm if t == 0 else None)
            for t in range(len(streams))]


def _rope_tables(seq_len, dim):
    rows = seq_len // GRID_W
    row_id = jnp.repeat(jnp.arange(rows, dtype=F32), GRID_W)
    col_id = jnp.tile(jnp.arange(GRID_W, dtype=F32), rows)
    nf = dim // 4
    inv = ROPE_THETA ** (-jnp.arange(nf, dtype=F32) / nf)
    ang_r = row_id[:, None] * inv
    ang_c = col_id[:, None] * inv
    cos = jnp.concatenate([jnp.cos(ang_r)] * 2 + [jnp.cos(ang_c)] * 2, axis=-1)
    sin = jnp.concatenate([-jnp.sin(ang_r), jnp.sin(ang_r), -jnp.sin(ang_c), jnp.sin(ang_c)], axis=-1)
    return cos, sin


def _identity_tables(seq_len, dim):
    return jnp.ones((seq_len, dim), F32), jnp.zeros((seq_len, dim), F32)


def _attention_mixer(x, ctx, nw, mods_lat, mods_ctx, w_qkv, q_norm, k_norm, w_o, layer, with_ctx_out, h_lat=None):
    hd = ATTN_HEAD_DIM
    nq, nkv = ATTN_HEADS * hd, ATTN_KV_HEADS * hd
    l, lc = x.shape[1], ctx.shape[1]
    tn = 512
    colw = jnp.concatenate([jnp.tile(q_norm.astype(F32) * (hd ** -0.5), ATTN_HEADS),
                            jnp.tile(k_norm.astype(F32), ATTN_KV_HEADS), jnp.ones((nkv,), F32)]).reshape(1, -1)
    cos, sin = _rope_tables(l, hd)
    icos, isin = _identity_tables(lc, hd)
    kw = dict(colw=colw, rope_blocks=(nq + nkv) // tn, head_norm=True, half=hd // 4, tn=tn)
    p_lat = _project(x, nw, mods_lat[0], mods_lat[1], w_qkv, layer, h=h_lat, cos=cos, sin=sin, **kw)
    off = 0 if with_ctx_out else nq // tn
    p_ctx = _project(ctx, nw, mods_ctx[0], mods_ctx[1], w_qkv, layer, cos=icos, sin=isin, col_off=off, **kw)
    kc0 = nq - off * tn
    ctx_kv = (p_ctx, kc0, kc0 + nkv)
    o_lat = _attention(p_lat, [ctx_kv, (p_lat, nq, nq + nkv)])
    x = _out_project(o_lat, w_o, layer, x, mods_lat[2])
    if with_ctx_out:
        ctx = _out_project(_attention(p_ctx, [ctx_kv]), w_o, layer, ctx, mods_ctx[2])
    return x, ctx


def _retention_mixer(x, ctx, nw, mods_lat, mods_ctx, w_in, decay_logit, w_o, layer, with_ctx_out, h_lat=None):
    dk, hh = RET_QK_DIM, RET_HEADS
    l, lc = x.shape[1], ctx.shape[1]
    tn = 512
    log_g = jax.nn.log_sigmoid(decay_logit.astype(F32))
    colw = jnp.concatenate([jnp.ones((hh * dk,), F32), jnp.full((hh * dk,), dk ** -0.5, F32),
                            jnp.ones((w_in.shape[2] - 2 * hh * dk,), F32)]).reshape(1, -1)
    rope_blocks = 2 * hh * dk // tn
    cos, sin = _rope_tables(l, dk)
    icos, isin = _identity_tables(lc, dk)
    kw = dict(colw=colw, rope_blocks=rope_blocks, head_norm=False, half=dk // 4, tn=tn)
    p_lat = _project(x, nw, mods_lat[0], mods_lat[1], w_in, layer, h=h_lat, cos=cos, sin=sin, **kw)
    p_ctx = _project(ctx, nw, mods_ctx[0], mods_ctx[1], w_in, layer, cos=icos, sin=isin, **kw)
    r_ctx, r_lat = _retention(log_g, p_ctx, p_lat, with_ctx_out)
    x = _out_project(r_lat, w_o, layer, x, mods_lat[2])
    if with_ctx_out:
        ctx = _out_project(r_ctx, w_o, layer, ctx, mods_ctx[2])
    return x, ctx


def _hyena_mixer(x, ctx, nw, mods_lat, mods_ctx, w_in, conv_w, conv_b, fparams, skip, w_out, layer, with_ctx_out,
                 h_lat=None):
    z = _project(x, nw, mods_lat[0], mods_lat[1], w_in, layer, h=h_lat)
    x = _out_project(_hyena_operator(z, conv_w, conv_b, fparams, skip), w_out, layer, x, mods_lat[2])
    if with_ctx_out:
        zc = _project(ctx, nw, mods_ctx[0], mods_ctx[1], w_in, layer)
        ctx = _out_project(_hyena_operator(zc, conv_w, conv_b, fparams, skip), w_out, layer, ctx, mods_ctx[2])
    return x, ctx


def kernel(x, c, ctx, c_ctx, w_mod, b_mod, norm_w, attn_w_qkv, attn_q_norm, attn_k_norm, attn_w_o, ret_w_in, ret_decay_logit, ret_w_o, hy_w_in, hy_conv_w, hy_conv_b, hy_f_w1, hy_f_b1, hy_f_freq1, hy_f_w2, hy_f_b2, hy_f_freq2, hy_f_w3, hy_skip, hy_w_out, moe_router, moe_w_gate, moe_w_up, moe_w_down, final_norm_w):
    depth = w_mod.shape[0]
    b, _, d = x.shape
    rows = -(-(b + 1) // 8) * 8
    c_all = jnp.concatenate([c, c_ctx[None, :], jnp.zeros((rows - b - 1, d), F32)], axis=0)
    mod = _modulation(c_all, w_mod, b_mod)

    def layer_mods(i):
        m = mod[i].reshape(rows, N_MOD, d)
        lat = [m[:b, t][:, None, :] for t in range(N_MOD)]
        ctx_ = [m[b:b + 1, t][:, None, :] for t in range(N_MOD)]
        return lat, ctx_

    h_lat = None
    for i in range(depth):
        kind, j = i % 3, i // 3
        last = i == depth - 1
        with_ctx = not last
        mods_lat, mods_ctx = layer_mods(i)
        nw1, nw2 = norm_w[i, 0], norm_w[i, 1]
        if kind == 0:
            x, ctx = _attention_mixer(x, ctx, nw1, mods_lat, mods_ctx, attn_w_qkv, attn_q_norm[j], attn_k_norm[j],
                                      attn_w_o, j, with_ctx, h_lat)
        elif kind == 1:
            x, ctx = _retention_mixer(x, ctx, nw1, mods_lat, mods_ctx, ret_w_in, ret_decay_logit[j], ret_w_o,
                                      j, with_ctx, h_lat)
        else:
            fparams = (hy_f_w1[j], hy_f_b1[j], hy_f_freq1[j], hy_f_w2[j], hy_f_b2[j], hy_f_freq2[j], hy_f_w3[j])
            x, ctx = _hyena_mixer(x, ctx, nw1, mods_lat, mods_ctx, hy_w_in, hy_conv_w[j], hy_conv_b[j], fparams,
                                  hy_skip[j], hy_w_out, j, with_ctx, h_lat)
        streams = [(x, mods_lat[3], mods_lat[4], mods_lat[5])]
        if with_ctx:
            streams.append((ctx, mods_ctx[3], mods_ctx[4], mods_ctx[5]))
        if last:
            tails = dict(final_w=final_norm_w)
        else:
            nxt = layer_mods(i + 1)[0]
            tails = dict(next_norm=(norm_w[i + 1, 0], nxt[0], nxt[1]))
        outs = _moe_block(streams, nw2, moe_router[i], moe_w_gate, moe_w_up, moe_w_down, i, **tails)
        if last:
            x = outs[0]
        else:
            (x, h_lat), ctx = outs
    return x
```

```python
import functools
import math

import jax
import jax.numpy as jnp
from jax import lax
from jax.experimental import pallas as pl
from jax.experimental.pallas import tpu as pltpu

F32 = jnp.float32
BF16 = jnp.bfloat16

EPS = 1e-6
GRID_W = 64
ROPE_THETA = 10000.0
ATTN_HEADS = 16
ATTN_KV_HEADS = 4
ATTN_HEAD_DIM = 128
RET_HEADS = 8
RET_QK_DIM = 256
RET_V_DIM = 512
N_EXPERTS = 16
CAPACITY_FACTOR = 2
N_MOD = 6
HYENA_EMB = 33
HYENA_TARGET = 1e-2
HYENA_FAST = 0.3
HYENA_SLOW = 1.5
HYENA_SHIFT = 0.0

LANE = 128
VMEM_LIMIT = 52 * 1024 * 1024
FFN_VMEM_LIMIT = 58 * 1024 * 1024
FFN_MAX_ROWS = 1152


def _cparams(*sem, vmem=VMEM_LIMIT):
    return pltpu.CompilerParams(dimension_semantics=sem, vmem_limit_bytes=vmem)


def _silu(v):
    return v / (1.0 + jnp.exp(-v))


def _split_bf16(v):
    hi = v.astype(BF16)
    lo = (v - hi.astype(F32)).astype(BF16)
    return hi, lo


def _dot(a, b):
    return jnp.dot(a, b, preferred_element_type=F32)


def _dot3(a, b):
    ah, al = _split_bf16(a)
    bh, bl = _split_bf16(b)
    return _dot(ah, bh) + _dot(al, bh) + _dot(ah, bl)


def _dot_nt(a, b):
    return lax.dot_general(a, b, (((1,), (1,)), ((), ())), preferred_element_type=F32)


def _dot_tn(a, b):
    return lax.dot_general(a, b, (((0,), (0,)), ((), ())), preferred_element_type=F32)


def _mod_kernel(c_ref, w_ref, b_ref, o_ref):
    a = _silu(c_ref[...])
    o_ref[...] = _dot3(a, w_ref[...]) + b_ref[...]


def _modulation(c_all, w_mod, b_mod):
    depth, d, n = w_mod.shape
    rows = c_all.shape[0]
    tn = 1024
    return pl.pallas_call(
        _mod_kernel,
        grid=(depth, n // tn),
        in_specs=[
            pl.BlockSpec((rows, d), lambda i, j: (0, 0)),
            pl.BlockSpec((None, d, tn), lambda i, j: (i, 0, j)),
            pl.BlockSpec((None, 1, tn), lambda i, j: (i, 0, j)),
        ],
        out_specs=pl.BlockSpec((None, rows, tn), lambda i, j: (i, 0, j)),
        out_shape=jax.ShapeDtypeStruct((depth, rows, n), F32),
        compiler_params=_cparams("parallel", "parallel"),
        name="modulation",
    )(c_all, w_mod, b_mod.reshape(depth, 1, n))


def _norm_mod(x, nw, sh, sc):
    ms = jnp.mean(x * x, axis=-1, keepdims=True)
    y = x * lax.rsqrt(ms + EPS) * nw
    return y * (1.0 + sc) + sh


def _rope_slices(acc, colw, cos, sin, *, head_norm, half, table_w):
    tn = acc.shape[1]
    lane = lax.broadcasted_iota(jnp.int32, (1, LANE), 1)
    first_half = (lane % (2 * half)) < half
    outs = []
    for s in range(tn // LANE):
        xs = acc[:, s * LANE:(s + 1) * LANE]
        if head_norm:
            xs = xs * lax.rsqrt(jnp.mean(xs * xs, axis=-1, keepdims=True) + EPS)
        xs = xs * colw[:, s * LANE:(s + 1) * LANE]
        t0 = (s * LANE) % table_w
        cs = cos[:, t0:t0 + LANE]
        sn = sin[:, t0:t0 + LANE]
        if 2 * half == LANE:
            partner = pltpu.roll(xs, half, axis=1)
        else:
            partner = jnp.where(first_half, pltpu.roll(xs, LANE - half, axis=1), pltpu.roll(xs, half, axis=1))
        outs.append(xs * cs + partner * sn)
    return jnp.concatenate(outs, axis=1)


def _norm_pass_kernel(x_ref, nw_ref, sh_ref, sc_ref, o_ref):
    o_ref[...] = _norm_mod(x_ref[...], nw_ref[...], sh_ref[...], sc_ref[...]).astype(o_ref.dtype)


def _norm_mod_pass(x, nw, sh, sc, *, tm=512):
    b, l, d = x.shape
    tm = min(tm, l)
    per_sample = sh.shape[0] != 1
    mod_map = (lambda bi, i: (bi, 0, 0)) if per_sample else (lambda bi, i: (0, 0, 0))
    return pl.pallas_call(
        _norm_pass_kernel,
        grid=(b, l // tm),
        in_specs=[pl.BlockSpec((None, tm, d), lambda bi, i: (bi, i, 0)),
                  pl.BlockSpec((1, d), lambda bi, i: (0, 0)),
                  pl.BlockSpec((None, 1, d), mod_map),
                  pl.BlockSpec((None, 1, d), mod_map)],
        out_specs=pl.BlockSpec((None, tm, d), lambda bi, i: (bi, i, 0)),
        out_shape=jax.ShapeDtypeStruct((b, l, d), BF16),
        compiler_params=_cparams("parallel", "parallel"),
        name="norm_modulate",
    )(x, nw.reshape(1, d), sh, sc)


def _proj_kernel(h_ref, w_ref, colw_ref, cos_ref, sin_ref, o_ref, *, rope_blocks, col_off, head_norm, half, table_w):
    j = pl.program_id(2)
    acc = _dot(h_ref[...], w_ref[...].astype(BF16))
    o_ref[...] = acc.astype(o_ref.dtype)
    if rope_blocks:
        @pl.when(j + col_off < rope_blocks)
        def _():
            o_ref[...] = _rope_slices(acc, colw_ref[...], cos_ref[...], sin_ref[...], head_norm=head_norm,
                                      half=half, table_w=table_w).astype(o_ref.dtype)


def _project(x, nw, sh, sc, w, layer, *, h=None, colw=None, cos=None, sin=None, rope_blocks=0, col_off=0,
             head_norm=False, half=32, tn=512, tm=2048):
    b, l, d = x.shape
    n = w.shape[2] - col_off * tn
    per_sample = sh.shape[0] != 1
    if colw is None:
        colw = jnp.ones((1, w.shape[2]), F32)
        cos = jnp.ones((l, LANE), F32)
        sin = jnp.zeros((l, LANE), F32)
    if not per_sample and b > 1:
        out = _project(x.reshape(1, b * l, d), nw, sh, sc, w, layer, colw=colw, cos=jnp.tile(cos, (b, 1)),
                       sin=jnp.tile(sin, (b, 1)), rope_blocks=rope_blocks, col_off=col_off, head_norm=head_norm,
                       half=half, tn=tn, tm=tm)
        return out.reshape(b, l, n)
    tm = min(tm, l)
    table_w = cos.shape[1]
    kern = functools.partial(_proj_kernel, rope_blocks=rope_blocks, col_off=col_off, head_norm=head_norm,
                             half=half, table_w=table_w)
    return pl.pallas_call(
        kern,
        grid=(b, l // tm, n // tn),
        in_specs=[
            pl.BlockSpec((None, tm, d), lambda bi, i, j: (bi, i, 0)),
            pl.BlockSpec((None, d, tn), lambda bi, i, j: (layer, 0, j + col_off)),
            pl.BlockSpec((1, tn), lambda bi, i, j: (0, j + col_off)),
            pl.BlockSpec((tm, table_w), lambda bi, i, j: (i, 0)),
            pl.BlockSpec((tm, table_w), lambda bi, i, j: (i, 0)),
        ],
        out_specs=pl.BlockSpec((None, tm, tn), lambda bi, i, j: (bi, i, j)),
        out_shape=jax.ShapeDtypeStruct((b, l, n), BF16),
        compiler_params=_cparams("parallel", "parallel", "parallel"),
        name="project",
    )(_norm_mod_pass(x, nw, sh, sc) if h is None else h, w, colw, cos, sin)


def _out_proj_kernel(a_ref, w_ref, r_ref, g_ref, o_ref):
    acc = _dot(a_ref[...], w_ref[...].astype(BF16))
    o_ref[...] = r_ref[...] + g_ref[...] * acc


def _out_project(a, w, layer, resid, gate, *, tm=None, tn=512):
    b, l, k = a.shape
    n = w.shape[2]
    if tm is None:
        tm = (2048 * 2048) // k
    per_sample = gate.shape[0] != 1
    if not per_sample and b > 1:
        out = _out_project(a.reshape(1, b * l, k), w, layer, resid.reshape(1, b * l, n), gate, tm=tm, tn=tn)
        return out.reshape(b, l, n)
    tm = min(tm, l)
    g_map = (lambda bi, i, j: (bi, 0, j)) if per_sample else (lambda bi, i, j: (0, 0, j))
    return pl.pallas_call(
        _out_proj_kernel,
        grid=(b, l // tm, n // tn),
        in_specs=[
            pl.BlockSpec((None, tm, k), lambda bi, i, j: (bi, i, 0)),
            pl.BlockSpec((None, k, tn), lambda bi, i, j: (layer, 0, j)),
            pl.BlockSpec((None, tm, tn), lambda bi, i, j: (bi, i, j)),
            pl.BlockSpec((None, 1, tn), g_map),
        ],
        out_specs=pl.BlockSpec((None, tm, tn), lambda bi, i, j: (bi, i, j)),
        out_shape=jax.ShapeDtypeStruct((b, l, n), F32),
        compiler_params=_cparams("parallel", "parallel", "parallel"),
        name="out_project_residual",
    )(a, w, resid, gate)


def _attn_kernel(q_ref, *refs, groups, n_kv):
    hd = ATTN_HEAD_DIM
    o_ref = refs[2 * n_kv]
    ks = [refs[2 * t][...] for t in range(n_kv)]
    vs = [jnp.concatenate([refs[2 * t + 1][...], jnp.ones((k.shape[0], hd), BF16)], axis=1) for t, k in enumerate(ks)]
    for g in range(groups):
        q = q_ref[:, g * hd:(g + 1) * hd]
        ss = [_dot_nt(q, k) for k in ks]
        m = functools.reduce(jnp.maximum, [jnp.max(s, axis=-1, keepdims=True) for s in ss])
        ps = [jnp.exp(s - m).astype(BF16) for s in ss]
        ov = sum(_dot(p, v) for p, v in zip(ps, vs))
        o_ref[:, g * hd:(g + 1) * hd] = (ov[:, :hd] / ov[:, hd:hd + 1]).astype(o_ref.dtype)


def _attention(pq, kv_sources, *, tq=1024):
    b, lq, _ = pq.shape
    hd = ATTN_HEAD_DIM
    groups = ATTN_HEADS // ATTN_KV_HEADS
    gw = groups * hd
    tq = min(tq, lq)
    in_specs = [pl.BlockSpec((None, tq, gw), lambda bi, h, i: (bi, i, h))]
    args = [pq]
    for p, k0, v0 in kv_sources:
        lk = p.shape[1]
        in_specs.append(pl.BlockSpec((None, lk, hd), lambda bi, h, i, o=k0 // hd: (bi, 0, o + h)))
        in_specs.append(pl.BlockSpec((None, lk, hd), lambda bi, h, i, o=v0 // hd: (bi, 0, o + h)))
        args += [p, p]
    return pl.pallas_call(
        functools.partial(_attn_kernel, groups=groups, n_kv=len(kv_sources)),
        grid=(b, ATTN_KV_HEADS, lq // tq),
        in_specs=in_specs,
        out_specs=pl.BlockSpec((None, tq, gw), lambda bi, h, i: (bi, i, h)),
        out_shape=jax.ShapeDtypeStruct((b, lq, ATTN_HEADS * hd), BF16),
        compiler_params=_cparams("parallel", "parallel", "parallel"),
        name="gqa_attention",
    )(*args)


RET_CHUNK = 256


def _ret_readout(o, g):
    of = o * lax.rsqrt(jnp.mean(o * o, axis=-1, keepdims=True) + EPS)
    return (_silu(g.astype(F32)) * of).astype(BF16)


def _ret_kernel(lg_ref, qc_ref, kc_ref, vc_ref, gc_ref, ql_ref, kl_ref, vl_ref, gl_ref, oc_ref, ol_ref,
                of_scr, ob_scr, sf_scr, sb_scr, *, n_chunks, with_ctx_out):
    c = RET_CHUNK
    h = pl.program_id(1)
    lgf = lg_ref[0, h]
    lgb = lg_ref[1, h]
    row = lax.broadcasted_iota(jnp.int32, (c, 1), 0).astype(F32)
    col = lax.broadcasted_iota(jnp.int32, (1, c), 1).astype(F32)
    diff = row - col
    dmask = jnp.exp(jnp.where(diff >= 0, diff * lgf, -diff * lgb))
    qdec_f = jnp.exp((row + 1.0) * lgf)
    kdec_f = jnp.exp((c - 1.0 - row) * lgf)
    qdec_b = jnp.exp((c - row) * lgb)
    kdec_b = jnp.exp(row * lgb)
    one = jnp.ones((1, 1), F32)
    cdec_f = jnp.exp(one * (c * lgf))
    cdec_b = jnp.exp(one * (c * lgb))

    qc = qc_ref[...]
    kc = kc_ref[...].astype(F32)
    vc = vc_ref[...]
    sf_scr[...] = _dot_tn((kc * kdec_f).astype(BF16), vc)
    sb_scr[...] = _dot_tn((kc * kdec_b).astype(BF16), vc)
    if with_ctx_out:
        inner = (_dot_nt(qc, kc_ref[...]) * dmask).astype(BF16)
        oc_ref[...] = _ret_readout(_dot(inner, vc), gc_ref[...])
    else:
        oc_ref[...] = jnp.zeros(oc_ref.shape, oc_ref.dtype)

    def scan(i, carry):
        sl = pl.ds(pl.multiple_of(i * c, c), c)
        qb = ql_ref[sl, :]
        kb = kl_ref[sl, :]
        v = vl_ref[sl, :]
        inner = (_dot_nt(qb, kb) * dmask).astype(BF16)
        of_scr[sl, :] = _dot(inner, v) + _dot((qb.astype(F32) * qdec_f).astype(BF16), sf_scr[...].astype(BF16))
        sf_scr[...] = sf_scr[...] * cdec_f + _dot_tn((kb.astype(F32) * kdec_f).astype(BF16), v)

        sr = pl.ds(pl.multiple_of((n_chunks - 1 - i) * c, c), c)
        q = ql_ref[sr, :].astype(F32)
        k = kl_ref[sr, :].astype(F32)
        ob_scr[sr, :] = _dot((q * qdec_b).astype(BF16), sb_scr[...].astype(BF16))
        sb_scr[...] = sb_scr[...] * cdec_b + _dot_tn((k * kdec_b).astype(BF16), vl_ref[sr, :])
        return carry

    lax.fori_loop(0, n_chunks, scan, 0)

    def readout(i, carry):
        sl = pl.ds(pl.multiple_of(i * c, c), c)
        ol_ref[sl, :] = _ret_readout(of_scr[sl, :] + ob_scr[sl, :], gl_ref[sl, :])
        return carry

    lax.fori_loop(0, n_chunks, readout, 0)


def _retention(log_g, p_ctx, p_lat, with_ctx_out):
    b, l, _ = p_lat.shape
    lc = p_ctx.shape[1]
    assert lc == RET_CHUNK and l % RET_CHUNK == 0
    dk, dv, hh = RET_QK_DIM, RET_V_DIM, RET_HEADS
    k_off = hh * dk // dk
    v_off = 2 * hh * dk // dv
    g_off = v_off + hh

    def specs(ln):
        return [
            pl.BlockSpec((None, ln, dk), lambda bi, h: (bi, 0, h)),
            pl.BlockSpec((None, ln, dk), lambda bi, h: (bi, 0, k_off + h)),
            pl.BlockSpec((None, ln, dv), lambda bi, h: (bi, 0, v_off + h)),
            pl.BlockSpec((None, ln, dv), lambda bi, h: (bi, 0, g_off + h)),
        ]

    return pl.pallas_call(
        functools.partial(_ret_kernel, n_chunks=l // RET_CHUNK, with_ctx_out=with_ctx_out),
        grid=(b, hh),
        in_specs=[pl.BlockSpec(memory_space=pltpu.SMEM)] + specs(lc) + specs(l),
        out_specs=[
            pl.BlockSpec((None, lc, dv), lambda bi, h: (bi, 0, h)),
            pl.BlockSpec((None, l, dv), lambda bi, h: (bi, 0, h)),
        ],
        out_shape=[
            jax.ShapeDtypeStruct((b, lc, hh * dv), BF16),
            jax.ShapeDtypeStruct((b, l, hh * dv), BF16),
        ],
        scratch_shapes=[pltpu.VMEM((l, dv), F32), pltpu.VMEM((l, dv), F32), pltpu.VMEM((dk, dv), F32),
                        pltpu.VMEM((dk, dv), F32)],
        compiler_params=_cparams("parallel", "parallel"),
        name="retention",
    )(log_g, p_ctx, p_ctx, p_ctx, p_ctx, p_lat, p_lat, p_lat, p_lat)


def _hy_gate_kernel(x0_ref, x1_ref, v_ref, cw0_ref, cw1_ref, cwv_ref, cb0_ref, cb1_ref, cbv_ref, u_ref, g_ref):
    l = x0_ref.shape[0]
    t = lax.broadcasted_iota(jnp.int32, (l, 1), 0)

    def conv3(z_ref, cw_ref, cb_ref):
        z = z_ref[...].astype(F32)
        prev = jnp.where(t == 0, 0.0, pltpu.roll(z, 1, axis=0))
        nxt = jnp.where(t == l - 1, 0.0, pltpu.roll(z, l - 1, axis=0))
        cw = cw_ref[...]
        return prev * cw[0:1, :] + z * cw[1:2, :] + nxt * cw[2:3, :] + cb_ref[...]

    x1 = conv3(x1_ref, cw1_ref, cb1_ref)
    v = conv3(v_ref, cwv_ref, cbv_ref)
    u_ref[...] = (v * x1).astype(u_ref.dtype)
    g_ref[...] = conv3(x0_ref, cw0_ref, cb0_ref).astype(g_ref.dtype)


def _hyena_gate(z, conv_w, conv_b, *, tn=256):
    b, l, d3 = z.shape
    d = d3 // 3
    nb = d // tn
    cb = conv_b.reshape(1, d3)
    zs = [pl.BlockSpec((None, l, tn), (lambda bi, j, o=o: (bi, 0, j + o * nb))) for o in range(3)]
    ws = [pl.BlockSpec((3, tn), (lambda bi, j, o=o: (0, j + o * nb))) for o in range(3)]
    bs = [pl.BlockSpec((1, tn), (lambda bi, j, o=o: (0, j + o * nb))) for o in range(3)]
    return pl.pallas_call(
        _hy_gate_kernel,
        grid=(b, nb),
        in_specs=zs + ws + bs,
        out_specs=[pl.BlockSpec((None, l, tn), lambda bi, j: (bi, 0, j))] * 2,
        out_shape=[jax.ShapeDtypeStruct((b, l, d), BF16)] * 2,
        compiler_params=_cparams("parallel", "parallel"),
        name="hyena_conv3_gate",
    )(z, z, z, conv_w, conv_w, conv_w, cb, cb, cb)


def _hy_filter_kernel(z_ref, w1_ref, b1_ref, f1_ref, w2_ref, b2_ref, f2_ref, w3f_ref, w3b_ref, dec_ref, o_ref, h_scr):
    l = z_ref.shape[0]

    @pl.when(pl.program_id(0) == 0)
    def _():
        h1 = jnp.sin(f1_ref[...] * (_dot3(z_ref[...], w1_ref[...]) + b1_ref[...]))
        h_scr[...] = jnp.sin(f2_ref[...] * (_dot3(h1, w2_ref[...]) + b2_ref[...]))

    h = h_scr[...]
    decay = dec_ref[...] + HYENA_SHIFT
    hf = _dot3(h, w3f_ref[...]) * decay
    hb = _dot3(h, w3b_ref[...]) * decay
    t = lax.broadcasted_iota(jnp.int32, (l, 1), 0)
    hb = jnp.where(t == 0, 0.0, hb)
    norm = jnp.sum(jnp.abs(hf), axis=0, keepdims=True) + jnp.sum(jnp.abs(hb), axis=0, keepdims=True)
    o_ref[0] = (hf / norm).astype(o_ref.dtype)
    o_ref[1] = (hb / norm).astype(o_ref.dtype)


def _hyena_filter(l, d, w1, b1, fr1, w2, b2, fr2, w3, *, tn=256):
    t = jnp.linspace(0.0, 1.0, l, dtype=F32)[:, None]
    bands = (HYENA_EMB - 1) // 2
    w = 2.0 * math.pi * jnp.arange(l, dtype=F32)[:, None] / l
    f = jnp.linspace(1e-4, bands - 1, bands, dtype=F32)[None, :]
    z = jnp.concatenate([t, jnp.cos(f * w), -jnp.sin(f * w)], axis=-1)
    deltas = jnp.abs(jnp.linspace(math.log(HYENA_TARGET) / HYENA_SLOW, math.log(HYENA_TARGET) / HYENA_FAST, d, dtype=F32))
    decay = jnp.exp(-t * deltas)
    fw = w1.shape[1]
    pad = lambda a, r, c: jnp.pad(a.astype(F32), ((0, r - a.shape[0]), (0, c - a.shape[1])))
    z = pad(z, l, LANE)
    w1p = pad(w1, LANE, LANE)
    w2p = pad(w2, LANE, LANE)
    w3p = pad(w3, LANE, 2 * d)
    vec = lambda a: pad(a.reshape(1, fw), 1, LANE)
    nb = d // tn
    full = lambda shape: pl.BlockSpec(shape, lambda j: (0, 0))
    return pl.pallas_call(
        _hy_filter_kernel,
        grid=(nb,),
        in_specs=[full((l, LANE)), full((LANE, LANE)), full((1, LANE)), full((1, LANE)), full((LANE, LANE)),
                  full((1, LANE)), full((1, LANE)),
                  pl.BlockSpec((LANE, tn), lambda j: (0, j)),
                  pl.BlockSpec((LANE, tn), lambda j: (0, j + nb)),
                  pl.BlockSpec((l, tn), lambda j: (0, j))],
        out_specs=pl.BlockSpec((2, l, tn), lambda j: (0, 0, j)),
        out_shape=jax.ShapeDtypeStruct((2, l, d), BF16),
        scratch_shapes=[pltpu.VMEM((l, LANE), F32)],
        compiler_params=_cparams("arbitrary"),
        name="hyena_filter",
    )(z, w1p, vec(b1), vec(fr1), w2p, vec(b2), vec(fr2), w3p, w3p, decay)


def _dft_tables(l):
    n = 2 * l
    k = jnp.arange(l, dtype=jnp.int32)
    nb = 1 << ((l.bit_length() - 1 + 1) // 2)
    na = l // nb
    theta = lambda m: (m % n).astype(F32) * (2.0 * math.pi / n)
    ang_a = theta(k[:, None] * (jnp.arange(na, dtype=jnp.int32) * nb)[None, :])[:, :, None]
    ang_b = theta(k[:, None] * jnp.arange(nb, dtype=jnp.int32)[None, :])[:, None, :]
    cs = (jnp.cos(ang_a) * jnp.cos(ang_b) - jnp.sin(ang_a) * jnp.sin(ang_b)).reshape(l, l)
    sn = (jnp.sin(ang_a) * jnp.cos(ang_b) + jnp.cos(ang_a) * jnp.sin(ang_b)).reshape(l, l)
    alt = jnp.where(k % 2 == 0, 1.0, -1.0).astype(F32)
    f_b = jnp.where(k[:, None] == 0, alt[None, :], -sn)
    fwd = jnp.concatenate([cs, f_b], axis=0)
    wa = jnp.where(k[None, :] == 0, 1.0, 2.0) * cs.T
    wb = jnp.where(k[None, :] == 0, alt[:, None], -2.0 * sn.T)
    inv = jnp.concatenate([wa, wb], axis=1) * (1.0 / n)
    return fwd.astype(BF16), inv.astype(BF16)


def _dft_raw_kernel(fa_ref, fb_ref, u_ref, o_ref):
    u = u_ref[...]
    o_ref[0] = _dot(fa_ref[...], u)
    o_ref[1] = _dot(fb_ref[...], u)


def _dft_raw(fwd, u, *, tm=512, tn=512):
    b, l, d = u.shape
    tm, tn = min(tm, l), min(tn, d)
    nb = l // tm
    return pl.pallas_call(
        _dft_raw_kernel,
        grid=(b, d // tn, nb),
        in_specs=[pl.BlockSpec((tm, l), lambda bi, j, i: (i, 0)),
                  pl.BlockSpec((tm, l), lambda bi, j, i: (i + nb, 0)),
                  pl.BlockSpec((None, l, tn), lambda bi, j, i: (bi, 0, j))],
        out_specs=pl.BlockSpec((None, 2, tm, tn), lambda bi, j, i: (bi, 0, i, j)),
        out_shape=jax.ShapeDtypeStruct((b, 2, l, d), F32),
        compiler_params=_cparams("parallel", "parallel", "parallel"),
        name="hyena_dft_filter",
    )(fwd, fwd, u)


def _dft_mul_kernel(fa_ref, fb_ref, u_ref, hs_ref, o_ref):
    i = pl.program_id(0)
    u = u_ref[...]
    ua = _dot(fa_ref[...], u)
    ub = _dot(fb_ref[...], u)
    first = jnp.logical_and(lax.broadcasted_iota(jnp.int32, (ua.shape[0], 1), 0) == 0, i == 0)
    ha = hs_ref[0, 0] + hs_ref[1, 0]
    hb = jnp.where(first, hs_ref[0, 1] + hs_ref[1, 1], hs_ref[0, 1] - hs_ref[1, 1])
    pa = jnp.where(first, ua * ha, ua * ha - ub * hb)
    pb = jnp.where(first, ub * hb, ua * hb + ub * ha)
    o_ref[0] = pa.astype(o_ref.dtype)
    o_ref[1] = pb.astype(o_ref.dtype)


def _dft_mul(fwd, u, hspec, *, tm=512, tn=512):
    b, l, d = u.shape
    tm, tn = min(tm, l), min(tn, d)
    nb = l // tm
    out = pl.pallas_call(
        _dft_mul_kernel,
        grid=(nb, d // tn, b),
        in_specs=[pl.BlockSpec((tm, l), lambda i, j, bi: (i, 0)),
                  pl.BlockSpec((tm, l), lambda i, j, bi: (i + nb, 0)),
                  pl.BlockSpec((None, l, tn), lambda i, j, bi: (bi, 0, j)),
                  pl.BlockSpec((2, 2, tm, tn), lambda i, j, bi: (0, 0, i, j))],
        out_specs=pl.BlockSpec((None, 2, tm, tn), lambda i, j, bi: (bi, 0, i, j)),
        out_shape=jax.ShapeDtypeStruct((b, 2, l, d), BF16),
        compiler_params=_cparams("parallel", "parallel", "parallel"),
        name="hyena_dft_forward",
    )(fwd, fwd, u, hspec)
    return out.reshape(b, 2 * l, d)


def _idft_kernel(g_ref, p_ref, u_ref, x0_ref, skip_ref, o_ref):
    y = _dot(g_ref[...], p_ref[...]) + u_ref[...].astype(F32) * skip_ref[...]
    o_ref[...] = (y * x0_ref[...].astype(F32)).astype(o_ref.dtype)


def _idft_gate(inv, p, u, x0, skip, *, tm=1024, tn=512):
    b, l, d = u.shape
    tm, tn = min(tm, l), min(tn, d)
    return pl.pallas_call(
        _idft_kernel,
        grid=(l // tm, b, d // tn),
        in_specs=[pl.BlockSpec((tm, 2 * l), lambda i, bi, j: (i, 0)),
                  pl.BlockSpec((None, 2 * l, tn), lambda i, bi, j: (bi, 0, j)),
                  pl.BlockSpec((None, tm, tn), lambda i, bi, j: (bi, i, j)),
                  pl.BlockSpec((None, tm, tn), lambda i, bi, j: (bi, i, j)),
                  pl.BlockSpec((1, tn), lambda i, bi, j: (0, j))],
        out_specs=pl.BlockSpec((None, tm, tn), lambda i, bi, j: (bi, i, j)),
        out_shape=jax.ShapeDtypeStruct((b, l, d), BF16),
        compiler_params=_cparams("parallel", "parallel", "parallel"),
        name="hyena_dft_inverse",
    )(inv, p, u, x0, skip.reshape(1, d))


def _hyena_operator(z, conv_w, conv_b, fparams, skip):
    b, l, d3 = z.shape
    d = d3 // 3
    u, x0 = _hyena_gate(z, conv_w, conv_b)
    taps = _hyena_filter(l, d, *fparams)
    fwd, inv = _dft_tables(l)
    hspec = _dft_raw(fwd, taps)
    p = _dft_mul(fwd, u, hspec)
    return _idft_gate(inv, p, u, x0, skip)


def _router_kernel(x_ref, nw_ref, sh_ref, sc_ref, wr_ref, h_ref, lg_ref):
    h = _norm_mod(x_ref[...], nw_ref[...], sh_ref[...], sc_ref[...])
    hh, hl = _split_bf16(h)
    h_ref[...] = hh
    wh, wl = _split_bf16(wr_ref[...])
    r = _dot(hh, jnp.concatenate([wh, wl], axis=1))
    lg_ref[...] = r[:, :LANE] + r[:, LANE:] + _dot(hl, wh)


def _norm_mod_router(x, nw, sh, sc, w_router, *, tm=512):
    b, l, d = x.shape
    n_e = w_router.shape[1]
    e = LANE
    w_router = jnp.pad(w_router.astype(F32), ((0, 0), (0, e - n_e)))
    tm = min(tm, l)
    per_sample = sh.shape[0] != 1
    mod_map = (lambda bi, i: (bi, 0, 0)) if per_sample else (lambda bi, i: (0, 0, 0))
    return pl.pallas_call(
        _router_kernel,
        grid=(b, l // tm),
        in_specs=[
            pl.BlockSpec((None, tm, d), lambda bi, i: (bi, i, 0)),
            pl.BlockSpec((1, d), lambda bi, i: (0, 0)),
            pl.BlockSpec((None, 1, d), mod_map),
            pl.BlockSpec((None, 1, d), mod_map),
            pl.BlockSpec((d, e), lambda bi, i: (0, 0)),
        ],
        out_specs=[pl.BlockSpec((None, tm, d), lambda bi, i: (bi, i, 0)),
                   pl.BlockSpec((None, tm, e), lambda bi, i: (bi, i, 0))],
        out_shape=[jax.ShapeDtypeStruct((b, l, d), BF16), jax.ShapeDtypeStruct((b, l, e), F32)],
        compiler_params=_cparams("parallel", "parallel"),
        name="norm_mod_router",
    )(x, nw.reshape(1, d), sh, sc, w_router)


def _route_kernel(lg_ref, tri_ref, pos_ref, prob_ref, *, cap):
    lg = lg_ref[...]
    m = jnp.max(lg, axis=0, keepdims=True)
    ex = jnp.exp(lg - m)
    probs = ex / jnp.sum(ex, axis=0, keepdims=True)
    bits = lax.bitcast_convert_type(probs, jnp.int32)

    def count(mask):
        return jnp.sum(jnp.where(mask, 1.0, 0.0), axis=1, keepdims=True)

    def step(i, thr):
        trial = thr | lax.shift_left(jnp.int32(1), 30 - i)
        return jnp.where(count(bits >= trial) >= cap, trial, thr)

    thr = lax.fori_loop(0, 31, step, jnp.zeros((lg.shape[0], 1), jnp.int32))
    gt = bits > thr
    eq = bits == thr
    need = cap - count(gt).astype(jnp.int32)
    both = jnp.concatenate([jnp.where(gt, 1.0, 0.0), jnp.where(eq, 1.0, 0.0)], axis=0).astype(BF16)
    csum = _dot(both, tri_ref[...])
    e = lg.shape[0]
    rank_gt = csum[:e].astype(jnp.int32)
    rank_eq = csum[e:].astype(jnp.int32)
    sel = jnp.logical_or(gt, jnp.logical_and(eq, rank_eq < need))
    pos = rank_gt + jnp.minimum(rank_eq, need)
    pos_ref[...] = jnp.where(sel, pos, -1)
    prob_ref[...] = probs


def _route(logits_t, cap):
    b, e, n = logits_t.shape
    idx = jnp.arange(n, dtype=jnp.int32)
    tri = (idx[:, None] < idx[None, :]).astype(BF16)
    return pl.pallas_call(
        functools.partial(_route_kernel, cap=cap),
        grid=(b,),
        in_specs=[pl.BlockSpec((None, e, n), lambda bi: (bi, 0, 0)),
                  pl.BlockSpec((n, n), lambda bi: (0, 0))],
        out_specs=[pl.BlockSpec((None, e, n), lambda bi: (bi, 0, 0))] * 2,
        out_shape=[jax.ShapeDtypeStruct((b, e, n), jnp.int32), jax.ShapeDtypeStruct((b, e, n), F32)],
        compiler_params=_cparams("parallel"),
        name="expert_choice_route",
    )(logits_t, tri)


def _gather_kernel(h_ref, pos_ref, prob_ref, xg_ref, gate_ref, *, cap):
    n = h_ref.shape[0]
    slot = lax.broadcasted_iota(jnp.int32, (cap, n), 0)
    match = slot == pos_ref[...]
    onehot = jnp.where(match, 1.0, 0.0).astype(BF16)
    xg_ref[...] = _dot(onehot, h_ref[...]).astype(xg_ref.dtype)
    gate_ref[...] = jnp.sum(jnp.where(match, prob_ref[...], 0.0), axis=1, keepdims=True)


def _gather_append_kernel(h_ref, pos_ref, prob_ref, xin_ref, gin_ref, xg_ref, gate_ref, *, cap, nb):
    bi = pl.program_id(0)

    @pl.when(bi < nb)
    def _():
        _gather_kernel(h_ref, pos_ref, prob_ref, xg_ref, gate_ref, cap=cap)

    @pl.when(bi >= nb)
    def _():
        xg_ref[...] = xin_ref[...]
        gate_ref[...] = gin_ref[...]


def _moe_gather(h, pos, probs, cap, tail=None):
    b, n, d = h.shape
    e = pos.shape[1]
    row = lambda a: a.reshape(b, e, 1, n)
    if tail is None:
        nx, kern, clamp = 0, functools.partial(_gather_kernel, cap=cap), lambda bi: bi
        tail_specs, tail_args = [], []
    else:
        r = tail[0].shape[1]
        assert r % cap == 0
        nx, kern = r // cap, functools.partial(_gather_append_kernel, cap=cap, nb=b)
        clamp = lambda bi: jnp.minimum(bi, b - 1)
        tail_map = lambda bi, ei: (ei, jnp.maximum(bi - b, 0), 0)
        tail_specs = [pl.BlockSpec((None, cap, d), tail_map), pl.BlockSpec((None, cap, 1), tail_map)]
        tail_args = list(tail)
    m_total = (b + nx) * cap
    return pl.pallas_call(
        kern,
        grid=(b + nx, e),
        in_specs=[pl.BlockSpec((None, n, d), lambda bi, ei: (clamp(bi), 0, 0)),
                  pl.BlockSpec((None, None, 1, n), lambda bi, ei: (clamp(bi), ei, 0, 0)),
                  pl.BlockSpec((None, None, 1, n), lambda bi, ei: (clamp(bi), ei, 0, 0))] + tail_specs,
        out_specs=[pl.BlockSpec((None, cap, d), lambda bi, ei: (ei, bi, 0)),
                   pl.BlockSpec((None, cap, 1), lambda bi, ei: (ei, bi, 0))],
        out_shape=[jax.ShapeDtypeStruct((e, m_total, d), BF16), jax.ShapeDtypeStruct((e, m_total, 1), F32)],
        compiler_params=_cparams("parallel", "parallel"),
        name="moe_gather",
    )(h, row(pos), row(probs), *tail_args)


def _ffn_kernel(x_ref, g_ref, wg_ref, wu_ref, wd_ref, o_ref, acc_ref):
    f = pl.program_id(2)
    x = x_ref[...]
    a = _dot(x, wg_ref[...].astype(BF16))
    u = _dot(x, wu_ref[...].astype(BF16))
    hmid = (_silu(a) * u * g_ref[...]).astype(BF16)
    y = _dot(hmid, wd_ref[...].astype(BF16))
    acc = jnp.where(f == 0, y, acc_ref[...] + y)
    acc_ref[...] = acc
    o_ref[...] = acc.astype(o_ref.dtype)


def _moe_ffn(xg, gate, w_gate, w_up, w_down, layer, *, tf=256):
    e, m, d = xg.shape
    ff = w_gate.shape[3]
    tm = max(t for t in range(16, FFN_MAX_ROWS + 1, 16) if m % t == 0)
    return pl.pallas_call(
        _ffn_kernel,
        grid=(e, m // tm, ff // tf),
        in_specs=[pl.BlockSpec((None, tm, d), lambda ei, i, f: (ei, i, 0)),
                  pl.BlockSpec((None, tm, 1), lambda ei, i, f: (ei, i, 0)),
                  pl.BlockSpec((None, None, d, tf), lambda ei, i, f: (layer, ei, 0, f)),
                  pl.BlockSpec((None, None, d, tf), lambda ei, i, f: (layer, ei, 0, f)),
                  pl.BlockSpec((None, None, tf, d), lambda ei, i, f: (layer, ei, f, 0))],
        out_specs=pl.BlockSpec((None, tm, d), lambda ei, i, f: (ei, i, 0)),
        out_shape=jax.ShapeDtypeStruct((e, m, d), BF16),
        scratch_shapes=[pltpu.VMEM((tm, d), F32)],
        compiler_params=_cparams("parallel", "parallel", "arbitrary", vmem=FFN_VMEM_LIMIT),
        name="moe_expert_ffn",
    )(xg, gate, w_gate, w_up, w_down)


def _combine_kernel(post_ref, y_ref, r_ref, g_ref, *refs, cap, tail):
    if tail == "final":
        fw_ref, o_ref, pt_scr = refs
    elif tail == "next":
        nw_ref, sh_ref, sc_ref, o_ref, h_ref, pt_scr = refs
    else:
        o_ref, pt_scr = refs
    j = pl.program_id(2)
    tm, e = post_ref.shape

    @pl.when(j == 0)
    def _():
        post = post_ref[...]
        if cap % LANE == 0:
            slot = lax.broadcasted_iota(jnp.int32, (tm, cap), 1)
            for ei in range(e):
                pt_scr[:, ei * cap:(ei + 1) * cap] = jnp.where(post[:, ei:ei + 1] == slot, 1.0, 0.0).astype(BF16)
        else:
            slot = lax.broadcasted_iota(jnp.int32, (tm, e * cap), 1)
            hit = jnp.zeros((tm, e * cap), jnp.bool_)
            for ei in range(e):
                tgt = jnp.where(post[:, ei:ei + 1] >= 0, post[:, ei:ei + 1] + ei * cap, -1)
                hit = jnp.logical_or(hit, tgt == slot)
            pt_scr[...] = jnp.where(hit, 1.0, 0.0).astype(BF16)

    y = y_ref[...].reshape(e * cap, y_ref.shape[2])
    out = r_ref[...] + g_ref[...] * _dot(pt_scr[...], y)
    if tail == "final":
        out = out * lax.rsqrt(jnp.mean(out * out, axis=-1, keepdims=True) + EPS) * fw_ref[...]
    o_ref[...] = out
    if tail == "next":
        h_ref[...] = _norm_mod(out, nw_ref[...], sh_ref[...], sc_ref[...]).astype(h_ref.dtype)


def _moe_combine(pos_t, y, row0, resid, gate, cap, final_w=None, next_norm=None, *, tm=1024, tn=512):
    b, n, e = pos_t.shape
    d = y.shape[2]
    assert final_w is None or next_norm is None
    tail = "final" if final_w is not None else "next" if next_norm is not None else None
    if tail:
        tm, tn = 256, d
    tm = min(tm, n)
    blk0 = row0 // cap
    per_sample = gate.shape[0] != 1
    g_map = (lambda bi, i, j: (bi, 0, j)) if per_sample else (lambda bi, i, j: (0, 0, j))
    row_spec = pl.BlockSpec((1, tn), lambda bi, i, j: (0, j))
    out_spec = pl.BlockSpec((None, tm, tn), lambda bi, i, j: (bi, i, j))
    out_specs, out_shape = out_spec, jax.ShapeDtypeStruct((b, n, d), F32)
    extra_specs, extra_args = [], []
    if tail == "final":
        extra_specs, extra_args = [row_spec], [final_w.astype(F32).reshape(1, d)]
    elif tail == "next":
        mod_spec = pl.BlockSpec((None, 1, tn), lambda bi, i, j: (bi, 0, j))
        extra_specs = [row_spec, mod_spec, mod_spec]
        extra_args = [next_norm[0].astype(F32).reshape(1, d), next_norm[1], next_norm[2]]
        out_specs, out_shape = [out_spec, out_spec], [out_shape, jax.ShapeDtypeStruct((b, n, d), BF16)]
    return pl.pallas_call(
        functools.partial(_combine_kernel, cap=cap, tail=tail),
        grid=(b, n // tm, d // tn),
        in_specs=[pl.BlockSpec((None, tm, e), lambda bi, i, j: (bi, i, 0)),
                  pl.BlockSpec((e, cap, tn), lambda bi, i, j: (0, blk0 + bi, j)),
                  out_spec,
                  pl.BlockSpec((None, 1, tn), g_map)] + extra_specs,
        out_specs=out_specs,
        out_shape=out_shape,
        scratch_shapes=[pltpu.VMEM((tm, e * cap), BF16)],
        compiler_params=_cparams("parallel", "parallel", "arbitrary", vmem=FFN_VMEM_LIMIT if tail else VMEM_LIMIT),
        name="moe_combine_residual",
    )(pos_t, y, resid, gate, *extra_args)


def _moe_block(streams, nw, w_router, w_gate, w_up, w_down, layer, final_w=None, next_norm=None):
    e = w_router.shape[1]
    caps = [CAPACITY_FACTOR * s[0].shape[1] // e for s in streams]
    rows = [s[0].shape[0] * cap for s, cap in zip(streams, caps)]
    row0 = [sum(rows[:t]) for t in range(len(streams))]
    routed, buffers = [None] * len(streams), None
    for t in reversed(range(len(streams))):
        x, sh, sc, _ = streams[t]
        h, logits = _norm_mod_router(x, nw, sh, sc, w_router)
        pos, probs = _route(jnp.swapaxes(logits[..., :e], 1, 2), caps[t])
        buffers = _moe_gather(h, pos, probs, caps[t], buffers)
        routed[t] = jnp.swapaxes(pos, 1, 2)
    y = _moe_ffn(buffers[0], buffers[1], w_gate, w_up, w_down, layer)
    return [_moe_combine(routed[t], y, row0[t], streams[t][0], streams[t][3], caps[t],
                         final_w if t == 0 else None, next_norm if t == 0 else None)
            for t in range(len(streams))]


def _rope_tables(seq_len, dim):
    rows = seq_len // GRID_W
    row_id = jnp.repeat(jnp.arange(rows, dtype=F32), GRID_W)
    col_id = jnp.tile(jnp.arange(GRID_W, dtype=F32), rows)
    nf = dim // 4
    inv = ROPE_THETA ** (-jnp.arange(nf, dtype=F32) / nf)
    ang_r = row_id[:, None] * inv
    ang_c = col_id[:, None] * inv
    cos = jnp.concatenate([jnp.cos(ang_r)] * 2 + [jnp.cos(ang_c)] * 2, axis=-1)
    sin = jnp.concatenate([-jnp.sin(ang_r), jnp.sin(ang_r), -jnp.sin(ang_c), jnp.sin(ang_c)], axis=-1)
    return cos, sin


def _identity_tables(seq_len, dim):
    return jnp.ones((seq_len, dim), F32), jnp.zeros((seq_len, dim), F32)


def _attention_mixer(x, ctx, nw, mods_lat, mods_ctx, w_qkv, q_norm, k_norm, w_o, layer, with_ctx_out, h_lat=None):
    hd = ATTN_HEAD_DIM
    nq, nkv = ATTN_HEADS * hd, ATTN_KV_HEADS * hd
    l, lc = x.shape[1], ctx.shape[1]
    tn = 512
    colw = jnp.concatenate([jnp.tile(q_norm.astype(F32) * (hd ** -0.5), ATTN_HEADS),
                            jnp.tile(k_norm.astype(F32), ATTN_KV_HEADS), jnp.ones((nkv,), F32)]).reshape(1, -1)
    cos, sin = _rope_tables(l, hd)
    icos, isin = _identity_tables(lc, hd)
    kw = dict(colw=colw, rope_blocks=(nq + nkv) // tn, head_norm=True, half=hd // 4, tn=tn)
    p_lat = _project(x, nw, mods_lat[0], mods_lat[1], w_qkv, layer, h=h_lat, cos=cos, sin=sin, **kw)
    off = 0 if with_ctx_out else nq // tn
    p_ctx = _project(ctx, nw, mods_ctx[0], mods_ctx[1], w_qkv, layer, cos=icos, sin=isin, col_off=off, **kw)
    kc0 = nq - off * tn
    ctx_kv = (p_ctx, kc0, kc0 + nkv)
    o_lat = _attention(p_lat, [ctx_kv, (p_lat, nq, nq + nkv)])
    x = _out_project(o_lat, w_o, layer, x, mods_lat[2])
    if with_ctx_out:
        ctx = _out_project(_attention(p_ctx, [ctx_kv]), w_o, layer, ctx, mods_ctx[2])
    return x, ctx


def _retention_mixer(x, ctx, nw, mods_lat, mods_ctx, w_in, decay_logit, w_o, layer, with_ctx_out, h_lat=None):
    dk, hh = RET_QK_DIM, RET_HEADS
    l, lc = x.shape[1], ctx.shape[1]
    tn = 512
    log_g = jax.nn.log_sigmoid(decay_logit.astype(F32))
    colw = jnp.concatenate([jnp.ones((hh * dk,), F32), jnp.full((hh * dk,), dk ** -0.5, F32),
                            jnp.ones((w_in.shape[2] - 2 * hh * dk,), F32)]).reshape(1, -1)
    rope_blocks = 2 * hh * dk // tn
    cos, sin = _rope_tables(l, dk)
    icos, isin = _identity_tables(lc, dk)
    kw = dict(colw=colw, rope_blocks=rope_blocks, head_norm=False, half=dk // 4, tn=tn)
    p_lat = _project(x, nw, mods_lat[0], mods_lat[1], w_in, layer, h=h_lat, cos=cos, sin=sin, **kw)
    p_ctx = _project(ctx, nw, mods_ctx[0], mods_ctx[1], w_in, layer, cos=icos, sin=isin, **kw)
    r_ctx, r_lat = _retention(log_g, p_ctx, p_lat, with_ctx_out)
    x = _out_project(r_lat, w_o, layer, x, mods_lat[2])
    if with_ctx_out:
        ctx = _out_project(r_ctx, w_o, layer, ctx, mods_ctx[2])
    return x, ctx


def _hyena_mixer(x, ctx, nw, mods_lat, mods_ctx, w_in, conv_w, conv_b, fparams, skip, w_out, layer, with_ctx_out,
                 h_lat=None):
    z = _project(x, nw, mods_lat[0], mods_lat[1], w_in, layer, h=h_lat)
    x = _out_project(_hyena_operator(z, conv_w, conv_b, fparams, skip), w_out, layer, x, mods_lat[2])
    if with_ctx_out:
        zc = _project(ctx, nw, mods_ctx[0], mods_ctx[1], w_in, layer)
        ctx = _out_project(_hyena_operator(zc, conv_w, conv_b, fparams, skip), w_out, layer, ctx, mods_ctx[2])
    return x, ctx


def kernel(x, c, ctx, c_ctx, w_mod, b_mod, norm_w, attn_w_qkv, attn_q_norm, attn_k_norm, attn_w_o, ret_w_in, ret_decay_logit, ret_w_o, hy_w_in, hy_conv_w, hy_conv_b, hy_f_w1, hy_f_b1, hy_f_freq1, hy_f_w2, hy_f_b2, hy_f_freq2, hy_f_w3, hy_skip, hy_w_out, moe_router, moe_w_gate, moe_w_up, moe_w_down, final_norm_w):
    depth = w_mod.shape[0]
    b, _, d = x.shape
    rows = -(-(b + 1) // 8) * 8
    c_all = jnp.concatenate([c, c_ctx[None, :], jnp.zeros((rows - b - 1, d), F32)], axis=0)
    mod = _modulation(c_all, w_mod, b_mod)

    def layer_mods(i):
        m = mod[i].reshape(rows, N_MOD, d)
        lat = [m[:b, t][:, None, :] for t in range(N_MOD)]
        ctx_ = [m[b:b + 1, t][:, None, :] for t in range(N_MOD)]
        return lat, ctx_

    h_lat = None
    for i in range(depth):
        kind, j = i % 3, i // 3
        last = i == depth - 1
        with_ctx = not last
        mods_lat, mods_ctx = layer_mods(i)
        nw1, nw2 = norm_w[i, 0], norm_w[i, 1]
        if kind == 0:
            x, ctx = _attention_mixer(x, ctx, nw1, mods_lat, mods_ctx, attn_w_qkv, attn_q_norm[j], attn_k_norm[j],
                                      attn_w_o, j, with_ctx, h_lat)
        elif kind == 1:
            x, ctx = _retention_mixer(x, ctx, nw1, mods_lat, mods_ctx, ret_w_in, ret_decay_logit[j], ret_w_o,
                                      j, with_ctx, h_lat)
        else:
            fparams = (hy_f_w1[j], hy_f_b1[j], hy_f_freq1[j], hy_f_w2[j], hy_f_b2[j], hy_f_freq2[j], hy_f_w3[j])
            x, ctx = _hyena_mixer(x, ctx, nw1, mods_lat, mods_ctx, hy_w_in, hy_conv_w[j], hy_conv_b[j], fparams,
                                  hy_skip[j], hy_w_out, j, with_ctx, h_lat)
        streams = [(x, mods_lat[3], mods_lat[4], mods_lat[5])]
        if with_ctx:
            streams.append((ctx, mods_ctx[3], mods_ctx[4], mods_ctx[5]))
        if last:
            tails = dict(final_w=final_norm_w)
        else:
            nxt = layer_mods(i + 1)[0]
            tails = dict(next_norm=(norm_w[i + 1, 0], nxt[0], nxt[1]))
        outs = _moe_block(streams, nw2, moe_router[i], moe_w_gate, moe_w_up, moe_w_down, i, **tails)
        if last:
            x = outs[0]
        else:
            (x, h_lat), ctx = outs
    return x
```

```python
import functools
import math

import jax
import jax.numpy as jnp
from jax import lax
from jax.experimental import pallas as pl
from jax.experimental.pallas import tpu as pltpu

F32 = jnp.float32
BF16 = jnp.bfloat16

EPS = 1e-6
GRID_W = 64
ROPE_THETA = 10000.0
ATTN_HEADS = 16
ATTN_KV_HEADS = 4
ATTN_HEAD_DIM = 128
RET_HEADS = 8
RET_QK_DIM = 256
RET_V_DIM = 512
N_EXPERTS = 16
CAPACITY_FACTOR = 2
N_MOD = 6
HYENA_EMB = 33
HYENA_TARGET = 1e-2
HYENA_FAST = 0.3
HYENA_SLOW = 1.5
HYENA_SHIFT = 0.0

LANE = 128
VMEM_LIMIT = 52 * 1024 * 1024
FFN_VMEM_LIMIT = 58 * 1024 * 1024
FFN_MAX_ROWS = 1152


def _cparams(*sem, vmem=VMEM_LIMIT):
    return pltpu.CompilerParams(dimension_semantics=sem, vmem_limit_bytes=vmem)


def _silu(v):
    return v / (1.0 + jnp.exp(-v))


def _split_bf16(v):
    hi = v.astype(BF16)
    lo = (v - hi.astype(F32)).astype(BF16)
    return hi, lo


def _dot(a, b):
    return jnp.dot(a, b, preferred_element_type=F32)


def _dot3(a, b):
    ah, al = _split_bf16(a)
    bh, bl = _split_bf16(b)
    return _dot(ah, bh) + _dot(al, bh) + _dot(ah, bl)


def _dot_nt(a, b):
    return lax.dot_general(a, b, (((1,), (1,)), ((), ())), preferred_element_type=F32)


def _dot_tn(a, b):
    return lax.dot_general(a, b, (((0,), (0,)), ((), ())), preferred_element_type=F32)


def _mod_kernel(c_ref, w_ref, b_ref, o_ref):
    a = _silu(c_ref[...])
    o_ref[...] = _dot3(a, w_ref[...]) + b_ref[...]


def _modulation(c_all, w_mod, b_mod):
    depth, d, n = w_mod.shape
    rows = c_all.shape[0]
    tn = 1024
    return pl.pallas_call(
        _mod_kernel,
        grid=(depth, n // tn),
        in_specs=[
            pl.BlockSpec((rows, d), lambda i, j: (0, 0)),
            pl.BlockSpec((None, d, tn), lambda i, j: (i, 0, j)),
            pl.BlockSpec((None, 1, tn), lambda i, j: (i, 0, j)),
        ],
        out_specs=pl.BlockSpec((None, rows, tn), lambda i, j: (i, 0, j)),
        out_shape=jax.ShapeDtypeStruct((depth, rows, n), F32),
        compiler_params=_cparams("parallel", "parallel"),
        name="modulation",
    )(c_all, w_mod, b_mod.reshape(depth, 1, n))


def _norm_mod(x, nw, sh, sc):
    ms = jnp.mean(x * x, axis=-1, keepdims=True)
    y = x * lax.rsqrt(ms + EPS) * nw
    return y * (1.0 + sc) + sh


def _rope_slices(acc, colw, cos, sin, *, head_norm, half, table_w):
    tn = acc.shape[1]
    lane = lax.broadcasted_iota(jnp.int32, (1, LANE), 1)
    first_half = (lane % (2 * half)) < half
    outs = []
    for s in range(tn // LANE):
        xs = acc[:, s * LANE:(s + 1) * LANE]
        if head_norm:
            xs = xs * lax.rsqrt(jnp.mean(xs * xs, axis=-1, keepdims=True) + EPS)
        xs = xs * colw[:, s * LANE:(s + 1) * LANE]
        t0 = (s * LANE) % table_w
        cs = cos[:, t0:t0 + LANE]
        sn = sin[:, t0:t0 + LANE]
        if 2 * half == LANE:
            partner = pltpu.roll(xs, half, axis=1)
        else:
            partner = jnp.where(first_half, pltpu.roll(xs, LANE - half, axis=1), pltpu.roll(xs, half, axis=1))
        outs.append(xs * cs + partner * sn)
    return jnp.concatenate(outs, axis=1)


def _norm_pass_kernel(x_ref, nw_ref, sh_ref, sc_ref, o_ref):
    o_ref[...] = _norm_mod(x_ref[...], nw_ref[...], sh_ref[...], sc_ref[...]).astype(o_ref.dtype)


def _norm_mod_pass(x, nw, sh, sc, *, tm=512):
    b, l, d = x.shape
    tm = min(tm, l)
    per_sample = sh.shape[0] != 1
    mod_map = (lambda bi, i: (bi, 0, 0)) if per_sample else (lambda bi, i: (0, 0, 0))
    return pl.pallas_call(
        _norm_pass_kernel,
        grid=(b, l // tm),
        in_specs=[pl.BlockSpec((None, tm, d), lambda bi, i: (bi, i, 0)),
                  pl.BlockSpec((1, d), lambda bi, i: (0, 0)),
                  pl.BlockSpec((None, 1, d), mod_map),
                  pl.BlockSpec((None, 1, d), mod_map)],
        out_specs=pl.BlockSpec((None, tm, d), lambda bi, i: (bi, i, 0)),
        out_shape=jax.ShapeDtypeStruct((b, l, d), BF16),
        compiler_params=_cparams("parallel", "parallel"),
        name="norm_modulate",
    )(x, nw.reshape(1, d), sh, sc)


def _proj_kernel(h_ref, w_ref, colw_ref, cos_ref, sin_ref, o_ref, *, rope_blocks, col_off, head_norm, half, table_w):
    j = pl.program_id(2)
    acc = _dot(h_ref[...], w_ref[...].astype(BF16))
    o_ref[...] = acc.astype(o_ref.dtype)
    if rope_blocks:
        @pl.when(j + col_off < rope_blocks)
        def _():
            o_ref[...] = _rope_slices(acc, colw_ref[...], cos_ref[...], sin_ref[...], head_norm=head_norm,
                                      half=half, table_w=table_w).astype(o_ref.dtype)


def _project(x, nw, sh, sc, w, layer, *, h=None, colw=None, cos=None, sin=None, rope_blocks=0, col_off=0,
             head_norm=False, half=32, tn=512, tm=2048):
    b, l, d = x.shape
    n = w.shape[2] - col_off * tn
    per_sample = sh.shape[0] != 1
    if colw is None:
        colw = jnp.ones((1, w.shape[2]), F32)
        cos = jnp.ones((l, LANE), F32)
        sin = jnp.zeros((l, LANE), F32)
    if not per_sample and b > 1:
        out = _project(x.reshape(1, b * l, d), nw, sh, sc, w, layer, colw=colw, cos=jnp.tile(cos, (b, 1)),
                       sin=jnp.tile(sin, (b, 1)), rope_blocks=rope_blocks, col_off=col_off, head_norm=head_norm,
                       half=half, tn=tn, tm=tm)
        return out.reshape(b, l, n)
    tm = min(tm, l)
    table_w = cos.shape[1]
    kern = functools.partial(_proj_kernel, rope_blocks=rope_blocks, col_off=col_off, head_norm=head_norm,
                             half=half, table_w=table_w)
    return pl.pallas_call(
        kern,
        grid=(b, l // tm, n // tn),
        in_specs=[
            pl.BlockSpec((None, tm, d), lambda bi, i, j: (bi, i, 0)),
            pl.BlockSpec((None, d, tn), lambda bi, i, j: (layer, 0, j + col_off)),
            pl.BlockSpec((1, tn), lambda bi, i, j: (0, j + col_off)),
            pl.BlockSpec((tm, table_w), lambda bi, i, j: (i, 0)),
            pl.BlockSpec((tm, table_w), lambda bi, i, j: (i, 0)),
        ],
        out_specs=pl.BlockSpec((None, tm, tn), lambda bi, i, j: (bi, i, j)),
        out_shape=jax.ShapeDtypeStruct((b, l, n), BF16),
        compiler_params=_cparams("parallel", "parallel", "parallel"),
        name="project",
    )(_norm_mod_pass(x, nw, sh, sc) if h is None else h, w, colw, cos, sin)


def _out_proj_kernel(a_ref, w_ref, r_ref, g_ref, o_ref):
    acc = _dot(a_ref[...], w_ref[...].astype(BF16))
    o_ref[...] = r_ref[...] + g_ref[...] * acc


def _out_project(a, w, layer, resid, gate, *, tm=None, tn=512):
    b, l, k = a.shape
    n = w.shape[2]
    if tm is None:
        tm = (2048 * 2048) // k
    per_sample = gate.shape[0] != 1
    if not per_sample and b > 1:
        out = _out_project(a.reshape(1, b * l, k), w, layer, resid.reshape(1, b * l, n), gate, tm=tm, tn=tn)
        return out.reshape(b, l, n)
    tm = min(tm, l)
    if l // tm > 1:
        grid = (n // tn, b, l // tm)
        at = lambda f: (lambda j, bi, i: f(bi, i, j))
    else:
        grid = (b, l // tm, n // tn)
        at = lambda f: f
    return pl.pallas_call(
        _out_proj_kernel,
        grid=grid,
        in_specs=[
            pl.BlockSpec((None, tm, k), at(lambda bi, i, j: (bi, i, 0))),
            pl.BlockSpec((None, k, tn), at(lambda bi, i, j: (layer, 0, j))),
            pl.BlockSpec((None, tm, tn), at(lambda bi, i, j: (bi, i, j))),
            pl.BlockSpec((None, 1, tn), at(lambda bi, i, j: (bi if per_sample else 0, 0, j))),
        ],
        out_specs=pl.BlockSpec((None, tm, tn), at(lambda bi, i, j: (bi, i, j))),
        out_shape=jax.ShapeDtypeStruct((b, l, n), F32),
        compiler_params=_cparams("parallel", "parallel", "parallel"),
        name="out_project_residual",
    )(a, w, resid, gate)


def _attn_kernel(q_ref, *refs, groups, n_kv):
    hd = ATTN_HEAD_DIM
    o_ref = refs[2 * n_kv]
    ks = [refs[2 * t][...] for t in range(n_kv)]
    vs = [jnp.concatenate([refs[2 * t + 1][...], jnp.ones((k.shape[0], hd), BF16)], axis=1) for t, k in enumerate(ks)]
    for g in range(groups):
        q = q_ref[:, g * hd:(g + 1) * hd]
        ss = [_dot_nt(q, k) for k in ks]
        m = functools.reduce(jnp.maximum, [jnp.max(s, axis=-1, keepdims=True) for s in ss])
        ps = [jnp.exp(s - m).astype(BF16) for s in ss]
        ov = sum(_dot(p, v) for p, v in zip(ps, vs))
        o_ref[:, g * hd:(g + 1) * hd] = (ov[:, :hd] / ov[:, hd:hd + 1]).astype(o_ref.dtype)


def _attention(pq, kv_sources, *, tq=1024):
    b, lq, _ = pq.shape
    hd = ATTN_HEAD_DIM
    groups = ATTN_HEADS // ATTN_KV_HEADS
    gw = groups * hd
    tq = min(tq, lq)
    in_specs = [pl.BlockSpec((None, tq, gw), lambda bi, h, i: (bi, i, h))]
    args = [pq]
    for p, k0, v0 in kv_sources:
        lk = p.shape[1]
        in_specs.append(pl.BlockSpec((None, lk, hd), lambda bi, h, i, o=k0 // hd: (bi, 0, o + h)))
        in_specs.append(pl.BlockSpec((None, lk, hd), lambda bi, h, i, o=v0 // hd: (bi, 0, o + h)))
        args += [p, p]
    return pl.pallas_call(
        functools.partial(_attn_kernel, groups=groups, n_kv=len(kv_sources)),
        grid=(b, ATTN_KV_HEADS, lq // tq),
        in_specs=in_specs,
        out_specs=pl.BlockSpec((None, tq, gw), lambda bi, h, i: (bi, i, h)),
        out_shape=jax.ShapeDtypeStruct((b, lq, ATTN_HEADS * hd), BF16),
        compiler_params=_cparams("parallel", "parallel", "parallel"),
        name="gqa_attention",
    )(*args)


RET_CHUNK = 256


def _ret_readout(o, g):
    of = o * lax.rsqrt(jnp.mean(o * o, axis=-1, keepdims=True) + EPS)
    return (_silu(g.astype(F32)) * of).astype(BF16)


def _ret_kernel(lg_ref, qc_ref, kc_ref, vc_ref, gc_ref, ql_ref, kl_ref, vl_ref, gl_ref, oc_ref, ol_ref,
                of_scr, ob_scr, sf_scr, sb_scr, *, n_chunks, with_ctx_out):
    c = RET_CHUNK
    h = pl.program_id(1)
    lgf = lg_ref[0, h]
    lgb = lg_ref[1, h]
    row = lax.broadcasted_iota(jnp.int32, (c, 1), 0).astype(F32)
    col = lax.broadcasted_iota(jnp.int32, (1, c), 1).astype(F32)
    diff = row - col
    dmask = jnp.exp(jnp.where(diff >= 0, diff * lgf, -diff * lgb))
    qdec_f = jnp.exp((row + 1.0) * lgf)
    kdec_f = jnp.exp((c - 1.0 - row) * lgf)
    qdec_b = jnp.exp((c - row) * lgb)
    kdec_b = jnp.exp(row * lgb)
    one = jnp.ones((1, 1), F32)
    cdec_f = jnp.exp(one * (c * lgf))
    cdec_b = jnp.exp(one * (c * lgb))

    qc = qc_ref[...]
    kc = kc_ref[...].astype(F32)
    vc = vc_ref[...]
    sf_scr[...] = _dot_tn((kc * kdec_f).astype(BF16), vc)
    sb_scr[...] = _dot_tn((kc * kdec_b).astype(BF16), vc)
    if with_ctx_out:
        inner = (_dot_nt(qc, kc_ref[...]) * dmask).astype(BF16)
        oc_ref[...] = _ret_readout(_dot(inner, vc), gc_ref[...])
    else:
        oc_ref[...] = jnp.zeros(oc_ref.shape, oc_ref.dtype)

    def scan(i, carry):
        sl = pl.ds(pl.multiple_of(i * c, c), c)
        qb = ql_ref[sl, :]
        kb = kl_ref[sl, :]
        v = vl_ref[sl, :]
        inner = (_dot_nt(qb, kb) * dmask).astype(BF16)
        of_scr[sl, :] = _dot(inner, v) + _dot((qb.astype(F32) * qdec_f).astype(BF16), sf_scr[...].astype(BF16))
        sf_scr[...] = sf_scr[...] * cdec_f + _dot_tn((kb.astype(F32) * kdec_f).astype(BF16), v)

        sr = pl.ds(pl.multiple_of((n_chunks - 1 - i) * c, c), c)
        q = ql_ref[sr, :].astype(F32)
        k = kl_ref[sr, :].astype(F32)
        ob_scr[sr, :] = _dot((q * qdec_b).astype(BF16), sb_scr[...].astype(BF16))
        sb_scr[...] = sb_scr[...] * cdec_b + _dot_tn((k * kdec_b).astype(BF16), vl_ref[sr, :])
        return carry

    lax.fori_loop(0, n_chunks, scan, 0)

    def readout(i, carry):
        sl = pl.ds(pl.multiple_of(i * c, c), c)
        ol_ref[sl, :] = _ret_readout(of_scr[sl, :] + ob_scr[sl, :], gl_ref[sl, :])
        return carry

    lax.fori_loop(0, n_chunks, readout, 0)


def _retention(log_g, p_ctx, p_lat, with_ctx_out):
    b, l, _ = p_lat.shape
    lc = p_ctx.shape[1]
    assert lc == RET_CHUNK and l % RET_CHUNK == 0
    dk, dv, hh = RET_QK_DIM, RET_V_DIM, RET_HEADS
    k_off = hh * dk // dk
    v_off = 2 * hh * dk // dv
    g_off = v_off + hh

    def specs(ln):
        return [
            pl.BlockSpec((None, ln, dk), lambda bi, h: (bi, 0, h)),
            pl.BlockSpec((None, ln, dk), lambda bi, h: (bi, 0, k_off + h)),
            pl.BlockSpec((None, ln, dv), lambda bi, h: (bi, 0, v_off + h)),
            pl.BlockSpec((None, ln, dv), lambda bi, h: (bi, 0, g_off + h)),
        ]

    return pl.pallas_call(
        functools.partial(_ret_kernel, n_chunks=l // RET_CHUNK, with_ctx_out=with_ctx_out),
        grid=(b, hh),
        in_specs=[pl.BlockSpec(memory_space=pltpu.SMEM)] + specs(lc) + specs(l),
        out_specs=[
            pl.BlockSpec((None, lc, dv), lambda bi, h: (bi, 0, h)),
            pl.BlockSpec((None, l, dv), lambda bi, h: (bi, 0, h)),
        ],
        out_shape=[
            jax.ShapeDtypeStruct((b, lc, hh * dv), BF16),
            jax.ShapeDtypeStruct((b, l, hh * dv), BF16),
        ],
        scratch_shapes=[pltpu.VMEM((l, dv), F32), pltpu.VMEM((l, dv), F32), pltpu.VMEM((dk, dv), F32),
                        pltpu.VMEM((dk, dv), F32)],
        compiler_params=_cparams("parallel", "parallel"),
        name="retention",
    )(log_g, p_ctx, p_ctx, p_ctx, p_ctx, p_lat, p_lat, p_lat, p_lat)


def _hy_gate_kernel(x0_ref, x1_ref, v_ref, cw0_ref, cw1_ref, cwv_ref, cb0_ref, cb1_ref, cbv_ref, u_ref, g_ref):
    l = x0_ref.shape[0]
    t = lax.broadcasted_iota(jnp.int32, (l, 1), 0)

    def conv3(z_ref, cw_ref, cb_ref):
        z = z_ref[...].astype(F32)
        prev = jnp.where(t == 0, 0.0, pltpu.roll(z, 1, axis=0))
        nxt = jnp.where(t == l - 1, 0.0, pltpu.roll(z, l - 1, axis=0))
        cw = cw_ref[...]
        return prev * cw[0:1, :] + z * cw[1:2, :] + nxt * cw[2:3, :] + cb_ref[...]

    x1 = conv3(x1_ref, cw1_ref, cb1_ref)
    v = conv3(v_ref, cwv_ref, cbv_ref)
    u_ref[...] = (v * x1).astype(u_ref.dtype)
    g_ref[...] = conv3(x0_ref, cw0_ref, cb0_ref).astype(g_ref.dtype)


def _hyena_gate(z, conv_w, conv_b, *, tn=256):
    b, l, d3 = z.shape
    d = d3 // 3
    nb = d // tn
    cb = conv_b.reshape(1, d3)
    zs = [pl.BlockSpec((None, l, tn), (lambda bi, j, o=o: (bi, 0, j + o * nb))) for o in range(3)]
    ws = [pl.BlockSpec((3, tn), (lambda bi, j, o=o: (0, j + o * nb))) for o in range(3)]
    bs = [pl.BlockSpec((1, tn), (lambda bi, j, o=o: (0, j + o * nb))) for o in range(3)]
    return pl.pallas_call(
        _hy_gate_kernel,
        grid=(b, nb),
        in_specs=zs + ws + bs,
        out_specs=[pl.BlockSpec((None, l, tn), lambda bi, j: (bi, 0, j))] * 2,
        out_shape=[jax.ShapeDtypeStruct((b, l, d), BF16)] * 2,
        compiler_params=_cparams("parallel", "parallel"),
        name="hyena_conv3_gate",
    )(z, z, z, conv_w, conv_w, conv_w, cb, cb, cb)


def _hy_filter_kernel(z_ref, w1_ref, b1_ref, f1_ref, w2_ref, b2_ref, f2_ref, w3f_ref, w3b_ref, dec_ref, o_ref, h_scr):
    l = z_ref.shape[0]

    @pl.when(pl.program_id(0) == 0)
    def _():
        h1 = jnp.sin(f1_ref[...] * (_dot3(z_ref[...], w1_ref[...]) + b1_ref[...]))
        h_scr[...] = jnp.sin(f2_ref[...] * (_dot3(h1, w2_ref[...]) + b2_ref[...]))

    h = h_scr[...]
    decay = dec_ref[...] + HYENA_SHIFT
    hf = _dot3(h, w3f_ref[...]) * decay
    hb = _dot3(h, w3b_ref[...]) * decay
    t = lax.broadcasted_iota(jnp.int32, (l, 1), 0)
    hb = jnp.where(t == 0, 0.0, hb)
    norm = jnp.sum(jnp.abs(hf), axis=0, keepdims=True) + jnp.sum(jnp.abs(hb), axis=0, keepdims=True)
    o_ref[0] = (hf / norm).astype(o_ref.dtype)
    o_ref[1] = (hb / norm).astype(o_ref.dtype)


def _hyena_filter(l, d, w1, b1, fr1, w2, b2, fr2, w3, *, tn=256):
    t = jnp.linspace(0.0, 1.0, l, dtype=F32)[:, None]
    bands = (HYENA_EMB - 1) // 2
    w = 2.0 * math.pi * jnp.arange(l, dtype=F32)[:, None] / l
    f = jnp.linspace(1e-4, bands - 1, bands, dtype=F32)[None, :]
    z = jnp.concatenate([t, jnp.cos(f * w), -jnp.sin(f * w)], axis=-1)
    deltas = jnp.abs(jnp.linspace(math.log(HYENA_TARGET) / HYENA_SLOW, math.log(HYENA_TARGET) / HYENA_FAST, d, dtype=F32))
    decay = jnp.exp(-t * deltas)
    fw = w1.shape[1]
    pad = lambda a, r, c: jnp.pad(a.astype(F32), ((0, r - a.shape[0]), (0, c - a.shape[1])))
    z = pad(z, l, LANE)
    w1p = pad(w1, LANE, LANE)
    w2p = pad(w2, LANE, LANE)
    w3p = pad(w3, LANE, 2 * d)
    vec = lambda a: pad(a.reshape(1, fw), 1, LANE)
    nb = d // tn
    full = lambda shape: pl.BlockSpec(shape, lambda j: (0, 0))
    return pl.pallas_call(
        _hy_filter_kernel,
        grid=(nb,),
        in_specs=[full((l, LANE)), full((LANE, LANE)), full((1, LANE)), full((1, LANE)), full((LANE, LANE)),
                  full((1, LANE)), full((1, LANE)),
                  pl.BlockSpec((LANE, tn), lambda j: (0, j)),
                  pl.BlockSpec((LANE, tn), lambda j: (0, j + nb)),
                  pl.BlockSpec((l, tn), lambda j: (0, j))],
        out_specs=pl.BlockSpec((2, l, tn), lambda j: (0, 0, j)),
        out_shape=jax.ShapeDtypeStruct((2, l, d), BF16),
        scratch_shapes=[pltpu.VMEM((l, LANE), F32)],
        compiler_params=_cparams("arbitrary"),
        name="hyena_filter",
    )(z, w1p, vec(b1), vec(fr1), w2p, vec(b2), vec(fr2), w3p, w3p, decay)


def _dft_tables(l):
    n = 2 * l
    k = jnp.arange(l, dtype=jnp.int32)
    nb = 1 << ((l.bit_length() - 1 + 1) // 2)
    na = l // nb
    theta = lambda m: (m % n).astype(F32) * (2.0 * math.pi / n)
    ang_a = theta(k[:, None] * (jnp.arange(na, dtype=jnp.int32) * nb)[None, :])[:, :, None]
    ang_b = theta(k[:, None] * jnp.arange(nb, dtype=jnp.int32)[None, :])[:, None, :]
    cs = (jnp.cos(ang_a) * jnp.cos(ang_b) - jnp.sin(ang_a) * jnp.sin(ang_b)).reshape(l, l)
    sn = (jnp.sin(ang_a) * jnp.cos(ang_b) + jnp.cos(ang_a) * jnp.sin(ang_b)).reshape(l, l)
    alt = jnp.where(k % 2 == 0, 1.0, -1.0).astype(F32)
    f_b = jnp.where(k[:, None] == 0, alt[None, :], -sn)
    fwd = jnp.concatenate([cs, f_b], axis=0)
    wa = jnp.where(k[None, :] == 0, 1.0, 2.0) * cs.T
    wb = jnp.where(k[None, :] == 0, alt[:, None], -2.0 * sn.T)
    inv = jnp.concatenate([wa, wb], axis=1) * (1.0 / n)
    return fwd.astype(BF16), inv.astype(BF16)


def _dft_raw_kernel(fa_ref, fb_ref, u_ref, o_ref):
    u = u_ref[...]
    o_ref[0] = _dot(fa_ref[...], u)
    o_ref[1] = _dot(fb_ref[...], u)


def _dft_raw(fwd, u, *, tm=512, tn=512):
    b, l, d = u.shape
    tm, tn = min(tm, l), min(tn, d)
    nb = l // tm
    return pl.pallas_call(
        _dft_raw_kernel,
        grid=(b, d // tn, nb),
        in_specs=[pl.BlockSpec((tm, l), lambda bi, j, i: (i, 0)),
                  pl.BlockSpec((tm, l), lambda bi, j, i: (i + nb, 0)),
                  pl.BlockSpec((None, l, tn), lambda bi, j, i: (bi, 0, j))],
        out_specs=pl.BlockSpec((None, 2, tm, tn), lambda bi, j, i: (bi, 0, i, j)),
        out_shape=jax.ShapeDtypeStruct((b, 2, l, d), F32),
        compiler_params=_cparams("parallel", "parallel", "parallel"),
        name="hyena_dft_filter",
    )(fwd, fwd, u)


def _dft_mul_kernel(fa_ref, fb_ref, u_ref, hs_ref, o_ref):
    i = pl.program_id(0)
    u = u_ref[...]
    ua = _dot(fa_ref[...], u)
    ub = _dot(fb_ref[...], u)
    first = jnp.logical_and(lax.broadcasted_iota(jnp.int32, (ua.shape[0], 1), 0) == 0, i == 0)
    ha = hs_ref[0, 0] + hs_ref[1, 0]
    hb = jnp.where(first, hs_ref[0, 1] + hs_ref[1, 1], hs_ref[0, 1] - hs_ref[1, 1])
    pa = jnp.where(first, ua * ha, ua * ha - ub * hb)
    pb = jnp.where(first, ub * hb, ua * hb + ub * ha)
    o_ref[0] = pa.astype(o_ref.dtype)
    o_ref[1] = pb.astype(o_ref.dtype)


def _dft_mul(fwd, u, hspec, *, tm=512, tn=512):
    b, l, d = u.shape
    tm, tn = min(tm, l), min(tn, d)
    nb = l // tm
    out = pl.pallas_call(
        _dft_mul_kernel,
        grid=(nb, d // tn, b),
        in_specs=[pl.BlockSpec((tm, l), lambda i, j, bi: (i, 0)),
                  pl.BlockSpec((tm, l), lambda i, j, bi: (i + nb, 0)),
                  pl.BlockSpec((None, l, tn), lambda i, j, bi: (bi, 0, j)),
                  pl.BlockSpec((2, 2, tm, tn), lambda i, j, bi: (0, 0, i, j))],
        out_specs=pl.BlockSpec((None, 2, tm, tn), lambda i, j, bi: (bi, 0, i, j)),
        out_shape=jax.ShapeDtypeStruct((b, 2, l, d), BF16),
        compiler_params=_cparams("parallel", "parallel", "parallel"),
        name="hyena_dft_forward",
    )(fwd, fwd, u, hspec)
    return out.reshape(b, 2 * l, d)


def _idft_kernel(g_ref, p_ref, u_ref, x0_ref, skip_ref, o_ref):
    y = _dot(g_ref[...], p_ref[...]) + u_ref[...].astype(F32) * skip_ref[...]
    o_ref[...] = (y * x0_ref[...].astype(F32)).astype(o_ref.dtype)


def _idft_gate(inv, p, u, x0, skip, *, tm=1024, tn=512):
    b, l, d = u.shape
    tm, tn = min(tm, l), min(tn, d)
    return pl.pallas_call(
        _idft_kernel,
        grid=(l // tm, b, d // tn),
        in_specs=[pl.BlockSpec((tm, 2 * l), lambda i, bi, j: (i, 0)),
                  pl.BlockSpec((None, 2 * l, tn), lambda i, bi, j: (bi, 0, j)),
                  pl.BlockSpec((None, tm, tn), lambda i, bi, j: (bi, i, j)),
                  pl.BlockSpec((None, tm, tn), lambda i, bi, j: (bi, i, j)),
                  pl.BlockSpec((1, tn), lambda i, bi, j: (0, j))],
        out_specs=pl.BlockSpec((None, tm, tn), lambda i, bi, j: (bi, i, j)),
        out_shape=jax.ShapeDtypeStruct((b, l, d), BF16),
        compiler_params=_cparams("parallel", "parallel", "parallel"),
        name="hyena_dft_inverse",
    )(inv, p, u, x0, skip.reshape(1, d))


def _hyena_operator(z, conv_w, conv_b, fparams, skip):
    b, l, d3 = z.shape
    d = d3 // 3
    u, x0 = _hyena_gate(z, conv_w, conv_b)
    taps = _hyena_filter(l, d, *fparams)
    fwd, inv = _dft_tables(l)
    hspec = _dft_raw(fwd, taps)
    p = _dft_mul(fwd, u, hspec)
    return _idft_gate(inv, p, u, x0, skip)


def _router_kernel(x_ref, nw_ref, sh_ref, sc_ref, wr_ref, h_ref, lg_ref):
    h = _norm_mod(x_ref[...], nw_ref[...], sh_ref[...], sc_ref[...])
    hh, hl = _split_bf16(h)
    h_ref[...] = hh
    wh, wl = _split_bf16(wr_ref[...])
    r = _dot(hh, jnp.concatenate([wh, wl], axis=1))
    lg_ref[...] = r[:, :LANE] + r[:, LANE:] + _dot(hl, wh)


def _norm_mod_router(x, nw, sh, sc, w_router, *, tm=512):
    b, l, d = x.shape
    n_e = w_router.shape[1]
    e = LANE
    w_router = jnp.pad(w_router.astype(F32), ((0, 0), (0, e - n_e)))
    tm = min(tm, l)
    per_sample = sh.shape[0] != 1
    mod_map = (lambda bi, i: (bi, 0, 0)) if per_sample else (lambda bi, i: (0, 0, 0))
    return pl.pallas_call(
        _router_kernel,
        grid=(b, l // tm),
        in_specs=[
            pl.BlockSpec((None, tm, d), lambda bi, i: (bi, i, 0)),
            pl.BlockSpec((1, d), lambda bi, i: (0, 0)),
            pl.BlockSpec((None, 1, d), mod_map),
            pl.BlockSpec((None, 1, d), mod_map),
            pl.BlockSpec((d, e), lambda bi, i: (0, 0)),
        ],
        out_specs=[pl.BlockSpec((None, tm, d), lambda bi, i: (bi, i, 0)),
                   pl.BlockSpec((None, tm, e), lambda bi, i: (bi, i, 0))],
        out_shape=[jax.ShapeDtypeStruct((b, l, d), BF16), jax.ShapeDtypeStruct((b, l, e), F32)],
        compiler_params=_cparams("parallel", "parallel"),
        name="norm_mod_router",
    )(x, nw.reshape(1, d), sh, sc, w_router)


def _route_kernel(lg_ref, tri_ref, pos_ref, prob_ref, *, cap):
    lg = lg_ref[...]
    m = jnp.max(lg, axis=0, keepdims=True)
    ex = jnp.exp(lg - m)
    probs = ex / jnp.sum(ex, axis=0, keepdims=True)
    bits = lax.bitcast_convert_type(probs, jnp.int32)

    def count(mask):
        return jnp.sum(jnp.where(mask, 1.0, 0.0), axis=1, keepdims=True)

    def step(i, thr):
        trial = thr | lax.shift_left(jnp.int32(1), 30 - i)
        return jnp.where(count(bits >= trial) >= cap, trial, thr)

    thr = lax.fori_loop(0, 31, step, jnp.zeros((lg.shape[0], 1), jnp.int32))
    gt = bits > thr
    eq = bits == thr
    need = cap - count(gt).astype(jnp.int32)
    both = jnp.concatenate([jnp.where(gt, 1.0, 0.0), jnp.where(eq, 1.0, 0.0)], axis=0).astype(BF16)
    csum = _dot(both, tri_ref[...])
    e = lg.shape[0]
    rank_gt = csum[:e].astype(jnp.int32)
    rank_eq = csum[e:].astype(jnp.int32)
    sel = jnp.logical_or(gt, jnp.logical_and(eq, rank_eq < need))
    pos = rank_gt + jnp.minimum(rank_eq, need)
    pos_ref[...] = jnp.where(sel, pos, -1)
    prob_ref[...] = probs


def _route(logits_t, cap):
    b, e, n = logits_t.shape
    idx = jnp.arange(n, dtype=jnp.int32)
    tri = (idx[:, None] < idx[None, :]).astype(BF16)
    return pl.pallas_call(
        functools.partial(_route_kernel, cap=cap),
        grid=(b,),
        in_specs=[pl.BlockSpec((None, e, n), lambda bi: (bi, 0, 0)),
                  pl.BlockSpec((n, n), lambda bi: (0, 0))],
        out_specs=[pl.BlockSpec((None, e, n), lambda bi: (bi, 0, 0))] * 2,
        out_shape=[jax.ShapeDtypeStruct((b, e, n), jnp.int32), jax.ShapeDtypeStruct((b, e, n), F32)],
        compiler_params=_cparams("parallel"),
        name="expert_choice_route",
    )(logits_t, tri)


def _gather_kernel(h_ref, pos_ref, prob_ref, xg_ref, gate_ref, *, cap):
    n = h_ref.shape[0]
    slot = lax.broadcasted_iota(jnp.int32, (cap, n), 0)
    match = slot == pos_ref[...]
    onehot = jnp.where(match, 1.0, 0.0).astype(BF16)
    xg_ref[...] = _dot(onehot, h_ref[...]).astype(xg_ref.dtype)
    gate_ref[...] = jnp.sum(jnp.where(match, prob_ref[...], 0.0), axis=1, keepdims=True)


def _gather_append_kernel(h_ref, pos_ref, prob_ref, xin_ref, gin_ref, xg_ref, gate_ref, *, cap, nb):
    bi = pl.program_id(0)

    @pl.when(bi < nb)
    def _():
        _gather_kernel(h_ref, pos_ref, prob_ref, xg_ref, gate_ref, cap=cap)

    @pl.when(bi >= nb)
    def _():
        xg_ref[...] = xin_ref[...]
        gate_ref[...] = gin_ref[...]


def _moe_gather(h, pos, probs, cap, tail=None):
    b, n, d = h.shape
    e = pos.shape[1]
    row = lambda a: a.reshape(b, e, 1, n)
    if tail is None:
        nx, kern, clamp = 0, functools.partial(_gather_kernel, cap=cap), lambda bi: bi
        tail_specs, tail_args = [], []
    else:
        r = tail[0].shape[1]
        assert r % cap == 0
        nx, kern = r // cap, functools.partial(_gather_append_kernel, cap=cap, nb=b)
        clamp = lambda bi: jnp.minimum(bi, b - 1)
        tail_map = lambda bi, ei: (ei, jnp.maximum(bi - b, 0), 0)
        tail_specs = [pl.BlockSpec((None, cap, d), tail_map), pl.BlockSpec((None, cap, 1), tail_map)]
        tail_args = list(tail)
    m_total = (b + nx) * cap
    return pl.pallas_call(
        kern,
        grid=(b + nx, e),
        in_specs=[pl.BlockSpec((None, n, d), lambda bi, ei: (clamp(bi), 0, 0)),
                  pl.BlockSpec((None, None, 1, n), lambda bi, ei: (clamp(bi), ei, 0, 0)),
                  pl.BlockSpec((None, None, 1, n), lambda bi, ei: (clamp(bi), ei, 0, 0))] + tail_specs,
        out_specs=[pl.BlockSpec((None, cap, d), lambda bi, ei: (ei, bi, 0)),
                   pl.BlockSpec((None, cap, 1), lambda bi, ei: (ei, bi, 0))],
        out_shape=[jax.ShapeDtypeStruct((e, m_total, d), BF16), jax.ShapeDtypeStruct((e, m_total, 1), F32)],
        compiler_params=_cparams("parallel", "parallel"),
        name="moe_gather",
    )(h, row(pos), row(probs), *tail_args)


def _ffn_kernel(x_ref, g_ref, wg_ref, wu_ref, wd_ref, o_ref, acc_ref):
    f = pl.program_id(2)
    x = x_ref[...]
    a = _dot(x, wg_ref[...].astype(BF16))
    u = _dot(x, wu_ref[...].astype(BF16))
    hmid = (_silu(a) * u * g_ref[...]).astype(BF16)
    y = _dot(hmid, wd_ref[...].astype(BF16))
    acc = jnp.where(f == 0, y, acc_ref[...] + y)
    acc_ref[...] = acc
    o_ref[...] = acc.astype(o_ref.dtype)


def _moe_ffn(xg, gate, w_gate, w_up, w_down, layer, *, tf=256):
    e, m, d = xg.shape
    ff = w_gate.shape[3]
    tm = max(t for t in range(16, FFN_MAX_ROWS + 1, 16) if m % t == 0)
    return pl.pallas_call(
        _ffn_kernel,
        grid=(e, m // tm, ff // tf),
        in_specs=[pl.BlockSpec((None, tm, d), lambda ei, i, f: (ei, i, 0)),
                  pl.BlockSpec((None, tm, 1), lambda ei, i, f: (ei, i, 0)),
                  pl.BlockSpec((None, None, d, tf), lambda ei, i, f: (layer, ei, 0, f)),
                  pl.BlockSpec((None, None, d, tf), lambda ei, i, f: (layer, ei, 0, f)),
                  pl.BlockSpec((None, None, tf, d), lambda ei, i, f: (layer, ei, f, 0))],
        out_specs=pl.BlockSpec((None, tm, d), lambda ei, i, f: (ei, i, 0)),
        out_shape=jax.ShapeDtypeStruct((e, m, d), BF16),
        scratch_shapes=[pltpu.VMEM((tm, d), F32)],
        compiler_params=_cparams("parallel", "parallel", "arbitrary", vmem=FFN_VMEM_LIMIT),
        name="moe_expert_ffn",
    )(xg, gate, w_gate, w_up, w_down)


def _combine_kernel(post_ref, y_ref, r_ref, g_ref, *refs, cap, tail):
    if tail == "final":
        fw_ref, o_ref, pt_scr = refs
    elif tail == "next":
        nw_ref, sh_ref, sc_ref, o_ref, h_ref, pt_scr = refs
    else:
        o_ref, pt_scr = refs
    j = pl.program_id(2)
    tm, e = post_ref.shape

    @pl.when(j == 0)
    def _():
        post = post_ref[...]
        if cap % LANE == 0:
            slot = lax.broadcasted_iota(jnp.int32, (tm, cap), 1)
            for ei in range(e):
                pt_scr[:, ei * cap:(ei + 1) * cap] = jnp.where(post[:, ei:ei + 1] == slot, 1.0, 0.0).astype(BF16)
        else:
            slot = lax.broadcasted_iota(jnp.int32, (tm, e * cap), 1)
            hit = jnp.zeros((tm, e * cap), jnp.bool_)
            for ei in range(e):
                tgt = jnp.where(post[:, ei:ei + 1] >= 0, post[:, ei:ei + 1] + ei * cap, -1)
                hit = jnp.logical_or(hit, tgt == slot)
            pt_scr[...] = jnp.where(hit, 1.0, 0.0).astype(BF16)

    y = y_ref[...].reshape(e * cap, y_ref.shape[2])
    out = r_ref[...] + g_ref[...] * _dot(pt_scr[...], y)
    if tail == "final":
        out = out * lax.rsqrt(jnp.mean(out * out, axis=-1, keepdims=True) + EPS) * fw_ref[...]
    o_ref[...] = out
    if tail == "next":
        h_ref[...] = _norm_mod(out, nw_ref[...], sh_ref[...], sc_ref[...]).astype(h_ref.dtype)


def _moe_combine(pos_t, y, row0, resid, gate, cap, final_w=None, next_norm=None, *, tm=1024, tn=512):
    b, n, e = pos_t.shape
    d = y.shape[2]
    assert final_w is None or next_norm is None
    tail = "final" if final_w is not None else "next" if next_norm is not None else None
    if tail:
        tm, tn = 256, d
    tm = min(tm, n)
    blk0 = row0 // cap
    per_sample = gate.shape[0] != 1
    g_map = (lambda bi, i, j: (bi, 0, j)) if per_sample else (lambda bi, i, j: (0, 0, j))
    row_spec = pl.BlockSpec((1, tn), lambda bi, i, j: (0, j))
    out_spec = pl.BlockSpec((None, tm, tn), lambda bi, i, j: (bi, i, j))
    out_specs, out_shape = out_spec, jax.ShapeDtypeStruct((b, n, d), F32)
    extra_specs, extra_args = [], []
    if tail == "final":
        extra_specs, extra_args = [row_spec], [final_w.astype(F32).reshape(1, d)]
    elif tail == "next":
        mod_spec = pl.BlockSpec((None, 1, tn), lambda bi, i, j: (bi, 0, j))
        extra_specs = [row_spec, mod_spec, mod_spec]
        extra_args = [next_norm[0].astype(F32).reshape(1, d), next_norm[1], next_norm[2]]
        out_specs, out_shape = [out_spec, out_spec], [out_shape, jax.ShapeDtypeStruct((b, n, d), BF16)]
    return pl.pallas_call(
        functools.partial(_combine_kernel, cap=cap, tail=tail),
        grid=(b, n // tm, d // tn),
        in_specs=[pl.BlockSpec((None, tm, e), lambda bi, i, j: (bi, i, 0)),
                  pl.BlockSpec((e, cap, tn), lambda bi, i, j: (0, blk0 + bi, j)),
                  out_spec,
                  pl.BlockSpec((None, 1, tn), g_map)] + extra_specs,
        out_specs=out_specs,
        out_shape=out_shape,
        scratch_shapes=[pltpu.VMEM((tm, e * cap), BF16)],
        compiler_params=_cparams("parallel", "parallel", "arbitrary", vmem=FFN_VMEM_LIMIT if tail else VMEM_LIMIT),
        name="moe_combine_residual",
    )(pos_t, y, resid, gate, *extra_args)


def _moe_block(streams, nw, w_router, w_gate, w_up, w_down, layer, final_w=None, next_norm=None):
    e = w_router.shape[1]
    caps = [CAPACITY_FACTOR * s[0].shape[1] // e for s in streams]
    rows = [s[0].shape[0] * cap for s, cap in zip(streams, caps)]
    row0 = [sum(rows[:t]) for t in range(len(streams))]
    routed, buffers = [None] * len(streams), None
    for t in reversed(range(len(streams))):
        x, sh, sc, _ = streams[t]
        h, logits = _norm_mod_router(x, nw, sh, sc, w_router)
        pos, probs = _route(jnp.swapaxes(logits[..., :e], 1, 2), caps[t])
        buffers = _moe_gather(h, pos, probs, caps[t], buffers)
        routed[t] = jnp.swapaxes(pos, 1, 2)
    y = _moe_ffn(buffers[0], buffers[1], w_gate, w_up, w_down, layer)
    return [_moe_combine(routed[t], y, row0[t], streams[t][0], streams[t][3], caps[t],
                         final_w if t == 0 else None, next_norm if t == 0 else None)
            for t in range(len(streams))]


def _rope_tables(seq_len, dim):
    rows = seq_len // GRID_W
    row_id = jnp.repeat(jnp.arange(rows, dtype=F32), GRID_W)
    col_id = jnp.tile(jnp.arange(GRID_W, dtype=F32), rows)
    nf = dim // 4
    inv = ROPE_THETA ** (-jnp.arange(nf, dtype=F32) / nf)
    ang_r = row_id[:, None] * inv
    ang_c = col_id[:, None] * inv
    cos = jnp.concatenate([jnp.cos(ang_r)] * 2 + [jnp.cos(ang_c)] * 2, axis=-1)
    sin = jnp.concatenate([-jnp.sin(ang_r), jnp.sin(ang_r), -jnp.sin(ang_c), jnp.sin(ang_c)], axis=-1)
    return cos, sin


def _identity_tables(seq_len, dim):
    return jnp.ones((seq_len, dim), F32), jnp.zeros((seq_len, dim), F32)


def _attention_mixer(x, ctx, nw, mods_lat, mods_ctx, w_qkv, q_norm, k_norm, w_o, layer, with_ctx_out, h_lat=None):
    hd = ATTN_HEAD_DIM
    nq, nkv = ATTN_HEADS * hd, ATTN_KV_HEADS * hd
    l, lc = x.shape[1], ctx.shape[1]
    tn = 512
    colw = jnp.concatenate([jnp.tile(q_norm.astype(F32) * (hd ** -0.5), ATTN_HEADS),
                            jnp.tile(k_norm.astype(F32), ATTN_KV_HEADS), jnp.ones((nkv,), F32)]).reshape(1, -1)
    cos, sin = _rope_tables(l, hd)
    icos, isin = _identity_tables(lc, hd)
    kw = dict(colw=colw, rope_blocks=(nq + nkv) // tn, head_norm=True, half=hd // 4, tn=tn)
    p_lat = _project(x, nw, mods_lat[0], mods_lat[1], w_qkv, layer, h=h_lat, cos=cos, sin=sin, **kw)
    off = 0 if with_ctx_out else nq // tn
    p_ctx = _project(ctx, nw, mods_ctx[0], mods_ctx[1], w_qkv, layer, cos=icos, sin=isin, col_off=off, **kw)
    kc0 = nq - off * tn
    ctx_kv = (p_ctx, kc0, kc0 + nkv)
    o_lat = _attention(p_lat, [ctx_kv, (p_lat, nq, nq + nkv)])
    x = _out_project(o_lat, w_o, layer, x, mods_lat[2])
    if with_ctx_out:
        ctx = _out_project(_attention(p_ctx, [ctx_kv]), w_o, layer, ctx, mods_ctx[2])
    return x, ctx


def _retention_mixer(x, ctx, nw, mods_lat, mods_ctx, w_in, decay_logit, w_o, layer, with_ctx_out, h_lat=None):
    dk, hh = RET_QK_DIM, RET_HEADS
    l, lc = x.shape[1], ctx.shape[1]
    tn = 512
    log_g = jax.nn.log_sigmoid(decay_logit.astype(F32))
    colw = jnp.concatenate([jnp.ones((hh * dk,), F32), jnp.full((hh * dk,), dk ** -0.5, F32),
                            jnp.ones((w_in.shape[2] - 2 * hh * dk,), F32)]).reshape(1, -1)
    rope_blocks = 2 * hh * dk // tn
    cos, sin = _rope_tables(l, dk)
    icos, isin = _identity_tables(lc, dk)
    kw = dict(colw=colw, rope_blocks=rope_blocks, head_norm=False, half=dk // 4, tn=tn)
    p_lat = _project(x, nw, mods_lat[0], mods_lat[1], w_in, layer, h=h_lat, cos=cos, sin=sin, **kw)
    p_ctx = _project(ctx, nw, mods_ctx[0], mods_ctx[1], w_in, layer, cos=icos, sin=isin, **kw)
    r_ctx, r_lat = _retention(log_g, p_ctx, p_lat, with_ctx_out)
    x = _out_project(r_lat, w_o, layer, x, mods_lat[2])
    if with_ctx_out:
        ctx = _out_project(r_ctx, w_o, layer, ctx, mods_ctx[2])
    return x, ctx


def _hyena_mixer(x, ctx, nw, mods_lat, mods_ctx, w_in, conv_w, conv_b, fparams, skip, w_out, layer, with_ctx_out,
                 h_lat=None):
    z = _project(x, nw, mods_lat[0], mods_lat[1], w_in, layer, h=h_lat)
    x = _out_project(_hyena_operator(z, conv_w, conv_b, fparams, skip), w_out, layer, x, mods_lat[2])
    if with_ctx_out:
        zc = _project(ctx, nw, mods_ctx[0], mods_ctx[1], w_in, layer)
        ctx = _out_project(_hyena_operator(zc, conv_w, conv_b, fparams, skip), w_out, layer, ctx, mods_ctx[2])
    return x, ctx


def kernel(x, c, ctx, c_ctx, w_mod, b_mod, norm_w, attn_w_qkv, attn_q_norm, attn_k_norm, attn_w_o, ret_w_in, ret_decay_logit, ret_w_o, hy_w_in, hy_conv_w, hy_conv_b, hy_f_w1, hy_f_b1, hy_f_freq1, hy_f_w2, hy_f_b2, hy_f_freq2, hy_f_w3, hy_skip, hy_w_out, moe_router, moe_w_gate, moe_w_up, moe_w_down, final_norm_w):
    depth = w_mod.shape[0]
    b, _, d = x.shape
    rows = -(-(b + 1) // 8) * 8
    c_all = jnp.concatenate([c, c_ctx[None, :], jnp.zeros((rows - b - 1, d), F32)], axis=0)
    mod = _modulation(c_all, w_mod, b_mod)

    def layer_mods(i):
        m = mod[i].reshape(rows, N_MOD, d)
        lat = [m[:b, t][:, None, :] for t in range(N_MOD)]
        ctx_ = [m[b:b + 1, t][:, None, :] for t in range(N_MOD)]
        return lat, ctx_

    h_lat = None
    for i in range(depth):
        kind, j = i % 3, i // 3
        last = i == depth - 1
        with_ctx = not last
        mods_lat, mods_ctx = layer_mods(i)
        nw1, nw2 = norm_w[i, 0], norm_w[i, 1]
        if kind == 0:
            x, ctx = _attention_mixer(x, ctx, nw1, mods_lat, mods_ctx, attn_w_qkv, attn_q_norm[j], attn_k_norm[j],
                                      attn_w_o, j, with_ctx, h_lat)
        elif kind == 1:
            x, ctx = _retention_mixer(x, ctx, nw1, mods_lat, mods_ctx, ret_w_in, ret_decay_logit[j], ret_w_o,
                                      j, with_ctx, h_lat)
        else:
            fparams = (hy_f_w1[j], hy_f_b1[j], hy_f_freq1[j], hy_f_w2[j], hy_f_b2[j], hy_f_freq2[j], hy_f_w3[j])
            x, ctx = _hyena_mixer(x, ctx, nw1, mods_lat, mods_ctx, hy_w_in, hy_conv_w[j], hy_conv_b[j], fparams,
                                  hy_skip[j], hy_w_out, j, with_ctx, h_lat)
        streams = [(x, mods_lat[3], mods_lat[4], mods_lat[5])]
        if with_ctx:
            streams.append((ctx, mods_ctx[3], mods_ctx[4], mods_ctx[5]))
        if last:
            tails = dict(final_w=final_norm_w)
        else:
            nxt = layer_mods(i + 1)[0]
            tails = dict(next_norm=(norm_w[i + 1, 0], nxt[0], nxt[1]))
        outs = _moe_block(streams, nw2, moe_router[i], moe_w_gate, moe_w_up, moe_w_down, i, **tails)
        if last:
            x = outs[0]
        else:
            (x, h_lat), ctx = outs
    return x
```
